```python
import math
import jax, jax.numpy as jnp
from jax import lax
import numpy as np

D_MODEL = 2048
BATCH = 4
SEQ = 2048
DEPTH = 1
DEC_BATCH = 128
DEC_SEQ = 8
PAST_LEN = 16384
PAGE_SIZE = 128

SSM_INNER = D_MODEL
SSM_HEAD_DIM = 64
SSM_HEADS = SSM_INNER // SSM_HEAD_DIM
SSM_STATE = 128
SSM_GROUPS = 4
SSM_HPG = SSM_HEADS // SSM_GROUPS
CONV_K = 4
CONV_DIM = SSM_INNER + 2 * SSM_GROUPS * SSM_STATE
SSM_CHUNK = 64
HG_DIM = D_MODEL
HG_HEAD_DIM = 128
HG_HEADS = HG_DIM // HG_HEAD_DIM
HG_CHUNK = 64
MOE_GROUPS = 4
EXPERTS_PER_GROUP = 8
N_EXPERTS = MOE_GROUPS * EXPERTS_PER_GROUP
TOP_K = 2
D_EXPERT = D_MODEL // 4
IN_DIM = SSM_INNER + CONV_DIM + SSM_HEADS + 4 * HG_DIM + 2 * D_MODEL
DEEPNORM_ALPHA = (2.0 * DEPTH) ** 0.25
DEEPNORM_BETA = (8.0 * DEPTH) ** -0.25
EPS = 1e-5

kernel_name = 'hybrid_ssd_hgrn2_hiermoe_deepnorm_step'


def _split_points():
    sizes = (SSM_INNER, CONV_DIM, SSM_HEADS, HG_DIM, HG_DIM, HG_DIM, HG_DIM, D_MODEL, D_MODEL)
    pts, acc = [], 0
    for s in sizes[:-1]:
        acc += s
        pts.append(acc)
    return pts


def _layer_norm(x, g, b):
    xf = x.astype(jnp.float32)
    mu = jnp.mean(xf, axis=-1, keepdims=True)
    var = jnp.mean(jnp.square(xf - mu), axis=-1, keepdims=True)
    return ((xf - mu) * lax.rsqrt(var + EPS) * g + b).astype(x.dtype)


def _group_rms_norm(x, g, n_groups):
    shp = x.shape
    xf = x.astype(jnp.float32).reshape(shp[:-1] + (n_groups, shp[-1] // n_groups))
    xf = xf * lax.rsqrt(jnp.mean(xf * xf, axis=-1, keepdims=True) + EPS)
    return xf.reshape(shp) * g


def _pad_time(a, pad):
    if pad == 0:
        return a
    return jnp.pad(a, [(0, 0), (0, pad)] + [(0, 0)] * (a.ndim - 2))


def _causal_depthwise_conv(u, prev, w, b):
    L = u.shape[1]
    up = jnp.concatenate([prev.astype(u.dtype), u], axis=1)
    y = b + sum(up[:, k:k + L] * w[k] for k in range(CONV_K))
    return y, up[:, L:]


def _ssd_chunked(xh, dt, a, bm, cm, h0):
    f32 = jnp.float32
    bsz, L = xh.shape[:2]
    c = min(SSM_CHUNK, L)
    pad = (-L) % c
    nz = (L + pad) // c
    xdt = _pad_time(xh.astype(f32) * dt[..., None], pad).reshape(bsz, nz, c, SSM_GROUPS, SSM_HPG, SSM_HEAD_DIM)
    loga = _pad_time(dt * a, pad).reshape(bsz, nz, c, SSM_GROUPS, SSM_HPG)
    B = _pad_time(bm.astype(f32), pad).reshape(bsz, nz, c, SSM_GROUPS, SSM_STATE)
    C = _pad_time(cm.astype(f32), pad).reshape(bsz, nz, c, SSM_GROUPS, SSM_STATE)
    cum = jnp.cumsum(loga, axis=2)
    causal = jnp.tril(jnp.ones((c, c), dtype=bool))
    seg = cum[:, :, :, None] - cum[:, :, None]
    decay = jnp.exp(jnp.where(causal[:, :, None, None], seg, -jnp.inf))
    cb = jnp.einsum('bzlgn,bzsgn->bzlsg', C, B)
    y_intra = jnp.einsum('bzlsgr,bzsgrp->bzlgrp', cb[..., None] * decay, xdt)
    to_end = jnp.exp(cum[:, :, -1:] - cum)
    chunk_states = jnp.einsum('bzsgn,bzsgr,bzsgrp->bzgrpn', B, to_end, xdt)
    chunk_decay = jnp.exp(cum[:, :, -1])

    def step(h, inp):
        dec, st = inp
        return dec[..., None, None] * h + st, h

    h_init = h0.astype(f32).reshape(bsz, SSM_GROUPS, SSM_HPG, SSM_HEAD_DIM, SSM_STATE)
    h_fin, h_starts = lax.scan(step, h_init, (jnp.moveaxis(chunk_decay, 1, 0), jnp.moveaxis(chunk_states, 1, 0)))
    h_starts = jnp.moveaxis(h_starts, 0, 1)
    y_inter = jnp.einsum('bzlgn,bzlgr,bzgrpn->bzlgrp', C, jnp.exp(cum), h_starts)
    y = (y_intra + y_inter).reshape(bsz, nz * c, SSM_HEADS, SSM_HEAD_DIM)[:, :L]
    return y, h_fin.reshape(bsz, SSM_HEADS, SSM_HEAD_DIM, SSM_STATE)


def _gla_chunked(q, k, v, logf, s0):
    f32 = jnp.float32
    bsz, L = q.shape[:2]
    c = min(HG_CHUNK, L)
    pad = (-L) % c
    nz = (L + pad) // c

    def blk(t):
        return _pad_time(t.astype(f32), pad).reshape(bsz, nz, c, HG_HEADS, t.shape[-1])

    q, k, v, logf = blk(q), blk(k), blk(v), blk(logf)
    bc = jnp.cumsum(logf, axis=2)
    qe = q * jnp.exp(bc)
    ke = k * jnp.exp(-bc)
    causal = jnp.tril(jnp.ones((c, c), dtype=bool))
    attn = jnp.where(causal, jnp.einsum('bzlhd,bzshd->bzhls', qe, ke), 0.0)
    o_intra = jnp.einsum('bzhls,bzshv->bzlhv', attn, v)
    kd = k * jnp.exp(bc[:, :, -1:] - bc)
    chunk_states = jnp.einsum('bzshd,bzshv->bzhdv', kd, v)
    chunk_decay = jnp.exp(bc[:, :, -1])

    def step(s, inp):
        dec, st = inp
        return dec[..., None] * s + st, s

    s_fin, s_starts = lax.scan(step, s0.astype(f32), (jnp.moveaxis(chunk_decay, 1, 0), jnp.moveaxis(chunk_states, 1, 0)))
    s_starts = jnp.moveaxis(s_starts, 0, 1)
    o_inter = jnp.einsum('bzlhd,bzhdv->bzlhv', qe, s_starts)
    o = (o_intra + o_inter).reshape(bsz, nz * c, HG_HEADS, HG_HEAD_DIM)[:, :L]
    return o, s_fin


def _hier_moe(x, router_g_w, router_g_b, router_e_w, router_e_b, exp_w1, exp_w3, exp_w2):
    shp = x.shape
    xt = x.reshape(-1, D_MODEL)
    T = xt.shape[0]
    g_logits = (xt @ router_g_w).astype(jnp.float32) + router_g_b.astype(jnp.float32)
    g_prob = jax.nn.softmax(g_logits, axis=-1)
    g_sel = jnp.argmax(g_logits, axis=-1)
    e_logits = ((xt @ router_e_w).astype(jnp.float32) + router_e_b.astype(jnp.float32)).reshape(T, MOE_GROUPS, EXPERTS_PER_GROUP)
    e_in_group = jnp.take_along_axis(e_logits, g_sel[:, None, None], axis=1)[:, 0]
    top_v, top_i = lax.top_k(e_in_group, TOP_K)
    p_exp = jax.nn.softmax(top_v, axis=-1)
    p_grp = jnp.take_along_axis(g_prob, g_sel[:, None], axis=1)
    eid = g_sel[:, None] * EXPERTS_PER_GROUP + top_i
    combine = jnp.sum(jax.nn.one_hot(eid, N_EXPERTS, dtype=jnp.float32) * (p_grp * p_exp)[..., None], axis=1)
    y = jnp.zeros((T, D_MODEL), jnp.float32)
    for e in range(N_EXPERTS):
        h = jax.nn.silu(xt @ exp_w1[e]) * (xt @ exp_w3[e])
        y = y + combine[:, e:e + 1] * (h @ exp_w2[e])
    return y.reshape(shp)


def _layer(x, ssm0, conv0, hg0, w_in, conv_w, conv_b, dt_bias, a_log, d_skip, ssm_norm_g, lb, hgrn_norm_g,
           w_a, w_b, w_out, ln1_g, ln1_b, router_g_w, router_g_b, router_e_w, router_e_b,
           exp_w1, exp_w3, exp_w2, ln2_g, ln2_b):
    f32 = jnp.float32
    bsz, L, _ = x.shape
    proj = jnp.einsum('bld,de->ble', x, w_in)
    z, xbc, dt_raw, hq, hf, hi, hgate, ga, gb = jnp.split(proj, _split_points(), axis=-1)
    xbc, conv_new = _causal_depthwise_conv(xbc, conv0, conv_w, conv_b)
    xbc = jax.nn.silu(xbc)
    xs, bm, cm = jnp.split(xbc, [SSM_INNER, SSM_INNER + SSM_GROUPS * SSM_STATE], axis=-1)
    xh = xs.reshape(bsz, L, SSM_HEADS, SSM_HEAD_DIM)
    bm = bm.reshape(bsz, L, SSM_GROUPS, SSM_STATE)
    cm = cm.reshape(bsz, L, SSM_GROUPS, SSM_STATE)
    dt = jax.nn.softplus(dt_raw.astype(f32) + dt_bias.astype(f32))
    a = -jnp.exp(a_log.astype(f32))
    y, ssm_new = _ssd_chunked(xh, dt, a, bm, cm, ssm0)
    y = y + d_skip.astype(f32)[:, None] * xh.astype(f32)
    y_ssm = _group_rms_norm(y.reshape(bsz, L, SSM_INNER) * jax.nn.silu(z.astype(f32)), ssm_norm_g, SSM_GROUPS)
    f = lb + (1.0 - lb) * jax.nn.sigmoid(hf.astype(f32))
    hshape = (bsz, L, HG_HEADS, HG_HEAD_DIM)
    o, hg_new = _gla_chunked(jax.nn.silu(hq).reshape(hshape), (1.0 - f).reshape(hshape),
                             hi.reshape(hshape), jnp.log(f).reshape(hshape), hg0)
    y_hg = _group_rms_norm(o.reshape(bsz, L, HG_DIM), hgrn_norm_g, HG_HEADS) * jax.nn.silu(hgate.astype(f32))
    merged = jax.nn.sigmoid(ga) * (y_ssm @ w_a) + jax.nn.sigmoid(gb) * (y_hg @ w_b)
    mix = merged @ w_out
    x1 = _layer_norm(DEEPNORM_ALPHA * x + mix, ln1_g, ln1_b)
    moe = _hier_moe(x1, router_g_w, router_g_b, router_e_w, router_e_b, exp_w1, exp_w3, exp_w2)
    x2 = _layer_norm(DEEPNORM_ALPHA * x1 + moe, ln2_g, ln2_b)
    return x2.astype(x.dtype), ssm_new, conv_new, hg_new


def setup_inputs(seed: int = 0) -> dict:
    key = jax.random.key(seed)
    ks = jax.random.split(key, 32)
    f32 = jnp.float32

    def nrm(k, shape, scale):
        return jax.random.normal(k, shape, f32) * scale

    dt0 = jnp.exp(jax.random.uniform(ks[8], (DEPTH, SSM_HEADS), f32, minval=math.log(1e-3), maxval=math.log(1e-1)))
    return {
        'x_prompt': nrm(ks[0], (BATCH, SEQ, D_MODEL), 1.0),
        'x_sample': nrm(ks[1], (DEC_BATCH, DEC_SEQ, D_MODEL), 1.0),
        'state_ssm': nrm(ks[2], (DEPTH, DEC_BATCH, SSM_HEADS, SSM_HEAD_DIM, SSM_STATE), 0.1),
        'state_conv': nrm(ks[3], (DEPTH, DEC_BATCH, CONV_K - 1, CONV_DIM), 1.0),
        'state_hgrn': nrm(ks[4], (DEPTH, DEC_BATCH, HG_HEADS, HG_HEAD_DIM, HG_HEAD_DIM), 0.3),
        'w_in': nrm(ks[5], (DEPTH, D_MODEL, IN_DIM), D_MODEL ** -0.5),
        'conv_w': nrm(ks[6], (DEPTH, CONV_K, CONV_DIM), CONV_K ** -0.5),
        'conv_b': nrm(ks[7], (DEPTH, CONV_DIM), 0.02),
        'dt_bias': dt0 + jnp.log(-jnp.expm1(-dt0)),
        'a_log': jnp.log(jax.random.uniform(ks[9], (DEPTH, SSM_HEADS), f32, minval=1.0, maxval=16.0)),
        'd_skip': 1.0 + nrm(ks[10], (DEPTH, SSM_HEADS), 0.1),
        'ssm_norm_g': 1.0 + nrm(ks[11], (DEPTH, SSM_INNER), 0.05),
        'hgrn_lb_logits': 1.0 + nrm(ks[12], (DEPTH + 1, HG_DIM), 0.1),
        'hgrn_norm_g': 1.0 + nrm(ks[13], (DEPTH, HG_DIM), 0.05),
        'w_a': nrm(ks[14], (DEPTH, SSM_INNER, D_MODEL), SSM_INNER ** -0.5 * DEEPNORM_BETA),
        'w_b': nrm(ks[15], (DEPTH, HG_DIM, D_MODEL), HG_DIM ** -0.5 * DEEPNORM_BETA),
        'w_out': nrm(ks[16], (DEPTH, D_MODEL, D_MODEL), D_MODEL ** -0.5 * DEEPNORM_BETA),
        'ln1_g': 1.0 + nrm(ks[17], (DEPTH, D_MODEL), 0.05),
        'ln1_b': nrm(ks[18], (DEPTH, D_MODEL), 0.02),
        'router_g_w': nrm(ks[19], (DEPTH, D_MODEL, MOE_GROUPS), D_MODEL ** -0.5),
        'router_g_b': nrm(ks[20], (DEPTH, MOE_GROUPS), 0.01),
        'router_e_w': nrm(ks[21], (DEPTH, D_MODEL, N_EXPERTS), D_MODEL ** -0.5),
        'router_e_b': nrm(ks[22], (DEPTH, N_EXPERTS), 0.01),
        'exp_w1': nrm(ks[23], (DEPTH, N_EXPERTS, D_MODEL, D_EXPERT), D_MODEL ** -0.5),
        'exp_w3': nrm(ks[24], (DEPTH, N_EXPERTS, D_MODEL, D_EXPERT), D_MODEL ** -0.5),
        'exp_w2': nrm(ks[25], (DEPTH, N_EXPERTS, D_EXPERT, D_MODEL), D_EXPERT ** -0.5 * DEEPNORM_BETA),
        'ln2_g': 1.0 + nrm(ks[26], (DEPTH, D_MODEL), 0.05),
        'ln2_b': nrm(ks[27], (DEPTH, D_MODEL), 0.02),
    }


def reference(x_prompt, x_sample, state_ssm, state_conv, state_hgrn, w_in, conv_w, conv_b, dt_bias, a_log,
              d_skip, ssm_norm_g, hgrn_lb_logits, hgrn_norm_g, w_a, w_b, w_out, ln1_g, ln1_b,
              router_g_w, router_g_b, router_e_w, router_e_b, exp_w1, exp_w3, exp_w2, ln2_g, ln2_b):
    f32 = jnp.float32
    lb_all = jnp.cumsum(jax.nn.softmax(hgrn_lb_logits.astype(f32), axis=0), axis=0)
    nb = x_prompt.shape[0]
    hp, hs = x_prompt, x_sample
    ssm_p, conv_p, hg_p, ssm_s, conv_s, hg_s = [], [], [], [], [], []
    for l in range(DEPTH):
        params = (w_in[l], conv_w[l], conv_b[l], dt_bias[l], a_log[l], d_skip[l], ssm_norm_g[l], lb_all[l],
                  hgrn_norm_g[l], w_a[l], w_b[l], w_out[l], ln1_g[l], ln1_b[l], router_g_w[l], router_g_b[l],
                  router_e_w[l], router_e_b[l], exp_w1[l], exp_w3[l], exp_w2[l], ln2_g[l], ln2_b[l])
        z_ssm = jnp.zeros((nb, SSM_HEADS, SSM_HEAD_DIM, SSM_STATE), f32)
        z_conv = jnp.zeros((nb, CONV_K - 1, CONV_DIM), x_prompt.dtype)
        z_hg = jnp.zeros((nb, HG_HEADS, HG_HEAD_DIM, HG_HEAD_DIM), f32)
        hp, a1, a2, a3 = _layer(hp, z_ssm, z_conv, z_hg, *params)
        hs, b1, b2, b3 = _layer(hs, state_ssm[l], state_conv[l], state_hgrn[l], *params)
        ssm_p.append(a1); conv_p.append(a2); hg_p.append(a3)
        ssm_s.append(b1); conv_s.append(b2); hg_s.append(b3)
    return (hp, hs, jnp.stack(ssm_p), jnp.stack(conv_p), jnp.stack(hg_p),
            jnp.stack(ssm_s), jnp.stack(conv_s), jnp.stack(hg_s))
```

```python
import functools

import jax
import jax.numpy as jnp
from jax import lax
from jax.experimental import pallas as pl
from jax.experimental.pallas import tpu as pltpu

F32 = jnp.float32
BF16 = jnp.bfloat16
HI = lax.Precision.HIGHEST

D = 2048
SSM_HEADS = 32
SSM_P = 64
SSM_N = 128
SSM_G = 4
GW = D // SSM_G
BCW = SSM_G * SSM_N
CONV_K = 4
HG_HEADS = 16
HG_DK = 128
N_GROUPS = 4
EPG = 8
N_EXPERTS = 32
D_EXPERT = 512
EPS = 1e-5
ALPHA = 2.0 ** 0.25
NEG = -1e30

VMEM_LIMIT = 56 * 1024 * 1024
SSD_CHUNK = 128
GLA_CHUNK = 64
MOE_TM = 256
ROW_TM = 256
CARRY = 8


def _silu(x):
    return x * (1.0 / (1.0 + jnp.exp(-x)))


def _sigmoid(x):
    return 1.0 / (1.0 + jnp.exp(-x))


def _softplus(x):
    return jnp.maximum(x, 0.0) + jnp.log(1.0 + jnp.exp(-jnp.abs(x)))


def _tril(c):
    r = lax.broadcasted_iota(jnp.int32, (c, c), 0)
    s = lax.broadcasted_iota(jnp.int32, (c, c), 1)
    return r >= s


def _mm_kernel(x_ref, w_ref, o_ref):
    o_ref[...] = jnp.dot(x_ref[...], w_ref[...], preferred_element_type=F32).astype(o_ref.dtype)


def _matmul(x, w, tm, tn, out_dtype):
    m, k = x.shape
    n = w.shape[1]
    return pl.pallas_call(
        _mm_kernel,
        grid=(m // tm, n // tn),
        in_specs=[pl.BlockSpec((tm, k), lambda i, j: (i, 0)),
                  pl.BlockSpec((k, tn), lambda i, j: (0, j))],
        out_specs=pl.BlockSpec((tm, tn), lambda i, j: (i, j)),
        out_shape=jax.ShapeDtypeStruct((m, n), out_dtype),
        compiler_params=pltpu.CompilerParams(
            dimension_semantics=("parallel", "arbitrary"), vmem_limit_bytes=VMEM_LIMIT),
        name="proj",
    )(x, w)


def _conv_silu(buf_ref, carry0, u, w_ref, b_ref, c, first):
    @pl.when(first)
    def _():
        buf_ref[0:CARRY, :] = carry0
    buf_ref[CARRY:CARRY + c, :] = u
    acc = b_ref[...] + w_ref[CONV_K - 1:CONV_K, :] * u
    for k in range(CONV_K - 1):
        off = CARRY - (CONV_K - 1) + k
        acc = acc + w_ref[k:k + 1, :] * buf_ref[off:off + c, :]
    tail = buf_ref[c:c + CARRY, :]
    buf_ref[0:CARRY, :] = tail
    return _silu(acc), tail


def _ssd_kernel(xp_ref, bcp_ref, z_ref, dtr_ref, dtrt_ref, h0_ref, cx0_ref, cbc0_ref,
                cwx_ref, cwbc_ref, cbx_ref, cbbc_ref, dtb_ref, alog_ref, dtbt_ref, alogt_ref,
                dskip_ref, normg_ref, e64_ref, e128_ref,
                y_ref, hout_ref, ctx_ref, ctbc_ref,
                xbuf, bcbuf, h_ref, *, c):
    zi = pl.program_id(1)
    first = zi == 0

    @pl.when(first)
    def _():
        h_ref[...] = h0_ref[0]

    xs, tail_x = _conv_silu(xbuf, cx0_ref[0], xp_ref[...], cwx_ref, cbx_ref, c, first)
    bc, tail_bc = _conv_silu(bcbuf, cbc0_ref[0], bcp_ref[...], cwbc_ref, cbbc_ref, c, first)
    ctx_ref[0] = tail_x
    ctbc_ref[0] = tail_bc

    a_row = -jnp.exp(alog_ref[...])
    dt = _softplus(dtr_ref[...] + dtb_ref[...])
    loga = dt * a_row
    dt_t = _softplus(dtrt_ref[0] + dtbt_ref[...])
    loga_t = dt_t * (-jnp.exp(alogt_ref[...]))
    causal = _tril(c)
    lmat = causal.astype(F32)
    umat = (lax.broadcasted_iota(jnp.int32, (c, c), 0) <= lax.broadcasted_iota(jnp.int32, (c, c), 1)).astype(F32)
    cum = jnp.dot(lmat, loga, precision=HI, preferred_element_type=F32)
    cum_t = jnp.dot(loga_t, umat, precision=HI, preferred_element_type=F32)
    dt_e = jnp.dot(dt, e64_ref[...], precision=HI, preferred_element_type=F32)
    cum_e = jnp.dot(cum, e64_ref[...], precision=HI, preferred_element_type=F32)
    cum_b = jnp.dot(cum, e128_ref[...], precision=HI, preferred_element_type=F32)
    last_e = cum_e[c - 1:c, :]
    xdt = xs * dt_e
    in_scale = jnp.exp(cum_e)
    to_end = jnp.exp(last_e - cum_e)
    chunk_dec = jnp.exp(cum_b[c - 1:c, :])
    xw = xdt * to_end
    lane = lax.broadcasted_iota(jnp.int32, (c, 128), 1)

    y_groups = []
    for g in range(SSM_G):
        bg = bc[:, g * SSM_N:(g + 1) * SSM_N]
        cg = bc[:, BCW + g * SSM_N:BCW + (g + 1) * SSM_N]
        cb = lax.dot_general(cg, bg, (((1,), (1,)), ((), ())), preferred_element_type=F32)
        rows = slice(g * GW, (g + 1) * GW)
        hg = h_ref[rows, :]
        y_inter = lax.dot_general(cg, hg, (((1,), (1,)), ((), ())), preferred_element_type=F32)
        y_g = y_inter * in_scale[:, rows]
        pieces = []
        for j in range(GW // 128):
            col0 = g * GW + j * 128
            xpair = xdt[:, col0:col0 + 128]
            acc = None
            for half in range(2):
                r = (col0 // SSM_P) + half
                seg = cum_b[:, r * 128:r * 128 + c] - cum_t[r:r + 1, :]
                dec = jnp.exp(jnp.where(causal, seg, NEG))
                m = cb * dec
                rhs = jnp.where(lane >= SSM_P if half else lane < SSM_P, xpair, 0.0)
                part = jnp.dot(m, rhs, preferred_element_type=F32)
                acc = part if acc is None else acc + part
            pieces.append(acc)
        y_g = y_g + jnp.concatenate(pieces, axis=1)
        y_groups.append(y_g)
        st = lax.dot_general(xw[:, rows], bg, (((0,), (0,)), ((), ())), preferred_element_type=F32)
        for r8 in range(GW // SSM_P):
            r = g * (GW // SSM_P) + r8
            hr = slice(r * SSM_P, (r + 1) * SSM_P)
            h_ref[hr, :] = (chunk_dec[:, r * 128:(r + 1) * 128] * h_ref[hr, :]
                            + st[r8 * SSM_P:(r8 + 1) * SSM_P, :])
    y = jnp.concatenate(y_groups, axis=1) + dskip_ref[...] * xs
    yz = y * _silu(z_ref[...])
    outs = []
    for g in range(SSM_G):
        blk = yz[:, g * GW:(g + 1) * GW]
        ms = jnp.sum(blk * blk, axis=1, keepdims=True) * (1.0 / GW)
        outs.append(blk * lax.rsqrt(ms + EPS))
    y_ref[...] = (jnp.concatenate(outs, axis=1) * normg_ref[...]).astype(y_ref.dtype)

    @pl.when(zi == pl.num_programs(1) - 1)
    def _():
        hout_ref[0] = h_ref[...]


def _ssd(proj, dt_raw, row0, h0, conv0, p, nb, seq, c):
    nz = seq // c
    t = nb * seq
    r0 = row0 // c
    dtrt = dt_raw[row0:row0 + t, :SSM_HEADS].reshape(nb * nz, c, SSM_HEADS).transpose(0, 2, 1)
    pad_rows = CARRY - (CONV_K - 1)
    cx0 = jnp.pad(conv0[:, :, :D], ((0, 0), (pad_rows, 0), (0, 0)))
    cbc0 = jnp.pad(conv0[:, :, D:], ((0, 0), (pad_rows, 0), (0, 0)))
    rowblk = lambda col: pl.BlockSpec((c, D), lambda b, z: (r0 + b * nz + z, col))
    const = lambda shape: pl.BlockSpec(shape, lambda b, z: tuple(0 for _ in shape))
    per_b = lambda shape: pl.BlockSpec((1,) + shape, lambda b, z: (b, 0, 0))
    y, hout, ctx, ctbc = pl.pallas_call(
        functools.partial(_ssd_kernel, c=c),
        grid=(nb, nz),
        in_specs=[rowblk(1),
                  pl.BlockSpec((c, 2 * BCW), lambda b, z: (r0 + b * nz + z, 16)),
                  rowblk(0),
                  pl.BlockSpec((c, 128), lambda b, z: (r0 + b * nz + z, 0)),
                  pl.BlockSpec((1, SSM_HEADS, c), lambda b, z: (b * nz + z, 0, 0)),
                  per_b((D, SSM_N)), per_b((CARRY, D)), per_b((CARRY, 2 * BCW)),
                  const((CONV_K, D)), const((CONV_K, 2 * BCW)), const((1, D)), const((1, 2 * BCW)),
                  const((1, 128)), const((1, 128)), const((SSM_HEADS, c)), const((SSM_HEADS, c)),
                  const((1, D)), const((1, D)), const((128, D)), const((128, 2 * D))],
        out_specs=[pl.BlockSpec((c, D), lambda b, z: (b * nz + z, 0)),
                   per_b((D, SSM_N)), per_b((CARRY, D)), per_b((CARRY, 2 * BCW))],
        out_shape=[jax.ShapeDtypeStruct((t, D), BF16),
                   jax.ShapeDtypeStruct((nb, D, SSM_N), F32),
                   jax.ShapeDtypeStruct((nb, CARRY, D), F32),
                   jax.ShapeDtypeStruct((nb, CARRY, 2 * BCW), F32)],
        scratch_shapes=[pltpu.VMEM((c + CARRY, D), F32), pltpu.VMEM((c + CARRY, 2 * BCW), F32),
                        pltpu.VMEM((D, SSM_N), F32)],
        compiler_params=pltpu.CompilerParams(
            dimension_semantics=("parallel", "arbitrary"), vmem_limit_bytes=VMEM_LIMIT),
        name="ssd",
    )(proj, proj, proj, dt_raw, dtrt, h0, cx0, cbc0,
      p["cwx"], p["cwbc"], p["cbx"], p["cbbc"], p["dtb"], p["alog"],
      jnp.broadcast_to(p["dtb"][0, :SSM_HEADS, None], (SSM_HEADS, c)),
      jnp.broadcast_to(p["alog"][0, :SSM_HEADS, None], (SSM_HEADS, c)),
      p["dskip_e"], p["ssm_norm_g"], p["e64"], p["e128"])
    conv_new = jnp.concatenate([ctx[:, pad_rows:], ctbc[:, pad_rows:]], axis=-1)
    return y, hout, conv_new


def _gla_kernel(q_ref, f_ref, i_ref, g_ref, s0_ref, lb_ref, normg_ref, y_ref, sout_ref, s_ref, *, c):
    zi = pl.program_id(1)

    @pl.when(zi == 0)
    def _():
        s_ref[...] = s0_ref[0]

    lb = lb_ref[...]
    f = lb + (1.0 - lb) * _sigmoid(f_ref[...])
    k = 1.0 - f
    logf = jnp.log(f)
    q = _silu(q_ref[...])
    v = i_ref[...]
    causal = _tril(c)
    bc = jnp.dot(causal.astype(F32), logf, precision=HI, preferred_element_type=F32)
    last = bc[c - 1:c, :]
    qe = q * jnp.exp(bc)
    ke = k * jnp.exp(-bc)
    kd = k * jnp.exp(last - bc)
    chunk_dec = jnp.exp(last)
    gate = _silu(g_ref[...])
    outs = []
    for h in range(HG_HEADS):
        sl = slice(h * HG_DK, (h + 1) * HG_DK)
        qh, kh, vh = qe[:, sl], ke[:, sl], v[:, sl]
        attn = lax.dot_general(qh, kh, (((1,), (1,)), ((), ())), preferred_element_type=F32)
        attn = jnp.where(causal, attn, 0.0)
        sh = s_ref[sl, :]
        o = jnp.dot(attn, vh, preferred_element_type=F32) + jnp.dot(qh, sh, preferred_element_type=F32)
        st = lax.dot_general(kd[:, sl], vh, (((0,), (0,)), ((), ())), preferred_element_type=F32)
        dec_col = jnp.broadcast_to(chunk_dec[:, sl], (HG_DK, HG_DK)).T
        s_ref[sl, :] = dec_col * sh + st
        ms = jnp.sum(o * o, axis=1, keepdims=True) * (1.0 / HG_DK)
        outs.append(o * lax.rsqrt(ms + EPS))
    y_ref[...] = (jnp.concatenate(outs, axis=1) * normg_ref[...] * gate).astype(y_ref.dtype)

    @pl.when(zi == pl.num_programs(1) - 1)
    def _():
        sout_ref[0] = s_ref[...]


def _gla(proj, row0, s0, p, nb, seq, c):
    nz = seq // c
    t = nb * seq
    r0 = row0 // c
    rowblk = lambda col: pl.BlockSpec((c, D), lambda b, z: (r0 + b * nz + z, col))
    const = lambda shape: pl.BlockSpec(shape, lambda b, z: tuple(0 for _ in shape))
    per_b = lambda shape: pl.BlockSpec((1,) + shape, lambda b, z: (b, 0, 0))
    return pl.pallas_call(
        functools.partial(_gla_kernel, c=c),
        grid=(nb, nz),
        in_specs=[rowblk(2), rowblk(3), rowblk(4), rowblk(5), per_b((D, HG_DK)), const((1, D)), const((1, D))],
        out_specs=[pl.BlockSpec((c, D), lambda b, z: (b * nz + z, 0)), per_b((D, HG_DK))],
        out_shape=[jax.ShapeDtypeStruct((t, D), BF16), jax.ShapeDtypeStruct((nb, D, HG_DK), F32)],
        scratch_shapes=[pltpu.VMEM((D, HG_DK), F32)],
        compiler_params=pltpu.CompilerParams(
            dimension_semantics=("parallel", "arbitrary"), vmem_limit_bytes=VMEM_LIMIT),
        name="gla",
    )(proj, proj, proj, proj, s0, p["lb"], p["hgrn_norm_g"])


def _layer_norm(x, g, b):
    mu = jnp.mean(x, axis=1, keepdims=True)
    xc = x - mu
    var = jnp.mean(xc * xc, axis=1, keepdims=True)
    return xc * lax.rsqrt(var + EPS) * g + b


def _merge_kernel(ya_ref, yb_ref, ga_ref, gb_ref, x_ref, wa_ref, wb_ref, wo_ref, g1_ref, b1_ref,
                  wr_ref, br_ref, x1_ref, route_ref):
    a = jnp.dot(ya_ref[...], wa_ref[...], preferred_element_type=F32)
    b = jnp.dot(yb_ref[...], wb_ref[...], preferred_element_type=F32)
    merged = _sigmoid(ga_ref[...]) * a + _sigmoid(gb_ref[...]) * b
    mix = jnp.dot(merged.astype(BF16), wo_ref[...], preferred_element_type=F32)
    x1 = _layer_norm(ALPHA * x_ref[...] + mix, g1_ref[...], b1_ref[...])
    x1_ref[...] = x1
    logits = jnp.dot(x1, wr_ref[...], precision=HI, preferred_element_type=F32) + br_ref[...]
    lane = lax.broadcasted_iota(jnp.int32, logits.shape, 1).astype(F32)
    gl = jnp.where(lane < N_GROUPS, logits, NEG)
    gmax = jnp.max(gl, axis=1, keepdims=True)
    gsel = jnp.min(jnp.where(gl == gmax, lane, 1e9), axis=1, keepdims=True)
    p_grp = 1.0 / jnp.sum(jnp.exp(gl - gmax), axis=1, keepdims=True)
    lo = N_GROUPS + gsel * EPG
    el = jnp.where((lane >= lo) & (lane < lo + EPG), logits, NEG)
    v0 = jnp.max(el, axis=1, keepdims=True)
    i0 = jnp.min(jnp.where(el == v0, lane, 1e9), axis=1, keepdims=True)
    el2 = jnp.where(lane == i0, NEG, el)
    v1 = jnp.max(el2, axis=1, keepdims=True)
    i1 = jnp.min(jnp.where(el2 == v1, lane, 1e9), axis=1, keepdims=True)
    e1 = jnp.exp(v1 - v0)
    p0 = 1.0 / (1.0 + e1)
    p1 = e1 * p0
    route = jnp.where(lane == 0, i0 - N_GROUPS,
                      jnp.where(lane == 1, i1 - N_GROUPS,
                                jnp.where(lane == 2, p_grp * p0,
                                          jnp.where(lane == 3, p_grp * p1, 0.0))))
    route_ref[...] = route


def _merge(ya, yb, proj, x, p):
    t = x.shape[0]
    tm = ROW_TM
    row = lambda col: pl.BlockSpec((tm, D), lambda i: (i, col))
    res = lambda shape: pl.BlockSpec(shape, lambda i: (0, 0), pipeline_mode=pl.Buffered(1))
    return pl.pallas_call(
        _merge_kernel,
        grid=(t // tm,),
        in_specs=[row(0), row(0), row(6), row(7), row(0),
                  res((D, D)), res((D, D)), res((D, D)), res((1, D)), res((1, D)),
                  res((D, 128)), res((1, 128))],
        out_specs=[row(0), pl.BlockSpec((tm, 128), lambda i: (i, 0))],
        out_shape=[jax.ShapeDtypeStruct((t, D), F32), jax.ShapeDtypeStruct((t, 128), F32)],
        compiler_params=pltpu.CompilerParams(
            dimension_semantics=("parallel",), vmem_limit_bytes=VMEM_LIMIT),
        name="merge",
    )(ya, yb, proj, proj, x, p["w_a"], p["w_b"], p["w_out"], p["ln1_g"], p["ln1_b"], p["wr"], p["br"])


def _row_gather(src_hbm, dst, sem, idx_ref, base, n):
    def body(j, carry):
        r = idx_ref[base + j]
        pltpu.make_async_copy(src_hbm.at[pl.ds(r, 1), :], dst.at[pl.ds(j, 1), :], sem).start()
        return carry
    lax.fori_loop(0, n, body, 0)


def _expert_kernel(te_ref, first_ref, nv_ref, src_ref, x1_hbm, w1_ref, w3_ref, w2_ref, o_ref,
                   xbuf, sems, w1b, w3b, w2b, *, tm):
    i = pl.program_id(0)
    nv = nv_ref[0]
    slot = lax.rem(i, 2)

    @pl.when(i == 0)
    def _():
        _row_gather(x1_hbm, xbuf.at[0], sems.at[0], src_ref, 0, tm)

    @pl.when(i + 1 < nv)
    def _():
        _row_gather(x1_hbm, xbuf.at[1 - slot], sems.at[1 - slot], src_ref, (i + 1) * tm, tm)

    @pl.when(first_ref[i] == 1)
    def _():
        w1b[...] = w1_ref[0].astype(BF16)
        w3b[...] = w3_ref[0].astype(BF16)
        w2b[...] = w2_ref[0].astype(BF16)

    @pl.when(i < nv)
    def _():
        pltpu.make_async_copy(x1_hbm.at[pl.ds(0, tm), :], xbuf.at[slot], sems.at[slot]).wait()
        xb = xbuf[slot].astype(BF16)
        h1 = jnp.dot(xb, w1b[...], preferred_element_type=F32)
        h3 = jnp.dot(xb, w3b[...], preferred_element_type=F32)
        h = (_silu(h1) * h3).astype(BF16)
        o_ref[...] = jnp.dot(h, w2b[...], preferred_element_type=F32)

    @pl.when(i >= nv)
    def _():
        o_ref[...] = jnp.zeros_like(o_ref)


def _experts(x1, te, first, nv, src, w1, w3, w2, n_tiles, tm):
    wspec = lambda shape: pl.BlockSpec((1,) + shape, lambda i, te, first, nv, src: (te[i], 0, 0))
    return pl.pallas_call(
        functools.partial(_expert_kernel, tm=tm),
        grid_spec=pltpu.PrefetchScalarGridSpec(
            num_scalar_prefetch=4,
            grid=(n_tiles,),
            in_specs=[pl.BlockSpec(memory_space=pl.ANY),
                      wspec((D, D_EXPERT)), wspec((D, D_EXPERT)), wspec((D_EXPERT, D))],
            out_specs=pl.BlockSpec((tm, D), lambda i, te, first, nv, src: (i, 0)),
            scratch_shapes=[pltpu.VMEM((2, tm, D), F32), pltpu.SemaphoreType.DMA((2,)),
                            pltpu.VMEM((D, D_EXPERT), BF16), pltpu.VMEM((D, D_EXPERT), BF16),
                            pltpu.VMEM((D_EXPERT, D), BF16)]),
        out_shape=jax.ShapeDtypeStruct((n_tiles * tm, D), F32),
        compiler_params=pltpu.CompilerParams(
            dimension_semantics=("arbitrary",), vmem_limit_bytes=VMEM_LIMIT, disable_bounds_checks=True),
        name="experts",
    )(te, first, nv, src, x1, w1, w3, w2)


def _combine_kernel(dest_ref, es_hbm, x1_ref, route_ref, g2_ref, b2_ref, o_ref, gbuf, sems, *, tm):
    i = pl.program_id(0)
    n = pl.num_programs(0)
    slot = lax.rem(i, 2)

    def start(step, s):
        for k in range(2):
            _row_gather(es_hbm, gbuf.at[s, k], sems.at[s], dest_ref, (k * n + step) * tm, tm)

    @pl.when(i == 0)
    def _():
        start(0, 0)

    @pl.when(i + 1 < n)
    def _():
        start(i + 1, 1 - slot)

    for k in range(2):
        pltpu.make_async_copy(es_hbm.at[pl.ds(0, tm), :], gbuf.at[slot, k], sems.at[slot]).wait()
    route = route_ref[...]
    w0 = route[:, 2:3]
    w1 = route[:, 3:4]
    moe = w0 * gbuf[slot, 0] + w1 * gbuf[slot, 1]
    o_ref[...] = _layer_norm(ALPHA * x1_ref[...] + moe, g2_ref[...], b2_ref[...])


def _combine(es, dest_km, x1, route, p):
    t = x1.shape[0]
    tm = ROW_TM
    return pl.pallas_call(
        functools.partial(_combine_kernel, tm=tm),
        grid_spec=pltpu.PrefetchScalarGridSpec(
            num_scalar_prefetch=1,
            grid=(t // tm,),
            in_specs=[pl.BlockSpec(memory_space=pl.ANY),
                      pl.BlockSpec((tm, D), lambda i, d: (i, 0)),
                      pl.BlockSpec((tm, 128), lambda i, d: (i, 0)),
                      pl.BlockSpec((1, D), lambda i, d: (0, 0)),
                      pl.BlockSpec((1, D), lambda i, d: (0, 0))],
            out_specs=pl.BlockSpec((tm, D), lambda i, d: (i, 0)),
            scratch_shapes=[pltpu.VMEM((2, 2, tm, D), F32), pltpu.SemaphoreType.DMA((2,))]),
        out_shape=jax.ShapeDtypeStruct((t, D), F32),
        compiler_params=pltpu.CompilerParams(
            dimension_semantics=("arbitrary",), vmem_limit_bytes=VMEM_LIMIT, disable_bounds_checks=True),
        name="combine",
    )(dest_km, es, x1, route, p["ln2_g"], p["ln2_b"])


def _route_meta(e_flat, tm, n_tiles):
    a = e_flat.shape[0]
    ids = jnp.arange(N_EXPERTS, dtype=jnp.int32)
    onehot = (e_flat[:, None] == ids[None, :]).astype(jnp.int32)
    csum = jnp.cumsum(onehot, axis=0)
    rank = jnp.sum((csum - onehot) * onehot, axis=1)
    counts = csum[-1]
    tiles_per = (counts + tm - 1) // tm
    tile_end = jnp.cumsum(tiles_per)
    tile_start = tile_end - tiles_per
    nv = tile_end[-1]
    dest = tile_start[e_flat] * tm + rank
    tile_ids = jnp.arange(n_tiles, dtype=jnp.int32)
    te = jnp.sum((tile_end[None, :] <= tile_ids[:, None]).astype(jnp.int32), axis=1)
    te_last = jnp.sum((tile_end <= nv - 1).astype(jnp.int32))
    valid = tile_ids < nv
    te = jnp.where(valid, te, te_last).astype(jnp.int32)
    first = (valid & (tile_ids == tile_start[te])).astype(jnp.int32)
    src = jnp.zeros((n_tiles * tm,), jnp.int32).at[dest].set(jnp.arange(a, dtype=jnp.int32) // 2)
    return te, first, nv.reshape(1).astype(jnp.int32), src, dest.astype(jnp.int32)


def kernel(x_prompt, x_sample, state_ssm, state_conv, state_hgrn, w_in, conv_w, conv_b, dt_bias, a_log, d_skip, ssm_norm_g, hgrn_lb_logits, hgrn_norm_g, w_a, w_b, w_out, ln1_g, ln1_b, router_g_w, router_g_b, router_e_w, router_e_b, exp_w1, exp_w3, exp_w2, ln2_g, ln2_b):
    nbp, seqp, _ = x_prompt.shape
    nbs, seqs, _ = x_sample.shape
    tp, ts = nbp * seqp, nbs * seqs
    t = tp + ts
    l = 0
    x_all = jnp.concatenate([x_prompt.reshape(tp, D), x_sample.reshape(ts, D)], axis=0)

    wl = w_in[l]
    o_dt = D + D + 2 * BCW
    o_q = o_dt + SSM_HEADS
    w_main = jnp.concatenate([wl[:, :2 * D], wl[:, o_q:], wl[:, 2 * D:o_dt]], axis=1).astype(BF16)
    w_dt = jnp.pad(wl[:, o_dt:o_q], ((0, 0), (0, 128 - SSM_HEADS))).astype(BF16)
    xb = x_all.astype(BF16)
    proj = _matmul(xb, w_main, 1024, 1024, F32)
    dt_raw = _matmul(xb, w_dt, 1024, 128, F32)

    lb_all = jnp.cumsum(jax.nn.softmax(hgrn_lb_logits.astype(F32), axis=0), axis=0)
    head_of = jnp.arange(D, dtype=jnp.int32) // SSM_P
    pad128 = lambda v: jnp.pad(v, (0, 128 - v.shape[0])).reshape(1, 128)
    p = {
        "cwx": conv_w[l][:, :D], "cwbc": conv_w[l][:, D:],
        "cbx": conv_b[l][:D].reshape(1, D), "cbbc": conv_b[l][D:].reshape(1, 2 * BCW),
        "dtb": pad128(dt_bias[l]), "alog": pad128(a_log[l]),
        "dskip_e": d_skip[l][head_of].reshape(1, D),
        "ssm_norm_g": ssm_norm_g[l].reshape(1, D),
        "e64": (jnp.arange(128, dtype=jnp.int32)[:, None] == head_of[None, :]).astype(F32),
        "e128": (jnp.arange(128, dtype=jnp.int32)[:, None]
                 == (jnp.arange(2 * D, dtype=jnp.int32) // 128)[None, :]).astype(F32),
        "lb": lb_all[l].reshape(1, D), "hgrn_norm_g": hgrn_norm_g[l].reshape(1, D),
        "w_a": w_a[l].astype(BF16), "w_b": w_b[l].astype(BF16), "w_out": w_out[l].astype(BF16),
        "ln1_g": ln1_g[l].reshape(1, D), "ln1_b": ln1_b[l].reshape(1, D),
        "ln2_g": ln2_g[l].reshape(1, D), "ln2_b": ln2_b[l].reshape(1, D),
        "wr": jnp.pad(jnp.concatenate([router_g_w[l], router_e_w[l]], axis=1),
                      ((0, 0), (0, 128 - N_GROUPS - N_EXPERTS))),
        "br": pad128(jnp.concatenate([router_g_b[l], router_e_b[l]])),
    }

    cs_p = SSD_CHUNK if seqp % SSD_CHUNK == 0 else seqp
    cg_p = GLA_CHUNK if seqp % GLA_CHUNK == 0 else seqp
    cs_s = SSD_CHUNK if seqs % SSD_CHUNK == 0 else seqs
    cg_s = GLA_CHUNK if seqs % GLA_CHUNK == 0 else seqs

    ya_p, ssm_p, conv_p = _ssd(proj, dt_raw, 0, jnp.zeros((nbp, D, SSM_N), F32),
                               jnp.zeros((nbp, CONV_K - 1, D + 2 * BCW), F32), p, nbp, seqp, cs_p)
    ya_s, ssm_s, conv_s = _ssd(proj, dt_raw, tp, state_ssm[l].reshape(nbs, D, SSM_N), state_conv[l],
                               p, nbs, seqs, cs_s)
    yb_p, hg_p = _gla(proj, 0, jnp.zeros((nbp, D, HG_DK), F32), p, nbp, seqp, cg_p)
    yb_s, hg_s = _gla(proj, tp, state_hgrn[l].reshape(nbs, D, HG_DK), p, nbs, seqs, cg_s)

    ya = jnp.concatenate([ya_p, ya_s], axis=0)
    yb = jnp.concatenate([yb_p, yb_s], axis=0)
    x1, route = _merge(ya, yb, proj, x_all, p)

    e_flat = route[:, :2].astype(jnp.int32).reshape(-1)
    n_tiles = (2 * t) // MOE_TM + N_EXPERTS
    te, first, nv, src, dest = _route_meta(e_flat, MOE_TM, n_tiles)
    es = _experts(x1, te, first, nv, src, exp_w1[l], exp_w3[l], exp_w2[l], n_tiles, MOE_TM)
    dest_km = dest.reshape(t // ROW_TM, ROW_TM, 2).transpose(2, 0, 1).reshape(-1)
    x2 = _combine(es, dest_km, x1, route, p)

    y_prompt = x2[:tp].reshape(nbp, seqp, D)
    y_sample = x2[tp:].reshape(nbs, seqs, D)
    return (y_prompt, y_sample,
            ssm_p.reshape(1, nbp, SSM_HEADS, SSM_P, SSM_N), conv_p[None],
            hg_p.reshape(1, nbp, HG_HEADS, HG_DK, HG_DK),
            ssm_s.reshape(1, nbs, SSM_HEADS, SSM_P, SSM_N), conv_s[None],
            hg_s.reshape(1, nbs, HG_HEADS, HG_DK, HG_DK))
```

```python
import functools

import jax
import jax.numpy as jnp
from jax import lax
from jax.experimental import pallas as pl
from jax.experimental.pallas import tpu as pltpu

F32 = jnp.float32
BF16 = jnp.bfloat16
HI = lax.Precision.HIGHEST

D = 2048
SSM_HEADS = 32
SSM_P = 64
SSM_N = 128
SSM_G = 4
GW = D // SSM_G
BCW = SSM_G * SSM_N
CONV_K = 4
HG_HEADS = 16
HG_DK = 128
N_GROUPS = 4
EPG = 8
N_EXPERTS = 32
D_EXPERT = 512
EPS = 1e-5
ALPHA = 2.0 ** 0.25
NEG = -1e30
LANES = 128

VMEM_LIMIT = 56 * 1024 * 1024
SSD_CHUNK = 128
GLA_CHUNK = 64
STEP_ROWS = 64
MOE_TM = 256
ROW_TM = 256
MERGE_TM = 512
MERGE_TN = 512
CARRY = 8


def _sigmoid(x):
    return 0.5 * jnp.tanh(0.5 * x) + 0.5


def _silu(x):
    return x * _sigmoid(x)


def _softplus(x):
    return jnp.maximum(x, 0.0) + jnp.log(1.0 + jnp.exp(-jnp.abs(x)))


def _split3(x):
    hi = x.astype(BF16)
    r = x - hi.astype(F32)
    mid = r.astype(BF16)
    lo = (r - mid.astype(F32)).astype(BF16)
    return hi, mid, lo


def _dot3_rhs(m_bf16, x):
    return sum(jnp.dot(m_bf16, part, preferred_element_type=F32) for part in _split3(x))


def _dot3_lhs(x, m_bf16):
    return sum(jnp.dot(part, m_bf16, preferred_element_type=F32) for part in _split3(x))


def _seq_masks(r, seq_len):
    shift = seq_len.bit_length() - 1
    row = lax.broadcasted_iota(jnp.int32, (r, r), 0)
    col = lax.broadcasted_iota(jnp.int32, (r, r), 1)
    same = lax.shift_right_logical(row, shift) == lax.shift_right_logical(col, shift)
    return same & (row >= col), same & (row <= col)


def _seq_totals(cum, nseq, seq_len):
    w = cum.shape[1]
    parts = [jnp.broadcast_to(cum[(b + 1) * seq_len - 1:(b + 1) * seq_len, :], (seq_len, w)) for b in range(nseq)]
    return parts[0] if nseq == 1 else jnp.concatenate(parts, axis=0)


def _pad_rows(x, rows):
    if x.shape[0] == rows:
        return x
    return jnp.concatenate([x, jnp.zeros((rows - x.shape[0], x.shape[1]), x.dtype)], axis=0)


def _row_mask(x, b, seq_len):
    row = lax.broadcasted_iota(jnp.int32, x.shape, 0)
    return jnp.where((row >= b * seq_len) & (row < (b + 1) * seq_len), x, jnp.zeros_like(x))


def _mm_kernel(x_ref, w_ref, o_ref):
    o_ref[...] = jnp.dot(x_ref[...], w_ref[...], preferred_element_type=F32).astype(o_ref.dtype)


def _matmul(x, w, tm, tn, out_dtype):
    m, k = x.shape
    n = w.shape[1]
    return pl.pallas_call(
        _mm_kernel,
        grid=(m // tm, n // tn),
        in_specs=[pl.BlockSpec((tm, k), lambda i, j: (i, 0)),
                  pl.BlockSpec((k, tn), lambda i, j: (0, j))],
        out_specs=pl.BlockSpec((tm, tn), lambda i, j: (i, j)),
        out_shape=jax.ShapeDtypeStruct((m, n), out_dtype),
        compiler_params=pltpu.CompilerParams(
            dimension_semantics=("parallel", "arbitrary"), vmem_limit_bytes=VMEM_LIMIT),
        name="proj",
    )(x, w)


def _conv_silu(buf_ref, carry0_ref, u, w_ref, b_ref, nseq, seq_len, first):
    ch = u.shape[1]

    @pl.when(first)
    def _():
        buf_ref[:, 0:CARRY, :] = carry0_ref[...]
    u3 = u.reshape(nseq, seq_len, ch)
    buf_ref[:, CARRY:CARRY + seq_len, :] = u3
    acc = b_ref[...] + w_ref[CONV_K - 1:CONV_K, :] * u3
    for k in range(CONV_K - 1):
        off = CARRY - (CONV_K - 1) + k
        acc = acc + w_ref[k:k + 1, :] * buf_ref[:, off:off + seq_len, :]
    tail = buf_ref[:, seq_len:seq_len + CARRY, :]
    buf_ref[:, 0:CARRY, :] = tail
    return _silu(acc).reshape(nseq * seq_len, ch), tail


def _pair_cols(tile):
    lane = lax.broadcasted_iota(jnp.int32, tile.shape, 1)
    swapped = pltpu.roll(tile, SSM_P, axis=1)
    return jnp.where(lane < SSM_P, tile, swapped), jnp.where(lane < SSM_P, swapped, tile)


def _ssd_kernel(xp_ref, bcp_ref, z_ref, dtr_ref, dtrt_ref, h0_ref, cx0_ref, cbc0_ref,
                cwx_ref, cwbc_ref, cbx_ref, cbbc_ref, dtb_ref, dtbt_ref, alogt_ref, aloge_ref,
                dskip_ref, normg_ref, e64_ref,
                y_ref, hout_ref, ctx_ref, ctbc_ref,
                xbuf, bcbuf, h_ref, *, nseq, seq_len):
    r = nseq * seq_len
    rp = max(r, LANES)
    zi = pl.program_id(1)
    first = zi == 0

    @pl.when(first)
    def _():
        h_ref[...] = h0_ref[...]

    xs, tail_x = _conv_silu(xbuf, cx0_ref, xp_ref[...], cwx_ref, cbx_ref, nseq, seq_len, first)
    bc, tail_bc = _conv_silu(bcbuf, cbc0_ref, bcp_ref[...], cwbc_ref, cbbc_ref, nseq, seq_len, first)
    ctx_ref[...] = tail_x
    ctbc_ref[...] = tail_bc

    causal, anti = _seq_masks(r, seq_len)
    dt = _softplus(dtr_ref[...] + dtb_ref[...])
    dt_e = _dot3_lhs(dt, e64_ref[...])
    loga_e = dt_e * (-jnp.exp(aloge_ref[...]))
    cum_e = _dot3_rhs(causal.astype(BF16), loga_e)
    loga_t = _softplus(dtrt_ref[0] + dtbt_ref[...]) * (-jnp.exp(alogt_ref[...]))
    cum_t = _dot3_lhs(loga_t, anti.astype(BF16))
    tot_e = _seq_totals(cum_e, nseq, seq_len)
    xdt = xs * dt_e
    in_scale = jnp.exp(cum_e)
    xw = xdt * jnp.exp(tot_e - cum_e)
    xw_t = _pad_rows(xw, rp).T.astype(BF16)
    lane = lax.broadcasted_iota(jnp.int32, (r, LANES), 1)

    y_groups = []
    for g in range(SSM_G):
        bg = bc[:, g * SSM_N:(g + 1) * SSM_N]
        cg = bc[:, BCW + g * SSM_N:BCW + (g + 1) * SSM_N]
        cb = lax.dot_general(cg, bg, (((1,), (1,)), ((), ())), preferred_element_type=F32)
        rows = slice(g * GW, (g + 1) * GW)
        pieces = []
        for j in range(GW // LANES):
            col0 = g * GW + j * LANES
            xpair = xdt[:, col0:col0 + LANES]
            ms, rhs = [], []
            for half, colb in enumerate(_pair_cols(cum_e[:, col0:col0 + LANES])):
                head = col0 // SSM_P + half
                seg = colb[:, :r] - cum_t[head:head + 1, :]
                ms.append((cb * jnp.exp(jnp.where(causal, seg, NEG))).astype(BF16))
                rhs.append(jnp.where(lane >= SSM_P if half else lane < SSM_P, xpair, 0.0).astype(BF16))
            if r % LANES == 0:
                pieces.append(jnp.dot(jnp.concatenate(ms, axis=1), jnp.concatenate(rhs, axis=0),
                                      preferred_element_type=F32))
            else:
                pieces.append(jnp.dot(ms[0], rhs[0], preferred_element_type=F32)
                              + jnp.dot(ms[1], rhs[1], preferred_element_type=F32))
        y_intra = jnp.concatenate(pieces, axis=1)
        y_inter = []
        for b in range(nseq):
            rb = slice(b * seq_len, (b + 1) * seq_len)
            hg = h_ref[b, rows, :]
            y_inter.append(lax.dot_general(cg[rb, :], hg, (((1,), (1,)), ((), ())),
                                           preferred_element_type=F32))
            bmask = bg if nseq == 1 else _row_mask(bg, b, seq_len)
            st = jnp.dot(xw_t[rows, :], _pad_rows(bmask, rp).astype(BF16), preferred_element_type=F32)
            dec8 = jnp.exp(tot_e[b * seq_len:b * seq_len + 8, rows])
            for j in range(GW // LANES):
                for half, dcol in enumerate(_pair_cols(dec8[:, j * LANES:(j + 1) * LANES])):
                    h8 = 2 * j + half
                    hr = slice(g * GW + h8 * SSM_P, g * GW + (h8 + 1) * SSM_P)
                    h_ref[b, hr, :] = dcol[0:1, :] * h_ref[b, hr, :] + st[h8 * SSM_P:(h8 + 1) * SSM_P, :]
        y_inter = y_inter[0] if nseq == 1 else jnp.concatenate(y_inter, axis=0)
        y_groups.append(y_intra + y_inter * in_scale[:, rows])
    y = jnp.concatenate(y_groups, axis=1) + dskip_ref[...] * xs
    yz = y * _silu(z_ref[...])
    outs = []
    for g in range(SSM_G):
        blk = yz[:, g * GW:(g + 1) * GW]
        ms = jnp.sum(blk * blk, axis=1, keepdims=True) * (1.0 / GW)
        outs.append(blk * lax.rsqrt(ms + EPS))
    y_ref[...] = (jnp.concatenate(outs, axis=1) * normg_ref[...]).astype(y_ref.dtype)

    @pl.when(zi == pl.num_programs(1) - 1)
    def _():
        hout_ref[...] = h_ref[...]


def _ssd(proj, dt_raw, row0, h0, conv0, p, nb, seq):
    seq_len = SSD_CHUNK if seq % SSD_CHUNK == 0 else seq
    nseq = 1 if seq_len == SSD_CHUNK else max(1, min(nb, STEP_ROWS // seq_len))
    nz = seq // seq_len
    r = nseq * seq_len
    t = nb * seq
    r0 = row0 // r
    nblk = t // r
    dtrt = dt_raw[row0:row0 + t, :SSM_HEADS].reshape(nblk, r, SSM_HEADS).transpose(0, 2, 1)
    pad_rows = CARRY - (CONV_K - 1)
    cx0 = jnp.pad(conv0[:, :, :D], ((0, 0), (pad_rows, 0), (0, 0)))
    cbc0 = jnp.pad(conv0[:, :, D:], ((0, 0), (pad_rows, 0), (0, 0)))
    rowblk = lambda col: pl.BlockSpec((r, D), lambda b, z: (r0 + b * nz + z, col))
    const = lambda shape: pl.BlockSpec(shape, lambda b, z: tuple(0 for _ in shape))
    per_b = lambda shape: pl.BlockSpec((nseq,) + shape, lambda b, z: (b, 0, 0))
    y, hout, ctx, ctbc = pl.pallas_call(
        functools.partial(_ssd_kernel, nseq=nseq, seq_len=seq_len),
        grid=(nb // nseq, nz),
        in_specs=[rowblk(1),
                  pl.BlockSpec((r, 2 * BCW), lambda b, z: (r0 + b * nz + z, 16)),
                  rowblk(0),
                  pl.BlockSpec((r, LANES), lambda b, z: (r0 + b * nz + z, 0)),
                  pl.BlockSpec((1, SSM_HEADS, r), lambda b, z: (b * nz + z, 0, 0)),
                  per_b((D, SSM_N)), per_b((CARRY, D)), per_b((CARRY, 2 * BCW)),
                  const((CONV_K, D)), const((CONV_K, 2 * BCW)), const((1, D)), const((1, 2 * BCW)),
                  const((1, LANES)), const((SSM_HEADS, r)), const((SSM_HEADS, r)), const((1, D)),
                  const((1, D)), const((1, D)), const((LANES, D))],
        out_specs=[pl.BlockSpec((r, D), lambda b, z: (b * nz + z, 0)),
                   per_b((D, SSM_N)), per_b((CARRY, D)), per_b((CARRY, 2 * BCW))],
        out_shape=[jax.ShapeDtypeStruct((t, D), BF16),
                   jax.ShapeDtypeStruct((nb, D, SSM_N), F32),
                   jax.ShapeDtypeStruct((nb, CARRY, D), F32),
                   jax.ShapeDtypeStruct((nb, CARRY, 2 * BCW), F32)],
        scratch_shapes=[pltpu.VMEM((nseq, seq_len + CARRY, D), F32),
                        pltpu.VMEM((nseq, seq_len + CARRY, 2 * BCW), F32),
                        pltpu.VMEM((nseq, D, SSM_N), F32)],
        compiler_params=pltpu.CompilerParams(
            dimension_semantics=("parallel", "arbitrary"), vmem_limit_bytes=VMEM_LIMIT),
        name="ssd",
    )(proj, proj, proj, dt_raw, dtrt, h0, cx0, cbc0,
      p["cwx"], p["cwbc"], p["cbx"], p["cbbc"], p["dtb"],
      jnp.broadcast_to(p["dtb"][0, :SSM_HEADS, None], (SSM_HEADS, r)),
      jnp.broadcast_to(p["alog"][:, None], (SSM_HEADS, r)),
      p["alog_e"], p["dskip_e"], p["ssm_norm_g"], p["e64"])
    conv_new = jnp.concatenate([ctx[:, pad_rows:], ctbc[:, pad_rows:]], axis=-1)
    return y, hout, conv_new


def _gla_kernel(q_ref, f_ref, i_ref, g_ref, s0_ref, lb_ref, normg_ref, y_ref, sout_ref, s_ref,
                *, nseq, seq_len):
    r = nseq * seq_len
    rp = max(r, LANES)
    zi = pl.program_id(1)

    @pl.when(zi == 0)
    def _():
        s_ref[...] = s0_ref[...]

    lb = lb_ref[...]
    f = lb + (1.0 - lb) * _sigmoid(f_ref[...])
    k = 1.0 - f
    q = _silu(q_ref[...])
    v = i_ref[...]
    causal, _ = _seq_masks(r, seq_len)
    bc = _dot3_rhs(causal.astype(BF16), jnp.log(f))
    tot = _seq_totals(bc, nseq, seq_len)
    qe = q * jnp.exp(bc)
    ke = k * jnp.exp(-bc)
    kd_t = _pad_rows(ke * jnp.exp(tot), rp).T.astype(BF16)
    vb = v.astype(BF16)
    gate = _silu(g_ref[...])
    outs = []
    for h in range(HG_HEADS):
        sl = slice(h * HG_DK, (h + 1) * HG_DK)
        qh = qe[:, sl]
        attn = lax.dot_general(qh, ke[:, sl], (((1,), (1,)), ((), ())), preferred_element_type=F32)
        attn = jnp.where(causal, attn, 0.0)
        o = jnp.dot(attn.astype(BF16), vb[:, sl], preferred_element_type=F32)
        o_inter = []
        for b in range(nseq):
            rb = slice(b * seq_len, (b + 1) * seq_len)
            sh = s_ref[b, sl, :]
            o_inter.append(jnp.dot(qh[rb, :], sh, preferred_element_type=F32))
            vmask = vb[:, sl] if nseq == 1 else _row_mask(vb[:, sl], b, seq_len)
            st = jnp.dot(kd_t[sl, :], _pad_rows(vmask, rp), preferred_element_type=F32)
            dec = jnp.exp(tot[b * seq_len:b * seq_len + 1, sl])
            dec_col = jnp.broadcast_to(dec, (HG_DK, HG_DK)).T
            s_ref[b, sl, :] = dec_col * sh + st
        o = o + (o_inter[0] if nseq == 1 else jnp.concatenate(o_inter, axis=0))
        ms = jnp.sum(o * o, axis=1, keepdims=True) * (1.0 / HG_DK)
        outs.append(o * lax.rsqrt(ms + EPS))
    y_ref[...] = (jnp.concatenate(outs, axis=1) * normg_ref[...] * gate).astype(y_ref.dtype)

    @pl.when(zi == pl.num_programs(1) - 1)
    def _():
        sout_ref[...] = s_ref[...]


def _gla(proj, row0, s0, p, nb, seq):
    seq_len = GLA_CHUNK if seq % GLA_CHUNK == 0 else seq
    nseq = 1 if seq_len == GLA_CHUNK else max(1, min(nb, STEP_ROWS // seq_len))
    nz = seq // seq_len
    r = nseq * seq_len
    r0 = row0 // r
    rowblk = lambda col: pl.BlockSpec((r, D), lambda b, z: (r0 + b * nz + z, col))
    const = lambda shape: pl.BlockSpec(shape, lambda b, z: tuple(0 for _ in shape))
    per_b = lambda shape: pl.BlockSpec((nseq,) + shape, lambda b, z: (b, 0, 0))
    return pl.pallas_call(
        functools.partial(_gla_kernel, nseq=nseq, seq_len=seq_len),
        grid=(nb // nseq, nz),
        in_specs=[rowblk(2), rowblk(3), rowblk(4), rowblk(5), per_b((D, HG_DK)), const((1, D)), const((1, D))],
        out_specs=[pl.BlockSpec((r, D), lambda b, z: (b * nz + z, 0)), per_b((D, HG_DK))],
        out_shape=[jax.ShapeDtypeStruct((nb * seq, D), BF16), jax.ShapeDtypeStruct((nb, D, HG_DK), F32)],
        scratch_shapes=[pltpu.VMEM((nseq, D, HG_DK), F32)],
        compiler_params=pltpu.CompilerParams(
            dimension_semantics=("parallel", "arbitrary"), vmem_limit_bytes=VMEM_LIMIT),
        name="gla",
    )(proj, proj, proj, proj, s0, p["lb"], p["hgrn_norm_g"])


def _layer_norm(x, g, b):
    mu = jnp.mean(x, axis=1, keepdims=True)
    xc = x - mu
    var = jnp.mean(xc * xc, axis=1, keepdims=True)
    return xc * lax.rsqrt(var + EPS) * g + b


def _gate_kernel(yap_ref, yas_ref, ybp_ref, ybs_ref, ga_ref, gb_ref, wa_ref, wb_ref, o_ref, *, n_prompt):
    is_prompt = pl.program_id(0) < n_prompt
    ya = jnp.where(is_prompt, yap_ref[...], yas_ref[...])
    yb = jnp.where(is_prompt, ybp_ref[...], ybs_ref[...])
    a = jnp.dot(ya, wa_ref[...], preferred_element_type=F32)
    b = jnp.dot(yb, wb_ref[...], preferred_element_type=F32)
    o_ref[...] = (_sigmoid(ga_ref[...]) * a + _sigmoid(gb_ref[...]) * b).astype(o_ref.dtype)


def _gate(ya_p, ya_s, yb_p, yb_s, proj, p):
    tm, tn = MERGE_TM, MERGE_TN
    n_p = ya_p.shape[0] // tm
    t = ya_p.shape[0] + ya_s.shape[0]
    prow = pl.BlockSpec((tm, D), lambda i, j: (jnp.minimum(i, n_p - 1), 0))
    srow = pl.BlockSpec((tm, D), lambda i, j: (jnp.maximum(i - n_p, 0), 0))
    gcol = lambda seg: pl.BlockSpec((tm, tn), lambda i, j: (i, seg * (D // tn) + j))
    wcol = pl.BlockSpec((D, tn), lambda i, j: (0, j))
    return pl.pallas_call(
        functools.partial(_gate_kernel, n_prompt=n_p),
        grid=(t // tm, D // tn),
        in_specs=[prow, srow, prow, srow, gcol(6), gcol(7), wcol, wcol],
        out_specs=pl.BlockSpec((tm, tn), lambda i, j: (i, j)),
        out_shape=jax.ShapeDtypeStruct((t, D), BF16),
        compiler_params=pltpu.CompilerParams(
            dimension_semantics=("parallel", "arbitrary"), vmem_limit_bytes=VMEM_LIMIT),
        name="gate",
    )(ya_p, ya_s, yb_p, yb_s, proj, proj, p["w_a"], p["w_b"])


def _merge_kernel(m_ref, xp_ref, xs_ref, wo_ref, g1_ref, b1_ref, wrh_ref, wrl_ref, br_ref,
                  x1_ref, route_ref, *, n_prompt):
    x = jnp.where(pl.program_id(0) < n_prompt, xp_ref[...], xs_ref[...])
    mix = jnp.dot(m_ref[...], wo_ref[...], preferred_element_type=F32)
    x1 = _layer_norm(ALPHA * x + mix, g1_ref[...], b1_ref[...])
    x1_ref[...] = x1
    hi = x1.astype(BF16)
    lo = (x1 - hi.astype(F32)).astype(BF16)
    logits = (jnp.dot(hi, wrh_ref[...], preferred_element_type=F32)
              + jnp.dot(lo, wrh_ref[...], preferred_element_type=F32)
              + jnp.dot(hi, wrl_ref[...], preferred_element_type=F32) + br_ref[...])
    lane = lax.broadcasted_iota(jnp.int32, logits.shape, 1).astype(F32)
    gl = jnp.where(lane < N_GROUPS, logits, NEG)
    gmax = jnp.max(gl, axis=1, keepdims=True)
    gsel = jnp.min(jnp.where(gl == gmax, lane, 1e9), axis=1, keepdims=True)
    p_grp = 1.0 / jnp.sum(jnp.exp(gl - gmax), axis=1, keepdims=True)
    lo = N_GROUPS + gsel * EPG
    el = jnp.where((lane >= lo) & (lane < lo + EPG), logits, NEG)
    v0 = jnp.max(el, axis=1, keepdims=True)
    i0 = jnp.min(jnp.where(el == v0, lane, 1e9), axis=1, keepdims=True)
    el2 = jnp.where(lane == i0, NEG, el)
    v1 = jnp.max(el2, axis=1, keepdims=True)
    i1 = jnp.min(jnp.where(el2 == v1, lane, 1e9), axis=1, keepdims=True)
    e1 = jnp.exp(v1 - v0)
    p0 = 1.0 / (1.0 + e1)
    p1 = e1 * p0
    route = jnp.where(lane == 0, i0 - N_GROUPS,
                      jnp.where(lane == 1, i1 - N_GROUPS,
                                jnp.where(lane == 2, p_grp * p0,
                                          jnp.where(lane == 3, p_grp * p1, 0.0))))
    route_ref[...] = route


def _merge(merged, x_p, x_s, p):
    tm = MERGE_TM
    n_p = x_p.shape[0] // tm
    t = x_p.shape[0] + x_s.shape[0]
    row = pl.BlockSpec((tm, D), lambda i: (i, 0))
    prow = pl.BlockSpec((tm, D), lambda i: (jnp.minimum(i, n_p - 1), 0))
    srow = pl.BlockSpec((tm, D), lambda i: (jnp.maximum(i - n_p, 0), 0))
    res = lambda shape: pl.BlockSpec(shape, lambda i: (0, 0), pipeline_mode=pl.Buffered(1))
    wr_hi = p["wr"].astype(BF16)
    wr_lo = (p["wr"] - wr_hi.astype(F32)).astype(BF16)
    return pl.pallas_call(
        functools.partial(_merge_kernel, n_prompt=n_p),
        grid=(t // tm,),
        in_specs=[row, prow, srow, res((D, D)), res((1, D)), res((1, D)),
                  res((D, LANES)), res((D, LANES)), res((1, LANES))],
        out_specs=[row, pl.BlockSpec((tm, LANES), lambda i: (i, 0))],
        out_shape=[jax.ShapeDtypeStruct((t, D), F32), jax.ShapeDtypeStruct((t, LANES), F32)],
        compiler_params=pltpu.CompilerParams(
            dimension_semantics=("parallel",), vmem_limit_bytes=VMEM_LIMIT),
        name="merge",
    )(merged, x_p, x_s, p["w_out"], p["ln1_g"], p["ln1_b"], wr_hi, wr_lo, p["br"])


def _row_gather(src_hbm, dst, sem, idx_ref, base, n):
    def body(j, carry):
        r = idx_ref[base + j]
        pltpu.make_async_copy(src_hbm.at[pl.ds(r, 1), :], dst.at[pl.ds(j, 1), :], sem).start()
        return carry
    lax.fori_loop(0, n, body, 0)


def _expert_kernel(te_ref, first_ref, nv_ref, src_ref, x1_hbm, w1_ref, w3_ref, w2_ref, o_ref,
                   xbuf, sems, w1b, w3b, w2b, *, tm):
    i = pl.program_id(0)
    nv = nv_ref[0]
    slot = lax.rem(i, 2)

    @pl.when(i == 0)
    def _():
        _row_gather(x1_hbm, xbuf.at[0], sems.at[0], src_ref, 0, tm)

    @pl.when(i + 1 < nv)
    def _():
        _row_gather(x1_hbm, xbuf.at[1 - slot], sems.at[1 - slot], src_ref, (i + 1) * tm, tm)

    @pl.when(first_ref[i] == 1)
    def _():
        w1b[...] = w1_ref[0].astype(BF16)
        w3b[...] = w3_ref[0].astype(BF16)
        w2b[...] = w2_ref[0].astype(BF16)

    @pl.when(i < nv)
    def _():
        pltpu.make_async_copy(x1_hbm.at[pl.ds(0, tm), :], xbuf.at[slot], sems.at[slot]).wait()
        xb = xbuf[slot].astype(BF16)
        h1 = jnp.dot(xb, w1b[...], preferred_element_type=F32)
        h3 = jnp.dot(xb, w3b[...], preferred_element_type=F32)
        h = (_silu(h1) * h3).astype(BF16)
        o_ref[...] = jnp.dot(h, w2b[...], preferred_element_type=F32)

    @pl.when(i >= nv)
    def _():
        o_ref[...] = jnp.zeros_like(o_ref)


def _experts(x1, te, first, nv, src, w1, w3, w2, n_tiles, tm):
    wspec = lambda shape: pl.BlockSpec((1,) + shape, lambda i, te, first, nv, src: (te[i], 0, 0))
    return pl.pallas_call(
        functools.partial(_expert_kernel, tm=tm),
        grid_spec=pltpu.PrefetchScalarGridSpec(
            num_scalar_prefetch=4,
            grid=(n_tiles,),
            in_specs=[pl.BlockSpec(memory_space=pl.ANY),
                      wspec((D, D_EXPERT)), wspec((D, D_EXPERT)), wspec((D_EXPERT, D))],
            out_specs=pl.BlockSpec((tm, D), lambda i, te, first, nv, src: (i, 0)),
            scratch_shapes=[pltpu.VMEM((2, tm, D), F32), pltpu.SemaphoreType.DMA((2,)),
                            pltpu.VMEM((D, D_EXPERT), BF16), pltpu.VMEM((D, D_EXPERT), BF16),
                            pltpu.VMEM((D_EXPERT, D), BF16)]),
        out_shape=jax.ShapeDtypeStruct((n_tiles * tm, D), F32),
        compiler_params=pltpu.CompilerParams(
            dimension_semantics=("arbitrary",), vmem_limit_bytes=VMEM_LIMIT, disable_bounds_checks=True),
        name="experts",
    )(te, first, nv, src, x1, w1, w3, w2)


def _combine_kernel(dest_ref, es_hbm, x1_ref, route_ref, g2_ref, b2_ref, op_ref, os_ref, gbuf, sems,
                    *, tm, n_prompt):
    i = pl.program_id(0)
    n = pl.num_programs(0)
    slot = lax.rem(i, 2)

    def start(step, s):
        for k in range(2):
            _row_gather(es_hbm, gbuf.at[s, k], sems.at[s], dest_ref, (k * n + step) * tm, tm)

    @pl.when(i == 0)
    def _():
        start(0, 0)

    @pl.when(i + 1 < n)
    def _():
        start(i + 1, 1 - slot)

    for k in range(2):
        pltpu.make_async_copy(es_hbm.at[pl.ds(0, tm), :], gbuf.at[slot, k], sems.at[slot]).wait()
    route = route_ref[...]
    w0 = route[:, 2:3]
    w1 = route[:, 3:4]
    moe = w0 * gbuf[slot, 0] + w1 * gbuf[slot, 1]
    out = _layer_norm(ALPHA * x1_ref[...] + moe, g2_ref[...], b2_ref[...])

    @pl.when(i < n_prompt)
    def _():
        op_ref[...] = out

    @pl.when(i >= n_prompt)
    def _():
        os_ref[...] = out


def _combine(es, dest_km, x1, route, p, tp):
    t = x1.shape[0]
    tm = ROW_TM
    n_p = tp // tm
    return pl.pallas_call(
        functools.partial(_combine_kernel, tm=tm, n_prompt=n_p),
        grid_spec=pltpu.PrefetchScalarGridSpec(
            num_scalar_prefetch=1,
            grid=(t // tm,),
            in_specs=[pl.BlockSpec(memory_space=pl.ANY),
                      pl.BlockSpec((tm, D), lambda i, d: (i, 0)),
                      pl.BlockSpec((tm, LANES), lambda i, d: (i, 0)),
                      pl.BlockSpec((1, D), lambda i, d: (0, 0)),
                      pl.BlockSpec((1, D), lambda i, d: (0, 0))],
            out_specs=[pl.BlockSpec((tm, D), lambda i, d: (jnp.minimum(i, n_p - 1), 0)),
                       pl.BlockSpec((tm, D), lambda i, d: (jnp.maximum(i - n_p, 0), 0))],
            scratch_shapes=[pltpu.VMEM((2, 2, tm, D), F32), pltpu.SemaphoreType.DMA((2,))]),
        out_shape=[jax.ShapeDtypeStruct((tp, D), F32), jax.ShapeDtypeStruct((t - tp, D), F32)],
        compiler_params=pltpu.CompilerParams(
            dimension_semantics=("arbitrary",), vmem_limit_bytes=VMEM_LIMIT, disable_bounds_checks=True),
        name="combine",
    )(dest_km, es, x1, route, p["ln2_g"], p["ln2_b"])


def _route_meta(e_flat, tm, n_tiles):
    a = e_flat.shape[0]
    ids = jnp.arange(N_EXPERTS, dtype=jnp.int32)
    onehot = (e_flat[:, None] == ids[None, :]).astype(jnp.int32)
    csum = jnp.cumsum(onehot, axis=0)
    rank = jnp.sum((csum - onehot) * onehot, axis=1)
    counts = csum[-1]
    tiles_per = (counts + tm - 1) // tm
    tile_end = jnp.cumsum(tiles_per)
    tile_start = tile_end - tiles_per
    nv = tile_end[-1]
    dest = tile_start[e_flat] * tm + rank
    tile_ids = jnp.arange(n_tiles, dtype=jnp.int32)
    te = jnp.sum((tile_end[None, :] <= tile_ids[:, None]).astype(jnp.int32), axis=1)
    te_last = jnp.sum((tile_end <= nv - 1).astype(jnp.int32))
    valid = tile_ids < nv
    te = jnp.where(valid, te, te_last).astype(jnp.int32)
    first = (valid & (tile_ids == tile_start[te])).astype(jnp.int32)
    src = jnp.zeros((n_tiles * tm,), jnp.int32).at[dest].set(jnp.arange(a, dtype=jnp.int32) // 2)
    return te, first, nv.reshape(1).astype(jnp.int32), src, dest.astype(jnp.int32)


def kernel(x_prompt, x_sample, state_ssm, state_conv, state_hgrn, w_in, conv_w, conv_b, dt_bias, a_log, d_skip, ssm_norm_g, hgrn_lb_logits, hgrn_norm_g, w_a, w_b, w_out, ln1_g, ln1_b, router_g_w, router_g_b, router_e_w, router_e_b, exp_w1, exp_w3, exp_w2, ln2_g, ln2_b):
    nbp, seqp, _ = x_prompt.shape
    nbs, seqs, _ = x_sample.shape
    tp, ts = nbp * seqp, nbs * seqs
    t = tp + ts
    l = 0
    x_p, x_s = x_prompt.reshape(tp, D), x_sample.reshape(ts, D)

    wl = w_in[l]
    o_dt = D + D + 2 * BCW
    o_q = o_dt + SSM_HEADS
    w_main = jnp.concatenate([wl[:, :2 * D], wl[:, o_q:], wl[:, 2 * D:o_dt]], axis=1).astype(BF16)
    w_dt = jnp.pad(wl[:, o_dt:o_q], ((0, 0), (0, LANES - SSM_HEADS))).astype(BF16)
    xb = jnp.concatenate([x_p.astype(BF16), x_s.astype(BF16)], axis=0)
    proj = _matmul(xb, w_main, 1024, 1024, F32)
    dt_raw = _matmul(xb, w_dt, 1024, LANES, F32)

    lb_all = jnp.cumsum(jax.nn.softmax(hgrn_lb_logits.astype(F32), axis=0), axis=0)
    head_of = jnp.arange(D, dtype=jnp.int32) // SSM_P
    pad128 = lambda v: jnp.pad(v, (0, LANES - v.shape[0])).reshape(1, LANES)
    p = {
        "cwx": conv_w[l][:, :D], "cwbc": conv_w[l][:, D:],
        "cbx": conv_b[l][:D].reshape(1, D), "cbbc": conv_b[l][D:].reshape(1, 2 * BCW),
        "dtb": pad128(dt_bias[l]), "alog": a_log[l], "alog_e": a_log[l][head_of].reshape(1, D),
        "dskip_e": d_skip[l][head_of].reshape(1, D),
        "ssm_norm_g": ssm_norm_g[l].reshape(1, D),
        "e64": (jnp.arange(LANES, dtype=jnp.int32)[:, None] == head_of[None, :]).astype(BF16),
        "lb": lb_all[l].reshape(1, D), "hgrn_norm_g": hgrn_norm_g[l].reshape(1, D),
        "w_a": w_a[l].astype(BF16), "w_b": w_b[l].astype(BF16), "w_out": w_out[l].astype(BF16),
        "ln1_g": ln1_g[l].reshape(1, D), "ln1_b": ln1_b[l].reshape(1, D),
        "ln2_g": ln2_g[l].reshape(1, D), "ln2_b": ln2_b[l].reshape(1, D),
        "wr": jnp.pad(jnp.concatenate([router_g_w[l], router_e_w[l]], axis=1),
                      ((0, 0), (0, LANES - N_GROUPS - N_EXPERTS))),
        "br": pad128(jnp.concatenate([router_g_b[l], router_e_b[l]])),
    }

    ya_p, ssm_p, conv_p = _ssd(proj, dt_raw, 0, jnp.zeros((nbp, D, SSM_N), F32),
                               jnp.zeros((nbp, CONV_K - 1, D + 2 * BCW), F32), p, nbp, seqp)
    ya_s, ssm_s, conv_s = _ssd(proj, dt_raw, tp, state_ssm[l].reshape(nbs, D, SSM_N), state_conv[l],
                               p, nbs, seqs)
    yb_p, hg_p = _gla(proj, 0, jnp.zeros((nbp, D, HG_DK), F32), p, nbp, seqp)
    yb_s, hg_s = _gla(proj, tp, state_hgrn[l].reshape(nbs, D, HG_DK), p, nbs, seqs)

    merged = _gate(ya_p, ya_s, yb_p, yb_s, proj, p)
    x1, route = _merge(merged, x_p, x_s, p)

    e_flat = route[:, :2].astype(jnp.int32).reshape(-1)
    n_tiles = (2 * t) // MOE_TM + N_EXPERTS
    te, first, nv, src, dest = _route_meta(e_flat, MOE_TM, n_tiles)
    es = _experts(x1, te, first, nv, src, exp_w1[l], exp_w3[l], exp_w2[l], n_tiles, MOE_TM)
    dest_km = dest.reshape(t // ROW_TM, ROW_TM, 2).transpose(2, 0, 1).reshape(-1)
    x2_p, x2_s = _combine(es, dest_km, x1, route, p, tp)

    y_prompt = x2_p.reshape(nbp, seqp, D)
    y_sample = x2_s.reshape(nbs, seqs, D)
    return (y_prompt, y_sample,
            ssm_p.reshape(1, nbp, SSM_HEADS, SSM_P, SSM_N), conv_p[None],
            hg_p.reshape(1, nbp, HG_HEADS, HG_DK, HG_DK),
            ssm_s.reshape(1, nbs, SSM_HEADS, SSM_P, SSM_N), conv_s[None],
            hg_s.reshape(1, nbs, HG_HEADS, HG_DK, HG_DK))
```

```python
import functools

import jax
import jax.numpy as jnp
from jax import lax
from jax.experimental import pallas as pl
from jax.experimental.pallas import tpu as pltpu

F32 = jnp.float32
BF16 = jnp.bfloat16
HI = lax.Precision.HIGHEST

D = 2048
SSM_HEADS = 32
SSM_P = 64
SSM_N = 128
SSM_G = 4
GW = D // SSM_G
BCW = SSM_G * SSM_N
CONV_K = 4
HG_HEADS = 16
HG_DK = 128
N_GROUPS = 4
EPG = 8
N_EXPERTS = 32
D_EXPERT = 512
EPS = 1e-5
ALPHA = 2.0 ** 0.25
NEG = -1e30
LANES = 128

VMEM_LIMIT = 56 * 1024 * 1024
SSD_CHUNK = 128
GLA_CHUNK = 64
STEP_ROWS = 64
PROJ_TM = 2304
PROJ_TN = 512
MOE_TM = 128
ROW_TM = 256
MERGE_TM = 512
MERGE_TN = 512
CARRY = 8


def _sigmoid(x):
    return 0.5 * jnp.tanh(0.5 * x) + 0.5


def _silu(x):
    return x * _sigmoid(x)


def _softplus(x):
    return jnp.maximum(x, 0.0) + jnp.log(1.0 + jnp.exp(-jnp.abs(x)))


def _split3(x):
    hi = x.astype(BF16)
    r = x - hi.astype(F32)
    mid = r.astype(BF16)
    lo = (r - mid.astype(F32)).astype(BF16)
    return hi, mid, lo


def _dot3_rhs(m_bf16, x):
    return sum(jnp.dot(m_bf16, part, preferred_element_type=F32) for part in _split3(x))


def _dot3_lhs(x, m_bf16):
    return sum(jnp.dot(part, m_bf16, preferred_element_type=F32) for part in _split3(x))


def _seq_masks(r, seq_len):
    shift = seq_len.bit_length() - 1
    row = lax.broadcasted_iota(jnp.int32, (r, r), 0)
    col = lax.broadcasted_iota(jnp.int32, (r, r), 1)
    same = lax.shift_right_logical(row, shift) == lax.shift_right_logical(col, shift)
    return same & (row >= col), same & (row <= col)


def _seq_totals(cum, nseq, seq_len):
    w = cum.shape[1]
    parts = [jnp.broadcast_to(cum[(b + 1) * seq_len - 1:(b + 1) * seq_len, :], (seq_len, w)) for b in range(nseq)]
    return parts[0] if nseq == 1 else jnp.concatenate(parts, axis=0)


def _pad_rows(x, rows):
    if x.shape[0] == rows:
        return x
    return jnp.concatenate([x, jnp.zeros((rows - x.shape[0], x.shape[1]), x.dtype)], axis=0)


def _row_mask(x, b, seq_len):
    row = lax.broadcasted_iota(jnp.int32, x.shape, 0)
    return jnp.where((row >= b * seq_len) & (row < (b + 1) * seq_len), x, jnp.zeros_like(x))


def _mm_kernel(x_ref, w_ref, o_ref):
    o_ref[...] = jnp.dot(x_ref[...], w_ref[...].astype(BF16), preferred_element_type=F32).astype(o_ref.dtype)


def _mm_shift_kernel(x_ref, w_ref, wnext_ref, o_ref, *, shift):
    tn = w_ref.shape[1]
    wcat = jnp.concatenate([w_ref[...], wnext_ref[...]], axis=1)
    w = wcat[:, shift:shift + tn].astype(BF16)
    o_ref[...] = jnp.dot(x_ref[...], w, preferred_element_type=F32).astype(o_ref.dtype)


def _project(x, w, col0, n, tm, tn, out_dtype):
    m, k = x.shape
    shift = col0 % LANES
    base = (col0 - shift) // tn
    assert (col0 - shift) % tn == 0 and n % tn == 0 and m % tm == 0
    params = pltpu.CompilerParams(dimension_semantics=("parallel", "arbitrary"), vmem_limit_bytes=VMEM_LIMIT)
    x_spec = pl.BlockSpec((tm, k), lambda i, j: (i, 0))
    w_spec = pl.BlockSpec((k, tn), lambda i, j: (0, base + j))
    out_spec = pl.BlockSpec((tm, tn), lambda i, j: (i, j))
    out_shape = jax.ShapeDtypeStruct((m, n), out_dtype)
    if shift == 0:
        return pl.pallas_call(_mm_kernel, grid=(m // tm, n // tn), in_specs=[x_spec, w_spec],
                              out_specs=out_spec, out_shape=out_shape, compiler_params=params, name="proj")(x, w)
    per = tn // LANES
    wnext_spec = pl.BlockSpec((k, LANES), lambda i, j: (0, (base + j + 1) * per))
    return pl.pallas_call(functools.partial(_mm_shift_kernel, shift=shift), grid=(m // tm, n // tn),
                          in_specs=[x_spec, w_spec, wnext_spec], out_specs=out_spec, out_shape=out_shape,
                          compiler_params=params, name="proj_shift")(x, w, w)


def _conv_silu(buf_ref, carry0_ref, u, w_ref, b_ref, nseq, seq_len, first):
    ch = u.shape[1]

    @pl.when(first)
    def _():
        buf_ref[:, 0:CARRY, :] = carry0_ref[...]
    u3 = u.reshape(nseq, seq_len, ch)
    buf_ref[:, CARRY:CARRY + seq_len, :] = u3
    acc = b_ref[...] + w_ref[CONV_K - 1:CONV_K, :] * u3
    for k in range(CONV_K - 1):
        off = CARRY - (CONV_K - 1) + k
        acc = acc + w_ref[k:k + 1, :] * buf_ref[:, off:off + seq_len, :]
    tail = buf_ref[:, seq_len:seq_len + CARRY, :]
    buf_ref[:, 0:CARRY, :] = tail
    return _silu(acc).reshape(nseq * seq_len, ch), tail


def _pair_cols(tile):
    lane = lax.broadcasted_iota(jnp.int32, tile.shape, 1)
    swapped = pltpu.roll(tile, SSM_P, axis=1)
    return jnp.where(lane < SSM_P, tile, swapped), jnp.where(lane < SSM_P, swapped, tile)


def _ssd_kernel(xp_ref, bcp_ref, z_ref, dtr_ref, dtrt_ref, h0_ref, cx0_ref, cbc0_ref,
                cwx_ref, cwbc_ref, cbx_ref, cbbc_ref, dtb_ref, dtbt_ref, alogt_ref, aloge_ref,
                dskip_ref, normg_ref, e64_ref,
                y_ref, hout_ref, ctx_ref, ctbc_ref,
                xbuf, bcbuf, h_ref, *, nseq, seq_len):
    r = nseq * seq_len
    rp = max(r, LANES)
    zi = pl.program_id(1)
    first = zi == 0

    @pl.when(first)
    def _():
        h_ref[...] = h0_ref[...]

    xs, tail_x = _conv_silu(xbuf, cx0_ref, xp_ref[...].astype(F32), cwx_ref, cbx_ref, nseq, seq_len, first)
    bc, tail_bc = _conv_silu(bcbuf, cbc0_ref, bcp_ref[...].astype(F32), cwbc_ref, cbbc_ref, nseq, seq_len, first)
    ctx_ref[...] = tail_x
    ctbc_ref[...] = tail_bc

    causal, anti = _seq_masks(r, seq_len)
    dt = _softplus(dtr_ref[...] + dtb_ref[...])
    dt_e = _dot3_lhs(dt, e64_ref[...])
    loga_e = dt_e * (-jnp.exp(aloge_ref[...]))
    cum_e = _dot3_rhs(causal.astype(BF16), loga_e)
    loga_t = _softplus(dtrt_ref[0] + dtbt_ref[...]) * (-jnp.exp(alogt_ref[...]))
    cum_t = _dot3_lhs(loga_t, anti.astype(BF16))
    tot_e = _seq_totals(cum_e, nseq, seq_len)
    xdt = xs * dt_e
    in_scale = jnp.exp(cum_e)
    xw = xdt * jnp.exp(tot_e - cum_e)
    xw_t = _pad_rows(xw, rp).T.astype(BF16)
    lane = lax.broadcasted_iota(jnp.int32, (r, LANES), 1)

    y_groups = []
    for g in range(SSM_G):
        bg = bc[:, g * SSM_N:(g + 1) * SSM_N]
        cg = bc[:, BCW + g * SSM_N:BCW + (g + 1) * SSM_N]
        cb = lax.dot_general(cg, bg, (((1,), (1,)), ((), ())), preferred_element_type=F32)
        rows = slice(g * GW, (g + 1) * GW)
        pieces = []
        for j in range(GW // LANES):
            col0 = g * GW + j * LANES
            xpair = xdt[:, col0:col0 + LANES]
            ms, rhs = [], []
            for half, colb in enumerate(_pair_cols(cum_e[:, col0:col0 + LANES])):
                head = col0 // SSM_P + half
                seg = colb[:, :r] - cum_t[head:head + 1, :]
                ms.append((cb * jnp.exp(jnp.where(causal, seg, NEG))).astype(BF16))
                rhs.append(jnp.where(lane >= SSM_P if half else lane < SSM_P, xpair, 0.0).astype(BF16))
            if r % LANES == 0:
                pieces.append(jnp.dot(jnp.concatenate(ms, axis=1), jnp.concatenate(rhs, axis=0),
                                      preferred_element_type=F32))
            else:
                pieces.append(jnp.dot(ms[0], rhs[0], preferred_element_type=F32)
                              + jnp.dot(ms[1], rhs[1], preferred_element_type=F32))
        y_intra = jnp.concatenate(pieces, axis=1)
        y_inter = []
        for b in range(nseq):
            rb = slice(b * seq_len, (b + 1) * seq_len)
            hg = h_ref[b, rows, :]
            y_inter.append(lax.dot_general(cg[rb, :], hg, (((1,), (1,)), ((), ())),
                                           preferred_element_type=F32))
            bmask = bg if nseq == 1 else _row_mask(bg, b, seq_len)
            st = jnp.dot(xw_t[rows, :], _pad_rows(bmask, rp).astype(BF16), preferred_element_type=F32)
            dec8 = jnp.exp(tot_e[b * seq_len:b * seq_len + 8, rows])
            for j in range(GW // LANES):
                for half, dcol in enumerate(_pair_cols(dec8[:, j * LANES:(j + 1) * LANES])):
                    h8 = 2 * j + half
                    hr = slice(g * GW + h8 * SSM_P, g * GW + (h8 + 1) * SSM_P)
                    h_ref[b, hr, :] = dcol[0:1, :] * h_ref[b, hr, :] + st[h8 * SSM_P:(h8 + 1) * SSM_P, :]
        y_inter = y_inter[0] if nseq == 1 else jnp.concatenate(y_inter, axis=0)
        y_groups.append(y_intra + y_inter * in_scale[:, rows])
    y = jnp.concatenate(y_groups, axis=1) + dskip_ref[...] * xs
    yz = y * _silu(z_ref[...].astype(F32))
    outs = []
    for g in range(SSM_G):
        blk = yz[:, g * GW:(g + 1) * GW]
        ms = jnp.sum(blk * blk, axis=1, keepdims=True) * (1.0 / GW)
        outs.append(blk * lax.rsqrt(ms + EPS))
    y_ref[...] = (jnp.concatenate(outs, axis=1) * normg_ref[...]).astype(y_ref.dtype)

    @pl.when(zi == pl.num_programs(1) - 1)
    def _():
        hout_ref[...] = h_ref[...]


def _ssd(proj, dt_raw, row0, h0, conv0, p, nb, seq):
    seq_len = SSD_CHUNK if seq % SSD_CHUNK == 0 else seq
    nseq = 1 if seq_len == SSD_CHUNK else max(1, min(nb, STEP_ROWS // seq_len))
    nz = seq // seq_len
    r = nseq * seq_len
    t = nb * seq
    r0 = row0 // r
    nblk = t // r
    dtrt = dt_raw[row0:row0 + t, :SSM_HEADS].reshape(nblk, r, SSM_HEADS).transpose(0, 2, 1)
    pad_rows = CARRY - (CONV_K - 1)
    cx0 = jnp.pad(conv0[:, :, :D], ((0, 0), (pad_rows, 0), (0, 0)))
    cbc0 = jnp.pad(conv0[:, :, D:], ((0, 0), (pad_rows, 0), (0, 0)))
    rowblk = lambda col: pl.BlockSpec((r, D), lambda b, z: (r0 + b * nz + z, col))
    const = lambda shape: pl.BlockSpec(shape, lambda b, z: tuple(0 for _ in shape))
    per_b = lambda shape: pl.BlockSpec((nseq,) + shape, lambda b, z: (b, 0, 0))
    y, hout, ctx, ctbc = pl.pallas_call(
        functools.partial(_ssd_kernel, nseq=nseq, seq_len=seq_len),
        grid=(nb // nseq, nz),
        in_specs=[rowblk(1),
                  pl.BlockSpec((r, 2 * BCW), lambda b, z: (r0 + b * nz + z, 4)),
                  rowblk(0),
                  pl.BlockSpec((r, LANES), lambda b, z: (r0 + b * nz + z, 0)),
                  pl.BlockSpec((1, SSM_HEADS, r), lambda b, z: (b * nz + z, 0, 0)),
                  per_b((D, SSM_N)), per_b((CARRY, D)), per_b((CARRY, 2 * BCW)),
                  const((CONV_K, D)), const((CONV_K, 2 * BCW)), const((1, D)), const((1, 2 * BCW)),
                  const((1, LANES)), const((SSM_HEADS, r)), const((SSM_HEADS, r)), const((1, D)),
                  const((1, D)), const((1, D)), const((LANES, D))],
        out_specs=[pl.BlockSpec((r, D), lambda b, z: (b * nz + z, 0)),
                   per_b((D, SSM_N)), per_b((CARRY, D)), per_b((CARRY, 2 * BCW))],
        out_shape=[jax.ShapeDtypeStruct((t, D), BF16),
                   jax.ShapeDtypeStruct((nb, D, SSM_N), F32),
                   jax.ShapeDtypeStruct((nb, CARRY, D), F32),
                   jax.ShapeDtypeStruct((nb, CARRY, 2 * BCW), F32)],
        scratch_shapes=[pltpu.VMEM((nseq, seq_len + CARRY, D), F32),
                        pltpu.VMEM((nseq, seq_len + CARRY, 2 * BCW), F32),
                        pltpu.VMEM((nseq, D, SSM_N), F32)],
        compiler_params=pltpu.CompilerParams(
            dimension_semantics=("parallel", "arbitrary"), vmem_limit_bytes=VMEM_LIMIT),
        name="ssd",
    )(proj, proj, proj, dt_raw, dtrt, h0, cx0, cbc0,
      p["cwx"], p["cwbc"], p["cbx"], p["cbbc"], p["dtb"],
      jnp.broadcast_to(p["dtb"][0, :SSM_HEADS, None], (SSM_HEADS, r)),
      jnp.broadcast_to(p["alog"][:, None], (SSM_HEADS, r)),
      p["alog_e"], p["dskip_e"], p["ssm_norm_g"], p["e64"])
    conv_new = jnp.concatenate([ctx[:, pad_rows:], ctbc[:, pad_rows:]], axis=-1)
    return y, hout, conv_new


def _gla_kernel(q_ref, f_ref, i_ref, g_ref, s0_ref, lb_ref, normg_ref, y_ref, sout_ref, s_ref,
                *, nseq, seq_len):
    r = nseq * seq_len
    rp = max(r, LANES)
    zi = pl.program_id(1)

    @pl.when(zi == 0)
    def _():
        s_ref[...] = s0_ref[...]

    lb = lb_ref[...]
    f = lb + (1.0 - lb) * _sigmoid(f_ref[...].astype(F32))
    k = 1.0 - f
    q = _silu(q_ref[...].astype(F32))
    vb = i_ref[...]
    causal, _ = _seq_masks(r, seq_len)
    bc = _dot3_rhs(causal.astype(BF16), jnp.log(f))
    tot = _seq_totals(bc, nseq, seq_len)
    qe = q * jnp.exp(bc)
    ke = k * jnp.exp(-bc)
    kd_t = _pad_rows(ke * jnp.exp(tot), rp).T.astype(BF16)
    gate = _silu(g_ref[...].astype(F32))
    outs = []
    for h in range(HG_HEADS):
        sl = slice(h * HG_DK, (h + 1) * HG_DK)
        qh = qe[:, sl]
        attn = lax.dot_general(qh, ke[:, sl], (((1,), (1,)), ((), ())), preferred_element_type=F32)
        attn = jnp.where(causal, attn, 0.0)
        o = jnp.dot(attn.astype(BF16), vb[:, sl], preferred_element_type=F32)
        o_inter = []
        for b in range(nseq):
            rb = slice(b * seq_len, (b + 1) * seq_len)
            sh = s_ref[b, sl, :]
            o_inter.append(jnp.dot(qh[rb, :], sh, preferred_element_type=F32))
            vmask = vb[:, sl] if nseq == 1 else _row_mask(vb[:, sl], b, seq_len)
            st = jnp.dot(kd_t[sl, :], _pad_rows(vmask, rp), preferred_element_type=F32)
            dec = jnp.exp(tot[b * seq_len:b * seq_len + 1, sl])
            dec_col = jnp.broadcast_to(dec, (HG_DK, HG_DK)).T
            s_ref[b, sl, :] = dec_col * sh + st
        o = o + (o_inter[0] if nseq == 1 else jnp.concatenate(o_inter, axis=0))
        ms = jnp.sum(o * o, axis=1, keepdims=True) * (1.0 / HG_DK)
        outs.append(o * lax.rsqrt(ms + EPS))
    y_ref[...] = (jnp.concatenate(outs, axis=1) * normg_ref[...] * gate).astype(y_ref.dtype)

    @pl.when(zi == pl.num_programs(1) - 1)
    def _():
        sout_ref[...] = s_ref[...]


def _gla(proj, row0, s0, p, nb, seq):
    seq_len = GLA_CHUNK if seq % GLA_CHUNK == 0 else seq
    nseq = 1 if seq_len == GLA_CHUNK else max(1, min(nb, STEP_ROWS // seq_len))
    nz = seq // seq_len
    r = nseq * seq_len
    r0 = row0 // r
    rowblk = lambda col: pl.BlockSpec((r, D), lambda b, z: (r0 + b * nz + z, col))
    const = lambda shape: pl.BlockSpec(shape, lambda b, z: tuple(0 for _ in shape))
    per_b = lambda shape: pl.BlockSpec((nseq,) + shape, lambda b, z: (b, 0, 0))
    return pl.pallas_call(
        functools.partial(_gla_kernel, nseq=nseq, seq_len=seq_len),
        grid=(nb // nseq, nz),
        in_specs=[rowblk(0), rowblk(1), rowblk(2), rowblk(3), per_b((D, HG_DK)), const((1, D)), const((1, D))],
        out_specs=[pl.BlockSpec((r, D), lambda b, z: (b * nz + z, 0)), per_b((D, HG_DK))],
        out_shape=[jax.ShapeDtypeStruct((nb * seq, D), BF16), jax.ShapeDtypeStruct((nb, D, HG_DK), F32)],
        scratch_shapes=[pltpu.VMEM((nseq, D, HG_DK), F32)],
        compiler_params=pltpu.CompilerParams(
            dimension_semantics=("parallel", "arbitrary"), vmem_limit_bytes=VMEM_LIMIT),
        name="gla",
    )(proj, proj, proj, proj, s0, p["lb"], p["hgrn_norm_g"])


def _layer_norm(x, g, b):
    mu = jnp.mean(x, axis=1, keepdims=True)
    xc = x - mu
    var = jnp.mean(xc * xc, axis=1, keepdims=True)
    return xc * lax.rsqrt(var + EPS) * g + b


def _gate_kernel(yap_ref, yas_ref, ybp_ref, ybs_ref, ga_ref, gb_ref, wa_ref, wb_ref, o_ref, *, n_prompt):
    is_prompt = pl.program_id(0) < n_prompt
    ya = jnp.where(is_prompt, yap_ref[...], yas_ref[...])
    yb = jnp.where(is_prompt, ybp_ref[...], ybs_ref[...])
    a = jnp.dot(ya, wa_ref[...], preferred_element_type=F32)
    b = jnp.dot(yb, wb_ref[...], preferred_element_type=F32)
    ga = _sigmoid(ga_ref[...].astype(F32))
    gb = _sigmoid(gb_ref[...].astype(F32))
    o_ref[...] = (ga * a + gb * b).astype(o_ref.dtype)


def _gate(ya_p, ya_s, yb_p, yb_s, proj, p):
    tm, tn = MERGE_TM, MERGE_TN
    n_p = ya_p.shape[0] // tm
    t = ya_p.shape[0] + ya_s.shape[0]
    prow = pl.BlockSpec((tm, D), lambda i, j: (jnp.minimum(i, n_p - 1), 0))
    srow = pl.BlockSpec((tm, D), lambda i, j: (jnp.maximum(i - n_p, 0), 0))
    gcol = lambda seg: pl.BlockSpec((tm, tn), lambda i, j: (i, seg * (D // tn) + j))
    wcol = pl.BlockSpec((D, tn), lambda i, j: (0, j))
    return pl.pallas_call(
        functools.partial(_gate_kernel, n_prompt=n_p),
        grid=(t // tm, D // tn),
        in_specs=[prow, srow, prow, srow, gcol(4), gcol(5), wcol, wcol],
        out_specs=pl.BlockSpec((tm, tn), lambda i, j: (i, j)),
        out_shape=jax.ShapeDtypeStruct((t, D), BF16),
        compiler_params=pltpu.CompilerParams(
            dimension_semantics=("parallel", "arbitrary"), vmem_limit_bytes=VMEM_LIMIT),
        name="gate",
    )(ya_p, ya_s, yb_p, yb_s, proj, proj, p["w_a"], p["w_b"])


def _merge_kernel(m_ref, xp_ref, xs_ref, wo_ref, g1_ref, b1_ref, wrh_ref, wrl_ref, br_ref,
                  x1_ref, route_ref, *, n_prompt):
    x = jnp.where(pl.program_id(0) < n_prompt, xp_ref[...], xs_ref[...])
    mix = jnp.dot(m_ref[...], wo_ref[...], preferred_element_type=F32)
    x1 = _layer_norm(ALPHA * x + mix, g1_ref[...], b1_ref[...])
    x1_ref[...] = x1
    hi = x1.astype(BF16)
    lo = (x1 - hi.astype(F32)).astype(BF16)
    logits = (jnp.dot(hi, wrh_ref[...], preferred_element_type=F32)
              + jnp.dot(lo, wrh_ref[...], preferred_element_type=F32)
              + jnp.dot(hi, wrl_ref[...], preferred_element_type=F32) + br_ref[...])
    lane = lax.broadcasted_iota(jnp.int32, logits.shape, 1).astype(F32)
    gl = jnp.where(lane < N_GROUPS, logits, NEG)
    gmax = jnp.max(gl, axis=1, keepdims=True)
    gsel = jnp.min(jnp.where(gl == gmax, lane, 1e9), axis=1, keepdims=True)
    p_grp = 1.0 / jnp.sum(jnp.exp(gl - gmax), axis=1, keepdims=True)
    lo = N_GROUPS + gsel * EPG
    el = jnp.where((lane >= lo) & (lane < lo + EPG), logits, NEG)
    v0 = jnp.max(el, axis=1, keepdims=True)
    i0 = jnp.min(jnp.where(el == v0, lane, 1e9), axis=1, keepdims=True)
    el2 = jnp.where(lane == i0, NEG, el)
    v1 = jnp.max(el2, axis=1, keepdims=True)
    i1 = jnp.min(jnp.where(el2 == v1, lane, 1e9), axis=1, keepdims=True)
    e1 = jnp.exp(v1 - v0)
    p0 = 1.0 / (1.0 + e1)
    p1 = e1 * p0
    route = jnp.where(lane == 0, i0 - N_GROUPS,
                      jnp.where(lane == 1, i1 - N_GROUPS,
                                jnp.where(lane == 2, p_grp * p0,
                                          jnp.where(lane == 3, p_grp * p1, 0.0))))
    route_ref[...] = route


def _merge(merged, x_p, x_s, p):
    tm = MERGE_TM
    n_p = x_p.shape[0] // tm
    t = x_p.shape[0] + x_s.shape[0]
    row = pl.BlockSpec((tm, D), lambda i: (i, 0))
    prow = pl.BlockSpec((tm, D), lambda i: (jnp.minimum(i, n_p - 1), 0))
    srow = pl.BlockSpec((tm, D), lambda i: (jnp.maximum(i - n_p, 0), 0))
    res = lambda shape: pl.BlockSpec(shape, lambda i: (0, 0), pipeline_mode=pl.Buffered(1))
    wr_hi = p["wr"].astype(BF16)
    wr_lo = (p["wr"] - wr_hi.astype(F32)).astype(BF16)
    return pl.pallas_call(
        functools.partial(_merge_kernel, n_prompt=n_p),
        grid=(t // tm,),
        in_specs=[row, prow, srow, res((D, D)), res((1, D)), res((1, D)),
                  res((D, LANES)), res((D, LANES)), res((1, LANES))],
        out_specs=[row, pl.BlockSpec((tm, LANES), lambda i: (i, 0))],
        out_shape=[jax.ShapeDtypeStruct((t, D), F32), jax.ShapeDtypeStruct((t, LANES), F32)],
        compiler_params=pltpu.CompilerParams(
            dimension_semantics=("parallel",), vmem_limit_bytes=VMEM_LIMIT),
        name="merge",
    )(merged, x_p, x_s, p["w_out"], p["ln1_g"], p["ln1_b"], wr_hi, wr_lo, p["br"])


def _row_gather(src_hbm, dst, sem, idx_ref, base, n, unroll=8):
    def body(j, carry):
        r = idx_ref[base + j]
        pltpu.make_async_copy(src_hbm.at[pl.ds(r, 1), :], dst.at[pl.ds(j, 1), :], sem).start()
        return carry
    lax.fori_loop(0, n, body, 0, unroll=unroll)


def _expert_kernel(te_ref, first_ref, nv_ref, src_ref, x1_hbm, w1_ref, w3_ref, w2_ref, o_ref,
                   xbuf, sems, w1b, w3b, w2b, *, tm):
    i = pl.program_id(0)
    nv = nv_ref[0]
    slot = lax.rem(i, 2)

    def wait(s):
        pltpu.make_async_copy(x1_hbm.at[pl.ds(0, tm), :], xbuf.at[s], sems.at[s]).wait()

    @pl.when(i == 0)
    def _():
        _row_gather(x1_hbm, xbuf.at[0], sems.at[0], src_ref, 0, tm)

    @pl.when(first_ref[i] == 1)
    def _():
        w1b[...] = w1_ref[0].astype(BF16)
        w3b[...] = w3_ref[0].astype(BF16)
        w2b[...] = w2_ref[0].astype(BF16)

    @pl.when(i < nv)
    def _():
        _row_gather(x1_hbm, xbuf.at[1 - slot], sems.at[1 - slot], src_ref, (i + 1) * tm, tm, unroll=True)
        wait(slot)
        xb = xbuf[slot].astype(BF16)
        h1 = jnp.dot(xb, w1b[...], preferred_element_type=F32)
        h3 = jnp.dot(xb, w3b[...], preferred_element_type=F32)
        h = (_silu(h1) * h3).astype(BF16)
        o_ref[...] = jnp.dot(h, w2b[...], preferred_element_type=F32)

    @pl.when(i == nv)
    def _():
        wait(slot)

    @pl.when(i >= nv)
    def _():
        o_ref[...] = jnp.zeros_like(o_ref)


def _experts(x1, te, first, nv, src, w1, w3, w2, n_tiles, tm):
    wspec = lambda shape: pl.BlockSpec((1,) + shape, lambda i, te, first, nv, src: (te[i], 0, 0))
    return pl.pallas_call(
        functools.partial(_expert_kernel, tm=tm),
        grid_spec=pltpu.PrefetchScalarGridSpec(
            num_scalar_prefetch=4,
            grid=(n_tiles,),
            in_specs=[pl.BlockSpec(memory_space=pl.ANY),
                      wspec((D, D_EXPERT)), wspec((D, D_EXPERT)), wspec((D_EXPERT, D))],
            out_specs=pl.BlockSpec((tm, D), lambda i, te, first, nv, src: (i, 0)),
            scratch_shapes=[pltpu.VMEM((2, tm, D), F32), pltpu.SemaphoreType.DMA((2,)),
                            pltpu.VMEM((D, D_EXPERT), BF16), pltpu.VMEM((D, D_EXPERT), BF16),
                            pltpu.VMEM((D_EXPERT, D), BF16)]),
        out_shape=jax.ShapeDtypeStruct((n_tiles * tm, D), F32),
        compiler_params=pltpu.CompilerParams(
            dimension_semantics=("arbitrary",), vmem_limit_bytes=VMEM_LIMIT, disable_bounds_checks=True),
        name="experts",
    )(te, first, nv, src, x1, w1, w3, w2)


def _combine_kernel(dest_ref, es_hbm, x1_ref, route_ref, g2_ref, b2_ref, op_ref, os_ref, gbuf, sems,
                    *, tm, n_prompt):
    i = pl.program_id(0)
    n = pl.num_programs(0)
    slot = lax.rem(i, 2)

    def start(step, s):
        for k in range(2):
            _row_gather(es_hbm, gbuf.at[s, k], sems.at[s], dest_ref, (k * n + step) * tm, tm)

    @pl.when(i == 0)
    def _():
        start(0, 0)

    @pl.when(i + 1 < n)
    def _():
        start(i + 1, 1 - slot)

    for k in range(2):
        pltpu.make_async_copy(es_hbm.at[pl.ds(0, tm), :], gbuf.at[slot, k], sems.at[slot]).wait()
    route = route_ref[...]
    w0 = route[:, 2:3]
    w1 = route[:, 3:4]
    moe = w0 * gbuf[slot, 0] + w1 * gbuf[slot, 1]
    out = _layer_norm(ALPHA * x1_ref[...] + moe, g2_ref[...], b2_ref[...])

    @pl.when(i < n_prompt)
    def _():
        op_ref[...] = out

    @pl.when(i >= n_prompt)
    def _():
        os_ref[...] = out


def _combine(es, dest_km, x1, route, p, tp):
    t = x1.shape[0]
    tm = ROW_TM
    n_p = tp // tm
    return pl.pallas_call(
        functools.partial(_combine_kernel, tm=tm, n_prompt=n_p),
        grid_spec=pltpu.PrefetchScalarGridSpec(
            num_scalar_prefetch=1,
            grid=(t // tm,),
            in_specs=[pl.BlockSpec(memory_space=pl.ANY),
                      pl.BlockSpec((tm, D), lambda i, d: (i, 0)),
                      pl.BlockSpec((tm, LANES), lambda i, d: (i, 0)),
                      pl.BlockSpec((1, D), lambda i, d: (0, 0)),
                      pl.BlockSpec((1, D), lambda i, d: (0, 0))],
            out_specs=[pl.BlockSpec((tm, D), lambda i, d: (jnp.minimum(i, n_p - 1), 0)),
                       pl.BlockSpec((tm, D), lambda i, d: (jnp.maximum(i - n_p, 0), 0))],
            scratch_shapes=[pltpu.VMEM((2, 2, tm, D), F32), pltpu.SemaphoreType.DMA((2,))]),
        out_shape=[jax.ShapeDtypeStruct((tp, D), F32), jax.ShapeDtypeStruct((t - tp, D), F32)],
        compiler_params=pltpu.CompilerParams(
            dimension_semantics=("arbitrary",), vmem_limit_bytes=VMEM_LIMIT, disable_bounds_checks=True),
        name="combine",
    )(dest_km, es, x1, route, p["ln2_g"], p["ln2_b"])


def _route_meta(e_flat, tm, n_tiles):
    a = e_flat.shape[0]
    ids = jnp.arange(N_EXPERTS, dtype=jnp.int32)
    onehot = (e_flat[:, None] == ids[None, :]).astype(jnp.int32)
    csum = jnp.cumsum(onehot, axis=0)
    rank = jnp.sum((csum - onehot) * onehot, axis=1)
    counts = csum[-1]
    tiles_per = (counts + tm - 1) // tm
    tile_end = jnp.cumsum(tiles_per)
    tile_start = tile_end - tiles_per
    nv = tile_end[-1]
    dest = tile_start[e_flat] * tm + rank
    tile_ids = jnp.arange(n_tiles, dtype=jnp.int32)
    te = jnp.sum((tile_end[None, :] <= tile_ids[:, None]).astype(jnp.int32), axis=1)
    te_last = jnp.sum((tile_end <= nv - 1).astype(jnp.int32))
    valid = tile_ids < nv
    te = jnp.where(valid, te, te_last).astype(jnp.int32)
    first = (valid & (tile_ids == tile_start[te])).astype(jnp.int32)
    src = jnp.zeros((n_tiles * tm,), jnp.int32).at[dest].set(jnp.arange(a, dtype=jnp.int32) // 2)
    return te, first, nv.reshape(1).astype(jnp.int32), src, dest.astype(jnp.int32)


def kernel(x_prompt, x_sample, state_ssm, state_conv, state_hgrn, w_in, conv_w, conv_b, dt_bias, a_log, d_skip, ssm_norm_g, hgrn_lb_logits, hgrn_norm_g, w_a, w_b, w_out, ln1_g, ln1_b, router_g_w, router_g_b, router_e_w, router_e_b, exp_w1, exp_w3, exp_w2, ln2_g, ln2_b):
    nbp, seqp, _ = x_prompt.shape
    nbs, seqs, _ = x_sample.shape
    tp, ts = nbp * seqp, nbs * seqs
    t = tp + ts
    l = 0
    x_p, x_s = x_prompt.reshape(tp, D), x_sample.reshape(ts, D)

    wl = w_in[l]
    o_dt = D + D + 2 * BCW
    o_q = o_dt + SSM_HEADS
    xb = jnp.concatenate([x_p.astype(BF16), x_s.astype(BF16)], axis=0)
    tm = PROJ_TM if t % PROJ_TM == 0 else 1024
    proj_a = _project(xb, wl, 0, o_dt, tm, PROJ_TN, BF16)
    dt_raw = _project(xb, wl, o_dt, LANES, tm, LANES, F32)
    proj_b = _project(xb, wl, o_q, 6 * D, tm, PROJ_TN, BF16)

    lb_all = jnp.cumsum(jax.nn.softmax(hgrn_lb_logits.astype(F32), axis=0), axis=0)
    head_of = jnp.arange(D, dtype=jnp.int32) // SSM_P
    pad128 = lambda v: jnp.pad(v, (0, LANES - v.shape[0])).reshape(1, LANES)
    p = {
        "cwx": conv_w[l][:, :D], "cwbc": conv_w[l][:, D:],
        "cbx": conv_b[l][:D].reshape(1, D), "cbbc": conv_b[l][D:].reshape(1, 2 * BCW),
        "dtb": pad128(dt_bias[l]), "alog": a_log[l], "alog_e": a_log[l][head_of].reshape(1, D),
        "dskip_e": d_skip[l][head_of].reshape(1, D),
        "ssm_norm_g": ssm_norm_g[l].reshape(1, D),
        "e64": (jnp.arange(LANES, dtype=jnp.int32)[:, None] == head_of[None, :]).astype(BF16),
        "lb": lb_all[l].reshape(1, D), "hgrn_norm_g": hgrn_norm_g[l].reshape(1, D),
        "w_a": w_a[l].astype(BF16), "w_b": w_b[l].astype(BF16), "w_out": w_out[l].astype(BF16),
        "ln1_g": ln1_g[l].reshape(1, D), "ln1_b": ln1_b[l].reshape(1, D),
        "ln2_g": ln2_g[l].reshape(1, D), "ln2_b": ln2_b[l].reshape(1, D),
        "wr": jnp.pad(jnp.concatenate([router_g_w[l], router_e_w[l]], axis=1),
                      ((0, 0), (0, LANES - N_GROUPS - N_EXPERTS))),
        "br": pad128(jnp.concatenate([router_g_b[l], router_e_b[l]])),
    }

    ya_p, ssm_p, conv_p = _ssd(proj_a, dt_raw, 0, jnp.zeros((nbp, D, SSM_N), F32),
                               jnp.zeros((nbp, CONV_K - 1, D + 2 * BCW), F32), p, nbp, seqp)
    ya_s, ssm_s, conv_s = _ssd(proj_a, dt_raw, tp, state_ssm[l].reshape(nbs, D, SSM_N), state_conv[l],
                               p, nbs, seqs)
    yb_p, hg_p = _gla(proj_b, 0, jnp.zeros((nbp, D, HG_DK), F32), p, nbp, seqp)
    yb_s, hg_s = _gla(proj_b, tp, state_hgrn[l].reshape(nbs, D, HG_DK), p, nbs, seqs)

    merged = _gate(ya_p, ya_s, yb_p, yb_s, proj_b, p)
    x1, route = _merge(merged, x_p, x_s, p)

    e_flat = route[:, :2].astype(jnp.int32).reshape(-1)
    n_tiles = (2 * t) // MOE_TM + N_EXPERTS + 1
    te, first, nv, src, dest = _route_meta(e_flat, MOE_TM, n_tiles)
    es = _experts(x1, te, first, nv, src, exp_w1[l], exp_w3[l], exp_w2[l], n_tiles, MOE_TM)
    dest_km = dest.reshape(t // ROW_TM, ROW_TM, 2).transpose(2, 0, 1).reshape(-1)
    x2_p, x2_s = _combine(es, dest_km, x1, route, p, tp)

    y_prompt = x2_p.reshape(nbp, seqp, D)
    y_sample = x2_s.reshape(nbs, seqs, D)
    return (y_prompt, y_sample,
            ssm_p.reshape(1, nbp, SSM_HEADS, SSM_P, SSM_N), conv_p[None],
            hg_p.reshape(1, nbp, HG_HEADS, HG_DK, HG_DK),
            ssm_s.reshape(1, nbs, SSM_HEADS, SSM_P, SSM_N), conv_s[None],
            hg_s.reshape(1, nbs, HG_HEADS, HG_DK, HG_DK))
```

```python
import functools

import jax
import jax.numpy as jnp
from jax import lax
from jax.experimental import pallas as pl
from jax.experimental.pallas import tpu as pltpu

F32 = jnp.float32
BF16 = jnp.bfloat16
HI = lax.Precision.HIGHEST

D = 2048
SSM_HEADS = 32
SSM_P = 64
SSM_N = 128
SSM_G = 4
GW = D // SSM_G
BCW = SSM_G * SSM_N
CONV_K = 4
HG_HEADS = 16
HG_DK = 128
N_GROUPS = 4
EPG = 8
N_EXPERTS = 32
D_EXPERT = 512
EPS = 1e-5
ALPHA = 2.0 ** 0.25
NEG = -1e30
LANES = 128
ROW_TILES = D // LANES

VMEM_LIMIT = 56 * 1024 * 1024
SSD_CHUNK = 128
GLA_CHUNK = 64
STEP_ROWS = 64
PROJ_TM = 2304
PROJ_TN = 512
MOE_TM = 128
ROW_TM = 256
MERGE_TM = 512
MERGE_TN = 512
CARRY = 8


def _sigmoid(x):
    return 0.5 * jnp.tanh(0.5 * x) + 0.5


def _silu(x):
    return x * _sigmoid(x)


def _softplus(x):
    return jnp.maximum(x, 0.0) + jnp.log(1.0 + jnp.exp(-jnp.abs(x)))


def _split3(x):
    hi = x.astype(BF16)
    r = x - hi.astype(F32)
    mid = r.astype(BF16)
    lo = (r - mid.astype(F32)).astype(BF16)
    return hi, mid, lo


def _dot3_rhs(m_bf16, x):
    return sum(jnp.dot(m_bf16, part, preferred_element_type=F32) for part in _split3(x))


def _dot3_lhs(x, m_bf16):
    return sum(jnp.dot(part, m_bf16, preferred_element_type=F32) for part in _split3(x))


def _seq_masks(r, seq_len):
    shift = seq_len.bit_length() - 1
    row = lax.broadcasted_iota(jnp.int32, (r, r), 0)
    col = lax.broadcasted_iota(jnp.int32, (r, r), 1)
    same = lax.shift_right_logical(row, shift) == lax.shift_right_logical(col, shift)
    return same & (row >= col), same & (row <= col)


def _seq_totals(cum, nseq, seq_len):
    w = cum.shape[1]
    parts = [jnp.broadcast_to(cum[(b + 1) * seq_len - 1:(b + 1) * seq_len, :], (seq_len, w)) for b in range(nseq)]
    return parts[0] if nseq == 1 else jnp.concatenate(parts, axis=0)


def _pad_rows(x, rows):
    if x.shape[0] == rows:
        return x
    return jnp.concatenate([x, jnp.zeros((rows - x.shape[0], x.shape[1]), x.dtype)], axis=0)


def _to_token_tiles(ref, val):
    rows = val.shape[0]
    for s in range(ROW_TILES):
        ref[pl.ds(s, rows, stride=ROW_TILES), :] = val[:, s * LANES:(s + 1) * LANES]


def _from_token_tiles(ref):
    rows = ref.shape[0] // ROW_TILES
    return jnp.concatenate([ref[pl.ds(s, rows, stride=ROW_TILES), :] for s in range(ROW_TILES)], axis=1)


def _row_mask(x, b, seq_len):
    row = lax.broadcasted_iota(jnp.int32, x.shape, 0)
    return jnp.where((row >= b * seq_len) & (row < (b + 1) * seq_len), x, jnp.zeros_like(x))


def _mm_nt_kernel(x_ref, wt_ref, o_ref):
    w = wt_ref[...].astype(BF16)
    o_ref[...] = lax.dot_general(x_ref[...], w, (((1,), (1,)), ((), ())),
                                 preferred_element_type=F32).astype(o_ref.dtype)


def _project(x, wt, row0, n, tm, tn, out_dtype):
    m, k = x.shape
    assert n % tn == 0 and m % tm == 0 and row0 % 8 == 0
    if row0 % tn == 0:
        w_spec = pl.BlockSpec((tn, k), lambda i, j: (row0 // tn + j, 0))
    else:
        w_spec = pl.BlockSpec((pl.Element(tn), pl.Element(k)),
                              lambda i, j: (pl.multiple_of(row0 + j * tn, 8), 0))
    return pl.pallas_call(
        _mm_nt_kernel,
        grid=(m // tm, n // tn),
        in_specs=[pl.BlockSpec((tm, k), lambda i, j: (i, 0)), w_spec],
        out_specs=pl.BlockSpec((tm, tn), lambda i, j: (i, j)),
        out_shape=jax.ShapeDtypeStruct((m, n), out_dtype),
        compiler_params=pltpu.CompilerParams(
            dimension_semantics=("parallel", "arbitrary"), vmem_limit_bytes=VMEM_LIMIT),
        name="proj",
    )(x, wt)


def _conv_silu(buf_ref, carry0_ref, u, w_ref, b_ref, nseq, seq_len, first):
    ch = u.shape[1]

    @pl.when(first)
    def _():
        buf_ref[:, 0:CARRY, :] = carry0_ref[...]
    u3 = u.reshape(nseq, seq_len, ch)
    buf_ref[:, CARRY:CARRY + seq_len, :] = u3
    acc = b_ref[...] + w_ref[CONV_K - 1:CONV_K, :] * u3
    for k in range(CONV_K - 1):
        off = CARRY - (CONV_K - 1) + k
        acc = acc + w_ref[k:k + 1, :] * buf_ref[:, off:off + seq_len, :]
    tail = buf_ref[:, seq_len:seq_len + CARRY, :]
    buf_ref[:, 0:CARRY, :] = tail
    return _silu(acc).reshape(nseq * seq_len, ch), tail


def _pair_cols(tile):
    lane = lax.broadcasted_iota(jnp.int32, tile.shape, 1)
    swapped = pltpu.roll(tile, SSM_P, axis=1)
    return jnp.where(lane < SSM_P, tile, swapped), jnp.where(lane < SSM_P, swapped, tile)


def _ssd_kernel(xp_ref, bcp_ref, z_ref, dtr_ref, dtrt_ref, h0_ref, cx0_ref, cbc0_ref,
                cwx_ref, cwbc_ref, cbx_ref, cbbc_ref, dtb_ref, dtbt_ref, alogt_ref, aloge_ref,
                dskip_ref, normg_ref, e64_ref,
                y_ref, hout_ref, ctx_ref, ctbc_ref,
                xbuf, bcbuf, h_ref, *, nseq, seq_len):
    r = nseq * seq_len
    rp = max(r, LANES)
    zi = pl.program_id(1)
    first = zi == 0

    @pl.when(first)
    def _():
        h_ref[...] = h0_ref[...]

    xs, tail_x = _conv_silu(xbuf, cx0_ref, xp_ref[...].astype(F32), cwx_ref, cbx_ref, nseq, seq_len, first)
    bc, tail_bc = _conv_silu(bcbuf, cbc0_ref, bcp_ref[...].astype(F32), cwbc_ref, cbbc_ref, nseq, seq_len, first)
    ctx_ref[...] = tail_x
    ctbc_ref[...] = tail_bc

    causal, anti = _seq_masks(r, seq_len)
    dt = _softplus(dtr_ref[...] + dtb_ref[...])
    dt_e = _dot3_lhs(dt, e64_ref[...])
    loga_e = dt_e * (-jnp.exp(aloge_ref[...]))
    cum_e = _dot3_rhs(causal.astype(BF16), loga_e)
    loga_t = _softplus(dtrt_ref[0] + dtbt_ref[...]) * (-jnp.exp(alogt_ref[...]))
    cum_t = _dot3_lhs(loga_t, anti.astype(BF16))
    tot_e = _seq_totals(cum_e, nseq, seq_len)
    xdt = xs * dt_e
    in_scale = jnp.exp(cum_e)
    xw = xdt * jnp.exp(tot_e - cum_e)
    xw_t = _pad_rows(xw, rp).T.astype(BF16)
    lane = lax.broadcasted_iota(jnp.int32, (r, LANES), 1)

    y_groups = []
    for g in range(SSM_G):
        bg = bc[:, g * SSM_N:(g + 1) * SSM_N]
        cg = bc[:, BCW + g * SSM_N:BCW + (g + 1) * SSM_N]
        cb = lax.dot_general(cg, bg, (((1,), (1,)), ((), ())), preferred_element_type=F32)
        rows = slice(g * GW, (g + 1) * GW)
        pieces = []
        for j in range(GW // LANES):
            col0 = g * GW + j * LANES
            xpair = xdt[:, col0:col0 + LANES]
            ms, rhs = [], []
            for half, colb in enumerate(_pair_cols(cum_e[:, col0:col0 + LANES])):
                head = col0 // SSM_P + half
                seg = colb[:, :r] - cum_t[head:head + 1, :]
                ms.append((cb * jnp.exp(jnp.where(causal, seg, NEG))).astype(BF16))
                rhs.append(jnp.where(lane >= SSM_P if half else lane < SSM_P, xpair, 0.0).astype(BF16))
            if r % LANES == 0:
                pieces.append(jnp.dot(jnp.concatenate(ms, axis=1), jnp.concatenate(rhs, axis=0),
                                      preferred_element_type=F32))
            else:
                pieces.append(jnp.dot(ms[0], rhs[0], preferred_element_type=F32)
                              + jnp.dot(ms[1], rhs[1], preferred_element_type=F32))
        y_intra = jnp.concatenate(pieces, axis=1)
        y_inter = []
        for b in range(nseq):
            rb = slice(b * seq_len, (b + 1) * seq_len)
            hg = h_ref[b, rows, :]
            y_inter.append(lax.dot_general(cg[rb, :], hg, (((1,), (1,)), ((), ())),
                                           preferred_element_type=F32))
            bmask = bg if nseq == 1 else _row_mask(bg, b, seq_len)
            st = jnp.dot(xw_t[rows, :], _pad_rows(bmask, rp).astype(BF16), preferred_element_type=F32)
            dec8 = jnp.exp(tot_e[b * seq_len:b * seq_len + 8, rows])
            for j in range(GW // LANES):
                for half, dcol in enumerate(_pair_cols(dec8[:, j * LANES:(j + 1) * LANES])):
                    h8 = 2 * j + half
                    hr = slice(g * GW + h8 * SSM_P, g * GW + (h8 + 1) * SSM_P)
                    h_ref[b, hr, :] = dcol[0:1, :] * h_ref[b, hr, :] + st[h8 * SSM_P:(h8 + 1) * SSM_P, :]
        y_inter = y_inter[0] if nseq == 1 else jnp.concatenate(y_inter, axis=0)
        y_groups.append(y_intra + y_inter * in_scale[:, rows])
    y = jnp.concatenate(y_groups, axis=1) + dskip_ref[...] * xs
    yz = y * _silu(z_ref[...].astype(F32))
    outs = []
    for g in range(SSM_G):
        blk = yz[:, g * GW:(g + 1) * GW]
        ms = jnp.sum(blk * blk, axis=1, keepdims=True) * (1.0 / GW)
        outs.append(blk * lax.rsqrt(ms + EPS))
    y_ref[...] = (jnp.concatenate(outs, axis=1) * normg_ref[...]).astype(y_ref.dtype)

    @pl.when(zi == pl.num_programs(1) - 1)
    def _():
        hout_ref[...] = h_ref[...]


def _ssd(proj, dt_raw, row0, h0, conv0, p, nb, seq):
    seq_len = SSD_CHUNK if seq % SSD_CHUNK == 0 else seq
    nseq = 1 if seq_len == SSD_CHUNK else max(1, min(nb, STEP_ROWS // seq_len))
    nz = seq // seq_len
    r = nseq * seq_len
    t = nb * seq
    r0 = row0 // r
    nblk = t // r
    dtrt = dt_raw[row0:row0 + t, :SSM_HEADS].reshape(nblk, r, SSM_HEADS).transpose(0, 2, 1)
    pad_rows = CARRY - (CONV_K - 1)
    cx0 = jnp.pad(conv0[:, :, :D], ((0, 0), (pad_rows, 0), (0, 0)))
    cbc0 = jnp.pad(conv0[:, :, D:], ((0, 0), (pad_rows, 0), (0, 0)))
    rowblk = lambda col: pl.BlockSpec((r, D), lambda b, z: (r0 + b * nz + z, col))
    const = lambda shape: pl.BlockSpec(shape, lambda b, z: tuple(0 for _ in shape))
    per_b = lambda shape: pl.BlockSpec((nseq,) + shape, lambda b, z: (b, 0, 0))
    y, hout, ctx, ctbc = pl.pallas_call(
        functools.partial(_ssd_kernel, nseq=nseq, seq_len=seq_len),
        grid=(nb // nseq, nz),
        in_specs=[rowblk(1),
                  pl.BlockSpec((r, 2 * BCW), lambda b, z: (r0 + b * nz + z, 4)),
                  rowblk(0),
                  pl.BlockSpec((r, LANES), lambda b, z: (r0 + b * nz + z, 0)),
                  pl.BlockSpec((1, SSM_HEADS, r), lambda b, z: (b * nz + z, 0, 0)),
                  per_b((D, SSM_N)), per_b((CARRY, D)), per_b((CARRY, 2 * BCW)),
                  const((CONV_K, D)), const((CONV_K, 2 * BCW)), const((1, D)), const((1, 2 * BCW)),
                  const((1, LANES)), const((SSM_HEADS, r)), const((SSM_HEADS, r)), const((1, D)),
                  const((1, D)), const((1, D)), const((LANES, D))],
        out_specs=[pl.BlockSpec((r, D), lambda b, z: (b * nz + z, 0)),
                   per_b((D, SSM_N)), per_b((CARRY, D)), per_b((CARRY, 2 * BCW))],
        out_shape=[jax.ShapeDtypeStruct((t, D), BF16),
                   jax.ShapeDtypeStruct((nb, D, SSM_N), F32),
                   jax.ShapeDtypeStruct((nb, CARRY, D), F32),
                   jax.ShapeDtypeStruct((nb, CARRY, 2 * BCW), F32)],
        scratch_shapes=[pltpu.VMEM((nseq, seq_len + CARRY, D), F32),
                        pltpu.VMEM((nseq, seq_len + CARRY, 2 * BCW), F32),
                        pltpu.VMEM((nseq, D, SSM_N), F32)],
        compiler_params=pltpu.CompilerParams(
            dimension_semantics=("parallel", "arbitrary"), vmem_limit_bytes=VMEM_LIMIT),
        name="ssd",
    )(proj, proj, proj, dt_raw, dtrt, h0, cx0, cbc0,
      p["cwx"], p["cwbc"], p["cbx"], p["cbbc"], p["dtb"],
      jnp.broadcast_to(p["dtb"][0, :SSM_HEADS, None], (SSM_HEADS, r)),
      jnp.broadcast_to(p["alog"][:, None], (SSM_HEADS, r)),
      p["alog_e"], p["dskip_e"], p["ssm_norm_g"], p["e64"])
    conv_new = jnp.concatenate([ctx[:, pad_rows:], ctbc[:, pad_rows:]], axis=-1)
    return y, hout, conv_new


def _gla_kernel(q_ref, f_ref, i_ref, g_ref, s0_ref, lb_ref, normg_ref, y_ref, sout_ref, s_ref,
                *, nseq, seq_len):
    r = nseq * seq_len
    rp = max(r, LANES)
    zi = pl.program_id(1)

    @pl.when(zi == 0)
    def _():
        s_ref[...] = s0_ref[...]

    lb = lb_ref[...]
    f = lb + (1.0 - lb) * _sigmoid(f_ref[...].astype(F32))
    k = 1.0 - f
    q = _silu(q_ref[...].astype(F32))
    vb = i_ref[...]
    causal, _ = _seq_masks(r, seq_len)
    bc = _dot3_rhs(causal.astype(BF16), jnp.log(f))
    tot = _seq_totals(bc, nseq, seq_len)
    qe = q * jnp.exp(bc)
    ke = k * jnp.exp(-bc)
    kd_t = _pad_rows(ke * jnp.exp(tot), rp).T.astype(BF16)
    gate = _silu(g_ref[...].astype(F32))
    outs = []
    for h in range(HG_HEADS):
        sl = slice(h * HG_DK, (h + 1) * HG_DK)
        qh = qe[:, sl]
        attn = lax.dot_general(qh, ke[:, sl], (((1,), (1,)), ((), ())), preferred_element_type=F32)
        attn = jnp.where(causal, attn, 0.0)
        o = jnp.dot(attn.astype(BF16), vb[:, sl], preferred_element_type=F32)
        o_inter = []
        for b in range(nseq):
            rb = slice(b * seq_len, (b + 1) * seq_len)
            sh = s_ref[b, sl, :]
            o_inter.append(jnp.dot(qh[rb, :], sh, preferred_element_type=F32))
            vmask = vb[:, sl] if nseq == 1 else _row_mask(vb[:, sl], b, seq_len)
            st = jnp.dot(kd_t[sl, :], _pad_rows(vmask, rp), preferred_element_type=F32)
            dec = jnp.exp(tot[b * seq_len:b * seq_len + 1, sl])
            dec_col = jnp.broadcast_to(dec, (HG_DK, HG_DK)).T
            s_ref[b, sl, :] = dec_col * sh + st
        o = o + (o_inter[0] if nseq == 1 else jnp.concatenate(o_inter, axis=0))
        ms = jnp.sum(o * o, axis=1, keepdims=True) * (1.0 / HG_DK)
        outs.append(o * lax.rsqrt(ms + EPS))
    y_ref[...] = (jnp.concatenate(outs, axis=1) * normg_ref[...] * gate).astype(y_ref.dtype)

    @pl.when(zi == pl.num_programs(1) - 1)
    def _():
        sout_ref[...] = s_ref[...]


def _gla(proj, row0, s0, p, nb, seq):
    seq_len = GLA_CHUNK if seq % GLA_CHUNK == 0 else seq
    nseq = 1 if seq_len == GLA_CHUNK else max(1, min(nb, STEP_ROWS // seq_len))
    nz = seq // seq_len
    r = nseq * seq_len
    r0 = row0 // r
    rowblk = lambda col: pl.BlockSpec((r, D), lambda b, z: (r0 + b * nz + z, col))
    const = lambda shape: pl.BlockSpec(shape, lambda b, z: tuple(0 for _ in shape))
    per_b = lambda shape: pl.BlockSpec((nseq,) + shape, lambda b, z: (b, 0, 0))
    return pl.pallas_call(
        functools.partial(_gla_kernel, nseq=nseq, seq_len=seq_len),
        grid=(nb // nseq, nz),
        in_specs=[rowblk(0), rowblk(1), rowblk(2), rowblk(3), per_b((D, HG_DK)), const((1, D)), const((1, D))],
        out_specs=[pl.BlockSpec((r, D), lambda b, z: (b * nz + z, 0)), per_b((D, HG_DK))],
        out_shape=[jax.ShapeDtypeStruct((nb * seq, D), BF16), jax.ShapeDtypeStruct((nb, D, HG_DK), F32)],
        scratch_shapes=[pltpu.VMEM((nseq, D, HG_DK), F32)],
        compiler_params=pltpu.CompilerParams(
            dimension_semantics=("parallel", "arbitrary"), vmem_limit_bytes=VMEM_LIMIT),
        name="gla",
    )(proj, proj, proj, proj, s0, p["lb"], p["hgrn_norm_g"])


def _layer_norm(x, g, b):
    mu = jnp.mean(x, axis=1, keepdims=True)
    xc = x - mu
    var = jnp.mean(xc * xc, axis=1, keepdims=True)
    return xc * lax.rsqrt(var + EPS) * g + b


def _gate_kernel(yap_ref, yas_ref, ybp_ref, ybs_ref, ga_ref, gb_ref, wa_ref, wb_ref, o_ref, *, n_prompt):
    is_prompt = pl.program_id(0) < n_prompt
    ya = jnp.where(is_prompt, yap_ref[...], yas_ref[...])
    yb = jnp.where(is_prompt, ybp_ref[...], ybs_ref[...])
    a = jnp.dot(ya, wa_ref[...], preferred_element_type=F32)
    b = jnp.dot(yb, wb_ref[...], preferred_element_type=F32)
    ga = _sigmoid(ga_ref[...].astype(F32))
    gb = _sigmoid(gb_ref[...].astype(F32))
    o_ref[...] = (ga * a + gb * b).astype(o_ref.dtype)


def _gate(ya_p, ya_s, yb_p, yb_s, proj, p):
    tm, tn = MERGE_TM, MERGE_TN
    n_p = ya_p.shape[0] // tm
    t = ya_p.shape[0] + ya_s.shape[0]
    prow = pl.BlockSpec((tm, D), lambda i, j: (jnp.minimum(i, n_p - 1), 0))
    srow = pl.BlockSpec((tm, D), lambda i, j: (jnp.maximum(i - n_p, 0), 0))
    gcol = lambda seg: pl.BlockSpec((tm, tn), lambda i, j: (i, seg * (D // tn) + j))
    wcol = pl.BlockSpec((D, tn), lambda i, j: (0, j))
    return pl.pallas_call(
        functools.partial(_gate_kernel, n_prompt=n_p),
        grid=(t // tm, D // tn),
        in_specs=[prow, srow, prow, srow, gcol(4), gcol(5), wcol, wcol],
        out_specs=pl.BlockSpec((tm, tn), lambda i, j: (i, j)),
        out_shape=jax.ShapeDtypeStruct((t, D), BF16),
        compiler_params=pltpu.CompilerParams(
            dimension_semantics=("parallel", "arbitrary"), vmem_limit_bytes=VMEM_LIMIT),
        name="gate",
    )(ya_p, ya_s, yb_p, yb_s, proj, proj, p["w_a"], p["w_b"])


def _merge_kernel(m_ref, xp_ref, xs_ref, wo_ref, g1_ref, b1_ref, wrh_ref, wrl_ref, br_ref,
                  x1_ref, route_ref, *, n_prompt):
    x = jnp.where(pl.program_id(0) < n_prompt, xp_ref[...], xs_ref[...])
    mix = jnp.dot(m_ref[...], wo_ref[...], preferred_element_type=F32)
    x1 = _layer_norm(ALPHA * x + mix, g1_ref[...], b1_ref[...])
    _to_token_tiles(x1_ref, x1)
    hi = x1.astype(BF16)
    lo = (x1 - hi.astype(F32)).astype(BF16)
    logits = (jnp.dot(hi, wrh_ref[...], preferred_element_type=F32)
              + jnp.dot(lo, wrh_ref[...], preferred_element_type=F32)
              + jnp.dot(hi, wrl_ref[...], preferred_element_type=F32) + br_ref[...])
    lane = lax.broadcasted_iota(jnp.int32, logits.shape, 1).astype(F32)
    gl = jnp.where(lane < N_GROUPS, logits, NEG)
    gmax = jnp.max(gl, axis=1, keepdims=True)
    gsel = jnp.min(jnp.where(gl == gmax, lane, 1e9), axis=1, keepdims=True)
    p_grp = 1.0 / jnp.sum(jnp.exp(gl - gmax), axis=1, keepdims=True)
    lo = N_GROUPS + gsel * EPG
    el = jnp.where((lane >= lo) & (lane < lo + EPG), logits, NEG)
    v0 = jnp.max(el, axis=1, keepdims=True)
    i0 = jnp.min(jnp.where(el == v0, lane, 1e9), axis=1, keepdims=True)
    el2 = jnp.where(lane == i0, NEG, el)
    v1 = jnp.max(el2, axis=1, keepdims=True)
    i1 = jnp.min(jnp.where(el2 == v1, lane, 1e9), axis=1, keepdims=True)
    e1 = jnp.exp(v1 - v0)
    p0 = 1.0 / (1.0 + e1)
    p1 = e1 * p0
    route = jnp.where(lane == 0, i0 - N_GROUPS,
                      jnp.where(lane == 1, i1 - N_GROUPS,
                                jnp.where(lane == 2, p_grp * p0,
                                          jnp.where(lane == 3, p_grp * p1, 0.0))))
    route_ref[...] = route


def _merge(merged, x_p, x_s, p):
    tm = MERGE_TM
    n_p = x_p.shape[0] // tm
    t = x_p.shape[0] + x_s.shape[0]
    row = pl.BlockSpec((tm, D), lambda i: (i, 0))
    prow = pl.BlockSpec((tm, D), lambda i: (jnp.minimum(i, n_p - 1), 0))
    srow = pl.BlockSpec((tm, D), lambda i: (jnp.maximum(i - n_p, 0), 0))
    res = lambda shape: pl.BlockSpec(shape, lambda i: (0, 0), pipeline_mode=pl.Buffered(1))
    wr_hi = p["wr"].astype(BF16)
    wr_lo = (p["wr"] - wr_hi.astype(F32)).astype(BF16)
    return pl.pallas_call(
        functools.partial(_merge_kernel, n_prompt=n_p),
        grid=(t // tm,),
        in_specs=[row, prow, srow, res((D, D)), res((1, D)), res((1, D)),
                  res((D, LANES)), res((D, LANES)), res((1, LANES))],
        out_specs=[pl.BlockSpec((tm * ROW_TILES, LANES), lambda i: (i, 0)),
                   pl.BlockSpec((tm, LANES), lambda i: (i, 0))],
        out_shape=[jax.ShapeDtypeStruct((t * ROW_TILES, LANES), F32), jax.ShapeDtypeStruct((t, LANES), F32)],
        compiler_params=pltpu.CompilerParams(
            dimension_semantics=("parallel",), vmem_limit_bytes=VMEM_LIMIT),
        name="merge",
    )(merged, x_p, x_s, p["w_out"], p["ln1_g"], p["ln1_b"], wr_hi, wr_lo, p["br"])


def _row_gather(src_hbm, dst, sem, idx_ref, base, n, unroll=8):
    def body(j, carry):
        r = idx_ref[base + j]
        pltpu.make_async_copy(src_hbm.at[pl.ds(pl.multiple_of(r * ROW_TILES, ROW_TILES), ROW_TILES), :],
                              dst.at[pl.ds(j * ROW_TILES, ROW_TILES), :], sem).start()
        return carry
    lax.fori_loop(0, n, body, 0, unroll=unroll)


def _expert_kernel(te_ref, first_ref, nv_ref, src_ref, x1_hbm, w1_ref, w3_ref, w2_ref, o_ref,
                   xbuf, sems, w1b, w3b, w2b, *, tm):
    i = pl.program_id(0)
    nv = nv_ref[0]
    slot = lax.rem(i, 2)

    def wait(s):
        pltpu.make_async_copy(x1_hbm.at[pl.ds(0, tm * ROW_TILES), :], xbuf.at[s], sems.at[s]).wait()

    @pl.when(i == 0)
    def _():
        _row_gather(x1_hbm, xbuf.at[0], sems.at[0], src_ref, 0, tm)

    @pl.when(first_ref[i] == 1)
    def _():
        w1b[...] = w1_ref[0].astype(BF16)
        w3b[...] = w3_ref[0].astype(BF16)
        w2b[...] = w2_ref[0].astype(BF16)

    @pl.when(i < nv)
    def _():
        _row_gather(x1_hbm, xbuf.at[1 - slot], sems.at[1 - slot], src_ref, (i + 1) * tm, tm, unroll=True)
        wait(slot)
        xb = _from_token_tiles(xbuf.at[slot]).astype(BF16)
        h1 = jnp.dot(xb, w1b[...], preferred_element_type=F32)
        h3 = jnp.dot(xb, w3b[...], preferred_element_type=F32)
        h = (_silu(h1) * h3).astype(BF16)
        _to_token_tiles(o_ref, jnp.dot(h, w2b[...], preferred_element_type=F32))

    @pl.when(i == nv)
    def _():
        wait(slot)

    @pl.when(i >= nv)
    def _():
        o_ref[...] = jnp.zeros_like(o_ref)


def _experts(x1, te, first, nv, src, w1, w3, w2, n_tiles, tm):
    wspec = lambda shape: pl.BlockSpec((1,) + shape, lambda i, te, first, nv, src: (te[i], 0, 0))
    return pl.pallas_call(
        functools.partial(_expert_kernel, tm=tm),
        grid_spec=pltpu.PrefetchScalarGridSpec(
            num_scalar_prefetch=4,
            grid=(n_tiles,),
            in_specs=[pl.BlockSpec(memory_space=pl.ANY),
                      wspec((D, D_EXPERT)), wspec((D, D_EXPERT)), wspec((D_EXPERT, D))],
            out_specs=pl.BlockSpec((tm * ROW_TILES, LANES), lambda i, te, first, nv, src: (i, 0)),
            scratch_shapes=[pltpu.VMEM((2, tm * ROW_TILES, LANES), F32), pltpu.SemaphoreType.DMA((2,)),
                            pltpu.VMEM((D, D_EXPERT), BF16), pltpu.VMEM((D, D_EXPERT), BF16),
                            pltpu.VMEM((D_EXPERT, D), BF16)]),
        out_shape=jax.ShapeDtypeStruct((n_tiles * tm * ROW_TILES, LANES), F32),
        compiler_params=pltpu.CompilerParams(
            dimension_semantics=("arbitrary",), vmem_limit_bytes=VMEM_LIMIT, disable_bounds_checks=True),
        name="experts",
    )(te, first, nv, src, x1, w1, w3, w2)


def _combine_kernel(dest_ref, es_hbm, x1_ref, route_ref, g2_ref, b2_ref, op_ref, os_ref, gbuf, sems,
                    *, tm, n_prompt):
    i = pl.program_id(0)
    n = pl.num_programs(0)
    slot = lax.rem(i, 2)

    def start(step, s):
        for k in range(2):
            _row_gather(es_hbm, gbuf.at[s, k], sems.at[s], dest_ref, (k * n + step) * tm, tm)

    @pl.when(i == 0)
    def _():
        start(0, 0)

    @pl.when(i + 1 < n)
    def _():
        start(i + 1, 1 - slot)

    for k in range(2):
        pltpu.make_async_copy(es_hbm.at[pl.ds(0, tm * ROW_TILES), :], gbuf.at[slot, k], sems.at[slot]).wait()
    route = route_ref[...]
    w0 = route[:, 2:3]
    w1 = route[:, 3:4]
    moe = w0 * _from_token_tiles(gbuf.at[slot, 0]) + w1 * _from_token_tiles(gbuf.at[slot, 1])
    out = _layer_norm(ALPHA * _from_token_tiles(x1_ref) + moe, g2_ref[...], b2_ref[...])

    @pl.when(i < n_prompt)
    def _():
        op_ref[...] = out

    @pl.when(i >= n_prompt)
    def _():
        os_ref[...] = out


def _combine(es, dest_km, x1, route, p, tp):
    t = x1.shape[0] // ROW_TILES
    tm = ROW_TM
    n_p = tp // tm
    return pl.pallas_call(
        functools.partial(_combine_kernel, tm=tm, n_prompt=n_p),
        grid_spec=pltpu.PrefetchScalarGridSpec(
            num_scalar_prefetch=1,
            grid=(t // tm,),
            in_specs=[pl.BlockSpec(memory_space=pl.ANY),
                      pl.BlockSpec((tm * ROW_TILES, LANES), lambda i, d: (i, 0)),
                      pl.BlockSpec((tm, LANES), lambda i, d: (i, 0)),
                      pl.BlockSpec((1, D), lambda i, d: (0, 0)),
                      pl.BlockSpec((1, D), lambda i, d: (0, 0))],
            out_specs=[pl.BlockSpec((tm, D), lambda i, d: (jnp.minimum(i, n_p - 1), 0)),
                       pl.BlockSpec((tm, D), lambda i, d: (jnp.maximum(i - n_p, 0), 0))],
            scratch_shapes=[pltpu.VMEM((2, 2, tm * ROW_TILES, LANES), F32), pltpu.SemaphoreType.DMA((2,))]),
        out_shape=[jax.ShapeDtypeStruct((tp, D), F32), jax.ShapeDtypeStruct((t - tp, D), F32)],
        compiler_params=pltpu.CompilerParams(
            dimension_semantics=("arbitrary",), vmem_limit_bytes=VMEM_LIMIT, disable_bounds_checks=True),
        name="combine",
    )(dest_km, es, x1, route, p["ln2_g"], p["ln2_b"])


def _route_meta(e_flat, tm, n_tiles):
    a = e_flat.shape[0]
    ids = jnp.arange(N_EXPERTS, dtype=jnp.int32)
    onehot = (e_flat[:, None] == ids[None, :]).astype(jnp.int32)
    csum = jnp.cumsum(onehot, axis=0)
    rank = jnp.sum((csum - onehot) * onehot, axis=1)
    counts = csum[-1]
    tiles_per = (counts + tm - 1) // tm
    tile_end = jnp.cumsum(tiles_per)
    tile_start = tile_end - tiles_per
    nv = tile_end[-1]
    dest = tile_start[e_flat] * tm + rank
    tile_ids = jnp.arange(n_tiles, dtype=jnp.int32)
    te = jnp.sum((tile_end[None, :] <= tile_ids[:, None]).astype(jnp.int32), axis=1)
    te_last = jnp.sum((tile_end <= nv - 1).astype(jnp.int32))
    valid = tile_ids < nv
    te = jnp.where(valid, te, te_last).astype(jnp.int32)
    first = (valid & (tile_ids == tile_start[te])).astype(jnp.int32)
    src = jnp.zeros((n_tiles * tm,), jnp.int32).at[dest].set(jnp.arange(a, dtype=jnp.int32) // 2)
    return te, first, nv.reshape(1).astype(jnp.int32), src, dest.astype(jnp.int32)


def kernel(x_prompt, x_sample, state_ssm, state_conv, state_hgrn, w_in, conv_w, conv_b, dt_bias, a_log, d_skip, ssm_norm_g, hgrn_lb_logits, hgrn_norm_g, w_a, w_b, w_out, ln1_g, ln1_b, router_g_w, router_g_b, router_e_w, router_e_b, exp_w1, exp_w3, exp_w2, ln2_g, ln2_b):
    nbp, seqp, _ = x_prompt.shape
    nbs, seqs, _ = x_sample.shape
    tp, ts = nbp * seqp, nbs * seqs
    t = tp + ts
    l = 0
    x_p, x_s = x_prompt.reshape(tp, D), x_sample.reshape(ts, D)

    wt = jnp.transpose(w_in[l])
    o_dt = D + D + 2 * BCW
    o_q = o_dt + SSM_HEADS
    xb = jnp.concatenate([x_p.astype(BF16), x_s.astype(BF16)], axis=0)
    tm = PROJ_TM if t % PROJ_TM == 0 else 1024
    proj_a = _project(xb, wt, 0, o_dt, tm, PROJ_TN, BF16)
    dt_raw = jnp.pad(_project(xb, wt, o_dt, SSM_HEADS, tm, SSM_HEADS, F32), ((0, 0), (0, LANES - SSM_HEADS)))
    proj_b = _project(xb, wt, o_q, 6 * D, tm, PROJ_TN, BF16)

    lb_all = jnp.cumsum(jax.nn.softmax(hgrn_lb_logits.astype(F32), axis=0), axis=0)
    head_of = jnp.arange(D, dtype=jnp.int32) // SSM_P
    pad128 = lambda v: jnp.pad(v, (0, LANES - v.shape[0])).reshape(1, LANES)
    p = {
        "cwx": conv_w[l][:, :D], "cwbc": conv_w[l][:, D:],
        "cbx": conv_b[l][:D].reshape(1, D), "cbbc": conv_b[l][D:].reshape(1, 2 * BCW),
        "dtb": pad128(dt_bias[l]), "alog": a_log[l], "alog_e": a_log[l][head_of].reshape(1, D),
        "dskip_e": d_skip[l][head_of].reshape(1, D),
        "ssm_norm_g": ssm_norm_g[l].reshape(1, D),
        "e64": (jnp.arange(LANES, dtype=jnp.int32)[:, None] == head_of[None, :]).astype(BF16),
        "lb": lb_all[l].reshape(1, D), "hgrn_norm_g": hgrn_norm_g[l].reshape(1, D),
        "w_a": w_a[l].astype(BF16), "w_b": w_b[l].astype(BF16), "w_out": w_out[l].astype(BF16),
        "ln1_g": ln1_g[l].reshape(1, D), "ln1_b": ln1_b[l].reshape(1, D),
        "ln2_g": ln2_g[l].reshape(1, D), "ln2_b": ln2_b[l].reshape(1, D),
        "wr": jnp.pad(jnp.concatenate([router_g_w[l], router_e_w[l]], axis=1),
                      ((0, 0), (0, LANES - N_GROUPS - N_EXPERTS))),
        "br": pad128(jnp.concatenate([router_g_b[l], router_e_b[l]])),
    }

    ya_p, ssm_p, conv_p = _ssd(proj_a, dt_raw, 0, jnp.zeros((nbp, D, SSM_N), F32),
                               jnp.zeros((nbp, CONV_K - 1, D + 2 * BCW), F32), p, nbp, seqp)
    ya_s, ssm_s, conv_s = _ssd(proj_a, dt_raw, tp, state_ssm[l].reshape(nbs, D, SSM_N), state_conv[l],
                               p, nbs, seqs)
    yb_p, hg_p = _gla(proj_b, 0, jnp.zeros((nbp, D, HG_DK), F32), p, nbp, seqp)
    yb_s, hg_s = _gla(proj_b, tp, state_hgrn[l].reshape(nbs, D, HG_DK), p, nbs, seqs)

    merged = _gate(ya_p, ya_s, yb_p, yb_s, proj_b, p)
    x1, route = _merge(merged, x_p, x_s, p)

    e_flat = route[:, :2].astype(jnp.int32).reshape(-1)
    n_tiles = (2 * t) // MOE_TM + N_EXPERTS + 1
    te, first, nv, src, dest = _route_meta(e_flat, MOE_TM, n_tiles)
    es = _experts(x1, te, first, nv, src, exp_w1[l], exp_w3[l], exp_w2[l], n_tiles, MOE_TM)
    dest_km = dest.reshape(t // ROW_TM, ROW_TM, 2).transpose(2, 0, 1).reshape(-1)
    x2_p, x2_s = _combine(es, dest_km, x1, route, p, tp)

    y_prompt = x2_p.reshape(nbp, seqp, D)
    y_sample = x2_s.reshape(nbs, seqs, D)
    return (y_prompt, y_sample,
            ssm_p.reshape(1, nbp, SSM_HEADS, SSM_P, SSM_N), conv_p[None],
            hg_p.reshape(1, nbp, HG_HEADS, HG_DK, HG_DK),
            ssm_s.reshape(1, nbs, SSM_HEADS, SSM_P, SSM_N), conv_s[None],
            hg_s.reshape(1, nbs, HG_HEADS, HG_DK, HG_DK))
```

```python
import functools

import jax
import jax.numpy as jnp
import numpy as np
from jax import lax
from jax.experimental import pallas as pl
from jax.experimental.pallas import tpu as pltpu

F32 = jnp.float32
BF16 = jnp.bfloat16
HI = lax.Precision.HIGHEST

D = 2048
SSM_HEADS = 32
SSM_P = 64
SSM_N = 128
SSM_G = 4
GW = D // SSM_G
BCW = SSM_G * SSM_N
CONV_K = 4
HG_HEADS = 16
HG_DK = 128
N_GROUPS = 4
EPG = 8
N_EXPERTS = 32
D_EXPERT = 512
EPS = 1e-5
ALPHA = 2.0 ** 0.25
NEG = -1e30
LANES = 128
ROW_TILES = D // LANES

VMEM_LIMIT = 56 * 1024 * 1024
SSD_CHUNK = 128
GLA_CHUNK = 64
STEP_ROWS = 64
PROJ_TM = 1024
PROJ_TN = 512
MOE_TM = 128
ROW_TM = 256
MERGE_TM = 512
MERGE_TN = 512
CARRY = 8


def _sigmoid(x):
    return 0.5 * jnp.tanh(0.5 * x) + 0.5


def _silu(x):
    return x * _sigmoid(x)


def _softplus(x):
    return jnp.maximum(x, 0.0) + jnp.log(1.0 + jnp.exp(-jnp.abs(x)))


def _split3(x):
    hi = x.astype(BF16)
    r = x - hi.astype(F32)
    mid = r.astype(BF16)
    lo = (r - mid.astype(F32)).astype(BF16)
    return hi, mid, lo


def _dot3_rhs(m_bf16, x):
    return sum(jnp.dot(m_bf16, part, preferred_element_type=F32) for part in _split3(x))


def _dot3_lhs(x, m_bf16):
    return sum(jnp.dot(part, m_bf16, preferred_element_type=F32) for part in _split3(x))


def _seq_masks(r, seq_len):
    shift = seq_len.bit_length() - 1
    row = lax.broadcasted_iota(jnp.int32, (r, r), 0)
    col = lax.broadcasted_iota(jnp.int32, (r, r), 1)
    same = lax.shift_right_logical(row, shift) == lax.shift_right_logical(col, shift)
    return same & (row >= col), same & (row <= col)


def _seq_totals(cum, nseq, seq_len):
    w = cum.shape[1]
    parts = [jnp.broadcast_to(cum[(b + 1) * seq_len - 1:(b + 1) * seq_len, :], (seq_len, w)) for b in range(nseq)]
    return parts[0] if nseq == 1 else jnp.concatenate(parts, axis=0)


def _pad_rows(x, rows):
    if x.shape[0] == rows:
        return x
    return jnp.concatenate([x, jnp.zeros((rows - x.shape[0], x.shape[1]), x.dtype)], axis=0)


def _to_token_tiles(ref, val):
    rows = val.shape[0]
    for s in range(ROW_TILES):
        ref[pl.ds(s, rows, stride=ROW_TILES), :] = val[:, s * LANES:(s + 1) * LANES]


def _from_token_tiles(ref):
    rows = ref.shape[0] // ROW_TILES
    return jnp.concatenate([ref[pl.ds(s, rows, stride=ROW_TILES), :] for s in range(ROW_TILES)], axis=1)


def _row_mask(x, b, seq_len):
    row = lax.broadcasted_iota(jnp.int32, x.shape, 0)
    return jnp.where((row >= b * seq_len) & (row < (b + 1) * seq_len), x, jnp.zeros_like(x))


def _mm_nt_kernel(xp_ref, xs_ref, wt_ref, o_ref, xb_ref, *, n_prompt):
    @pl.when(pl.program_id(1) == 0)
    def _():
        xb_ref[...] = jnp.where(pl.program_id(0) < n_prompt, xp_ref[...], xs_ref[...]).astype(BF16)
    w = wt_ref[...].astype(BF16)
    o_ref[...] = lax.dot_general(xb_ref[...], w, (((1,), (1,)), ((), ())),
                                 preferred_element_type=F32).astype(o_ref.dtype)


def _project(x_p, x_s, wt, row0, n, tm, tn, out_dtype):
    k = x_p.shape[1]
    m = x_p.shape[0] + x_s.shape[0]
    n_p = x_p.shape[0] // tm
    assert n % tn == 0 and x_p.shape[0] % tm == 0 and x_s.shape[0] % tm == 0 and row0 % 8 == 0
    if row0 % tn == 0:
        w_spec = pl.BlockSpec((tn, k), lambda i, j: (row0 // tn + j, 0))
    else:
        w_spec = pl.BlockSpec((pl.Element(tn), pl.Element(k)),
                              lambda i, j: (pl.multiple_of(row0 + j * tn, 8), 0))
    return pl.pallas_call(
        functools.partial(_mm_nt_kernel, n_prompt=n_p),
        grid=(m // tm, n // tn),
        in_specs=[pl.BlockSpec((tm, k), lambda i, j: (jnp.minimum(i, n_p - 1), 0)),
                  pl.BlockSpec((tm, k), lambda i, j: (jnp.maximum(i - n_p, 0), 0)), w_spec],
        out_specs=pl.BlockSpec((tm, tn), lambda i, j: (i, j)),
        out_shape=jax.ShapeDtypeStruct((m, n), out_dtype),
        scratch_shapes=[pltpu.VMEM((tm, k), BF16)],
        compiler_params=pltpu.CompilerParams(
            dimension_semantics=("parallel", "arbitrary"), vmem_limit_bytes=VMEM_LIMIT),
        name="proj",
    )(x_p, x_s, wt)


def _conv_silu(buf_ref, carry0_ref, u, w_ref, b_ref, nseq, seq_len, first):
    ch = u.shape[1]

    @pl.when(first)
    def _():
        buf_ref[:, 0:CARRY, :] = carry0_ref[...]
    u3 = u.reshape(nseq, seq_len, ch)
    buf_ref[:, CARRY:CARRY + seq_len, :] = u3
    acc = b_ref[...] + w_ref[CONV_K - 1:CONV_K, :] * u3
    for k in range(CONV_K - 1):
        off = CARRY - (CONV_K - 1) + k
        acc = acc + w_ref[k:k + 1, :] * buf_ref[:, off:off + seq_len, :]
    tail = buf_ref[:, seq_len:seq_len + CARRY, :]
    buf_ref[:, 0:CARRY, :] = tail
    return _silu(acc).reshape(nseq * seq_len, ch), tail


def _pair_cols(tile):
    lane = lax.broadcasted_iota(jnp.int32, tile.shape, 1)
    swapped = pltpu.roll(tile, SSM_P, axis=1)
    return jnp.where(lane < SSM_P, tile, swapped), jnp.where(lane < SSM_P, swapped, tile)


def _ssd_kernel(xp_ref, bcp_ref, z_ref, dtr_ref, dtrt_ref, h0_ref, cx0_ref, cbc0_ref,
                cwx_ref, cwbc_ref, cbx_ref, cbbc_ref, dtb_ref, dtbt_ref, alogt_ref, aloge_ref,
                dskip_ref, normg_ref, e64_ref,
                y_ref, hout_ref, ctx_ref, ctbc_ref,
                xbuf, bcbuf, h_ref, *, nseq, seq_len):
    r = nseq * seq_len
    rp = max(r, LANES)
    zi = pl.program_id(1)
    first = zi == 0

    @pl.when(first)
    def _():
        h_ref[...] = h0_ref[...]

    xs, tail_x = _conv_silu(xbuf, cx0_ref, xp_ref[...].astype(F32), cwx_ref, cbx_ref, nseq, seq_len, first)
    bc, tail_bc = _conv_silu(bcbuf, cbc0_ref, bcp_ref[...].astype(F32), cwbc_ref, cbbc_ref, nseq, seq_len, first)
    ctx_ref[...] = tail_x
    ctbc_ref[...] = tail_bc

    causal, anti = _seq_masks(r, seq_len)
    dt = _softplus(dtr_ref[...] + dtb_ref[...])
    dt_e = _dot3_lhs(dt, e64_ref[...])
    loga_e = dt_e * (-jnp.exp(aloge_ref[...]))
    cum_e = _dot3_rhs(causal.astype(BF16), loga_e)
    loga_t = _softplus(dtrt_ref[0] + dtbt_ref[...]) * (-jnp.exp(alogt_ref[...]))
    cum_t = _dot3_lhs(loga_t, anti.astype(BF16))
    tot_e = _seq_totals(cum_e, nseq, seq_len)
    xdt = xs * dt_e
    in_scale = jnp.exp(cum_e)
    xw = xdt * jnp.exp(tot_e - cum_e)
    xw_t = _pad_rows(xw, rp).T.astype(BF16)
    lane = lax.broadcasted_iota(jnp.int32, (r, LANES), 1)

    y_groups = []
    for g in range(SSM_G):
        bg = bc[:, g * SSM_N:(g + 1) * SSM_N]
        cg = bc[:, BCW + g * SSM_N:BCW + (g + 1) * SSM_N]
        cb = lax.dot_general(cg, bg, (((1,), (1,)), ((), ())), preferred_element_type=F32)
        rows = slice(g * GW, (g + 1) * GW)
        pieces = []
        for j in range(GW // LANES):
            col0 = g * GW + j * LANES
            xpair = xdt[:, col0:col0 + LANES]
            ms, rhs = [], []
            for half, colb in enumerate(_pair_cols(cum_e[:, col0:col0 + LANES])):
                head = col0 // SSM_P + half
                seg = colb[:, :r] - cum_t[head:head + 1, :]
                ms.append((cb * jnp.exp(jnp.where(causal, seg, NEG))).astype(BF16))
                rhs.append(jnp.where(lane >= SSM_P if half else lane < SSM_P, xpair, 0.0).astype(BF16))
            if r % LANES == 0:
                pieces.append(jnp.dot(jnp.concatenate(ms, axis=1), jnp.concatenate(rhs, axis=0),
                                      preferred_element_type=F32))
            else:
                pieces.append(jnp.dot(ms[0], rhs[0], preferred_element_type=F32)
                              + jnp.dot(ms[1], rhs[1], preferred_element_type=F32))
        y_intra = jnp.concatenate(pieces, axis=1)
        y_inter = []
        for b in range(nseq):
            rb = slice(b * seq_len, (b + 1) * seq_len)
            hg = h_ref[b, rows, :]
            y_inter.append(lax.dot_general(cg[rb, :], hg, (((1,), (1,)), ((), ())),
                                           preferred_element_type=F32))
            bmask = bg if nseq == 1 else _row_mask(bg, b, seq_len)
            st = jnp.dot(xw_t[rows, :], _pad_rows(bmask, rp).astype(BF16), preferred_element_type=F32)
            dec8 = jnp.exp(tot_e[b * seq_len:b * seq_len + 8, rows])
            for j in range(GW // LANES):
                for half, dcol in enumerate(_pair_cols(dec8[:, j * LANES:(j + 1) * LANES])):
                    h8 = 2 * j + half
                    hr = slice(g * GW + h8 * SSM_P, g * GW + (h8 + 1) * SSM_P)
                    h_ref[b, hr, :] = dcol[0:1, :] * h_ref[b, hr, :] + st[h8 * SSM_P:(h8 + 1) * SSM_P, :]
        y_inter = y_inter[0] if nseq == 1 else jnp.concatenate(y_inter, axis=0)
        y_groups.append(y_intra + y_inter * in_scale[:, rows])
    y = jnp.concatenate(y_groups, axis=1) + dskip_ref[...] * xs
    yz = y * _silu(z_ref[...].astype(F32))
    outs = []
    for g in range(SSM_G):
        blk = yz[:, g * GW:(g + 1) * GW]
        ms = jnp.sum(blk * blk, axis=1, keepdims=True) * (1.0 / GW)
        outs.append(blk * lax.rsqrt(ms + EPS))
    y_ref[...] = (jnp.concatenate(outs, axis=1) * normg_ref[...]).astype(y_ref.dtype)

    @pl.when(zi == pl.num_programs(1) - 1)
    def _():
        hout_ref[...] = h_ref[...]


def _ssd(proj, dt_raw, row0, h0, conv0, p, nb, seq):
    seq_len = SSD_CHUNK if seq % SSD_CHUNK == 0 else seq
    nseq = 1 if seq_len == SSD_CHUNK else max(1, min(nb, STEP_ROWS // seq_len))
    nz = seq // seq_len
    r = nseq * seq_len
    t = nb * seq
    r0 = row0 // r
    nblk = t // r
    dtrt = dt_raw[row0:row0 + t, :SSM_HEADS].reshape(nblk, r, SSM_HEADS).transpose(0, 2, 1)
    pad_rows = CARRY - (CONV_K - 1)
    cx0 = jnp.pad(conv0[:, :, :D], ((0, 0), (pad_rows, 0), (0, 0)))
    cbc0 = jnp.pad(conv0[:, :, D:], ((0, 0), (pad_rows, 0), (0, 0)))
    rowblk = lambda col: pl.BlockSpec((r, D), lambda b, z: (r0 + b * nz + z, col))
    const = lambda shape: pl.BlockSpec(shape, lambda b, z: tuple(0 for _ in shape))
    per_b = lambda shape: pl.BlockSpec((nseq,) + shape, lambda b, z: (b, 0, 0))
    y, hout, ctx, ctbc = pl.pallas_call(
        functools.partial(_ssd_kernel, nseq=nseq, seq_len=seq_len),
        grid=(nb // nseq, nz),
        in_specs=[rowblk(1),
                  pl.BlockSpec((r, 2 * BCW), lambda b, z: (r0 + b * nz + z, 4)),
                  rowblk(0),
                  pl.BlockSpec((r, LANES), lambda b, z: (r0 + b * nz + z, 0)),
                  pl.BlockSpec((1, SSM_HEADS, r), lambda b, z: (b * nz + z, 0, 0)),
                  per_b((D, SSM_N)), per_b((CARRY, D)), per_b((CARRY, 2 * BCW)),
                  const((CONV_K, D)), const((CONV_K, 2 * BCW)), const((1, D)), const((1, 2 * BCW)),
                  const((1, LANES)), const((SSM_HEADS, r)), const((SSM_HEADS, r)), const((1, D)),
                  const((1, D)), const((1, D)), const((LANES, D))],
        out_specs=[pl.BlockSpec((r, D), lambda b, z: (b * nz + z, 0)),
                   per_b((D, SSM_N)), per_b((CARRY, D)), per_b((CARRY, 2 * BCW))],
        out_shape=[jax.ShapeDtypeStruct((t, D), BF16),
                   jax.ShapeDtypeStruct((nb, D, SSM_N), F32),
                   jax.ShapeDtypeStruct((nb, CARRY, D), F32),
                   jax.ShapeDtypeStruct((nb, CARRY, 2 * BCW), F32)],
        scratch_shapes=[pltpu.VMEM((nseq, seq_len + CARRY, D), F32),
                        pltpu.VMEM((nseq, seq_len + CARRY, 2 * BCW), F32),
                        pltpu.VMEM((nseq, D, SSM_N), F32)],
        compiler_params=pltpu.CompilerParams(
            dimension_semantics=("parallel", "arbitrary"), vmem_limit_bytes=VMEM_LIMIT),
        name="ssd",
    )(proj, proj, proj, dt_raw, dtrt, h0, cx0, cbc0,
      p["cwx"], p["cwbc"], p["cbx"], p["cbbc"], p["dtb"],
      jnp.broadcast_to(p["dtb"][0, :SSM_HEADS, None], (SSM_HEADS, r)),
      jnp.broadcast_to(p["alog"][:, None], (SSM_HEADS, r)),
      p["alog_e"], p["dskip_e"], p["ssm_norm_g"], p["e64"])
    conv_new = jnp.concatenate([ctx[:, pad_rows:], ctbc[:, pad_rows:]], axis=-1)
    return y, hout, conv_new


def _gla_kernel(q_ref, f_ref, i_ref, g_ref, s0_ref, lb_ref, normg_ref, y_ref, sout_ref, s_ref,
                *, nseq, seq_len):
    r = nseq * seq_len
    rp = max(r, LANES)
    zi = pl.program_id(1)

    @pl.when(zi == 0)
    def _():
        s_ref[...] = s0_ref[...]

    lb = lb_ref[...]
    f = lb + (1.0 - lb) * _sigmoid(f_ref[...].astype(F32))
    k = 1.0 - f
    q = _silu(q_ref[...].astype(F32))
    vb = i_ref[...]
    causal, _ = _seq_masks(r, seq_len)
    bc = _dot3_rhs(causal.astype(BF16), jnp.log(f))
    tot = _seq_totals(bc, nseq, seq_len)
    qe = q * jnp.exp(bc)
    ke = k * jnp.exp(-bc)
    kd_t = _pad_rows(ke * jnp.exp(tot), rp).T.astype(BF16)
    gate = _silu(g_ref[...].astype(F32))
    outs = []
    for h in range(HG_HEADS):
        sl = slice(h * HG_DK, (h + 1) * HG_DK)
        qh = qe[:, sl]
        attn = lax.dot_general(qh, ke[:, sl], (((1,), (1,)), ((), ())), preferred_element_type=F32)
        attn = jnp.where(causal, attn, 0.0)
        o = jnp.dot(attn.astype(BF16), vb[:, sl], preferred_element_type=F32)
        o_inter = []
        for b in range(nseq):
            rb = slice(b * seq_len, (b + 1) * seq_len)
            sh = s_ref[b, sl, :]
            o_inter.append(jnp.dot(qh[rb, :], sh, preferred_element_type=F32))
            vmask = vb[:, sl] if nseq == 1 else _row_mask(vb[:, sl], b, seq_len)
            st = jnp.dot(kd_t[sl, :], _pad_rows(vmask, rp), preferred_element_type=F32)
            dec = jnp.exp(tot[b * seq_len:b * seq_len + 1, sl])
            dec_col = jnp.broadcast_to(dec, (HG_DK, HG_DK)).T
            s_ref[b, sl, :] = dec_col * sh + st
        o = o + (o_inter[0] if nseq == 1 else jnp.concatenate(o_inter, axis=0))
        ms = jnp.sum(o * o, axis=1, keepdims=True) * (1.0 / HG_DK)
        outs.append(o * lax.rsqrt(ms + EPS))
    y_ref[...] = (jnp.concatenate(outs, axis=1) * normg_ref[...] * gate).astype(y_ref.dtype)

    @pl.when(zi == pl.num_programs(1) - 1)
    def _():
        sout_ref[...] = s_ref[...]


def _gla(proj, row0, s0, p, nb, seq):
    seq_len = GLA_CHUNK if seq % GLA_CHUNK == 0 else seq
    nseq = 1 if seq_len == GLA_CHUNK else max(1, min(nb, STEP_ROWS // seq_len))
    nz = seq // seq_len
    r = nseq * seq_len
    r0 = row0 // r
    rowblk = lambda col: pl.BlockSpec((r, D), lambda b, z: (r0 + b * nz + z, col))
    const = lambda shape: pl.BlockSpec(shape, lambda b, z: tuple(0 for _ in shape))
    per_b = lambda shape: pl.BlockSpec((nseq,) + shape, lambda b, z: (b, 0, 0))
    return pl.pallas_call(
        functools.partial(_gla_kernel, nseq=nseq, seq_len=seq_len),
        grid=(nb // nseq, nz),
        in_specs=[rowblk(0), rowblk(1), rowblk(2), rowblk(3), per_b((D, HG_DK)), const((1, D)), const((1, D))],
        out_specs=[pl.BlockSpec((r, D), lambda b, z: (b * nz + z, 0)), per_b((D, HG_DK))],
        out_shape=[jax.ShapeDtypeStruct((nb * seq, D), BF16), jax.ShapeDtypeStruct((nb, D, HG_DK), F32)],
        scratch_shapes=[pltpu.VMEM((nseq, D, HG_DK), F32)],
        compiler_params=pltpu.CompilerParams(
            dimension_semantics=("parallel", "arbitrary"), vmem_limit_bytes=VMEM_LIMIT),
        name="gla",
    )(proj, proj, proj, proj, s0, p["lb"], p["hgrn_norm_g"])


def _layer_norm(x, g, b):
    mu = jnp.mean(x, axis=1, keepdims=True)
    xc = x - mu
    var = jnp.mean(xc * xc, axis=1, keepdims=True)
    return xc * lax.rsqrt(var + EPS) * g + b


def _gate_kernel(yap_ref, yas_ref, ybp_ref, ybs_ref, ga_ref, gb_ref, wa_ref, wb_ref, o_ref, *, n_prompt):
    is_prompt = pl.program_id(0) < n_prompt
    ya = jnp.where(is_prompt, yap_ref[...], yas_ref[...])
    yb = jnp.where(is_prompt, ybp_ref[...], ybs_ref[...])
    a = jnp.dot(ya, wa_ref[...], preferred_element_type=F32)
    b = jnp.dot(yb, wb_ref[...], preferred_element_type=F32)
    ga = _sigmoid(ga_ref[...].astype(F32))
    gb = _sigmoid(gb_ref[...].astype(F32))
    o_ref[...] = (ga * a + gb * b).astype(o_ref.dtype)


def _gate(ya_p, ya_s, yb_p, yb_s, proj, p):
    tm, tn = MERGE_TM, MERGE_TN
    n_p = ya_p.shape[0] // tm
    t = ya_p.shape[0] + ya_s.shape[0]
    prow = pl.BlockSpec((tm, D), lambda i, j: (jnp.minimum(i, n_p - 1), 0))
    srow = pl.BlockSpec((tm, D), lambda i, j: (jnp.maximum(i - n_p, 0), 0))
    gcol = lambda seg: pl.BlockSpec((tm, tn), lambda i, j: (i, seg * (D // tn) + j))
    wcol = pl.BlockSpec((D, tn), lambda i, j: (0, j))
    return pl.pallas_call(
        functools.partial(_gate_kernel, n_prompt=n_p),
        grid=(t // tm, D // tn),
        in_specs=[prow, srow, prow, srow, gcol(4), gcol(5), wcol, wcol],
        out_specs=pl.BlockSpec((tm, tn), lambda i, j: (i, j)),
        out_shape=jax.ShapeDtypeStruct((t, D), BF16),
        compiler_params=pltpu.CompilerParams(
            dimension_semantics=("parallel", "arbitrary"), vmem_limit_bytes=VMEM_LIMIT),
        name="gate",
    )(ya_p, ya_s, yb_p, yb_s, proj, proj, p["w_a"], p["w_b"])


def _merge_kernel(m_ref, xp_ref, xs_ref, wo_ref, g1_ref, b1_ref, wrh_ref, wrl_ref, br_ref,
                  x1_ref, route_ref, *, n_prompt):
    x = jnp.where(pl.program_id(0) < n_prompt, xp_ref[...], xs_ref[...])
    mix = jnp.dot(m_ref[...], wo_ref[...], preferred_element_type=F32)
    x1 = _layer_norm(ALPHA * x + mix, g1_ref[...], b1_ref[...])
    _to_token_tiles(x1_ref, x1)
    hi = x1.astype(BF16)
    lo = (x1 - hi.astype(F32)).astype(BF16)
    logits = (jnp.dot(hi, wrh_ref[...], preferred_element_type=F32)
              + jnp.dot(lo, wrh_ref[...], preferred_element_type=F32)
              + jnp.dot(hi, wrl_ref[...], preferred_element_type=F32) + br_ref[...])
    lane = lax.broadcasted_iota(jnp.int32, logits.shape, 1).astype(F32)
    gl = jnp.where(lane < N_GROUPS, logits, NEG)
    gmax = jnp.max(gl, axis=1, keepdims=True)
    gsel = jnp.min(jnp.where(gl == gmax, lane, 1e9), axis=1, keepdims=True)
    p_grp = 1.0 / jnp.sum(jnp.exp(gl - gmax), axis=1, keepdims=True)
    lo = N_GROUPS + gsel * EPG
    el = jnp.where((lane >= lo) & (lane < lo + EPG), logits, NEG)
    v0 = jnp.max(el, axis=1, keepdims=True)
    i0 = jnp.min(jnp.where(el == v0, lane, 1e9), axis=1, keepdims=True)
    el2 = jnp.where(lane == i0, NEG, el)
    v1 = jnp.max(el2, axis=1, keepdims=True)
    i1 = jnp.min(jnp.where(el2 == v1, lane, 1e9), axis=1, keepdims=True)
    e1 = jnp.exp(v1 - v0)
    p0 = 1.0 / (1.0 + e1)
    p1 = e1 * p0
    route = jnp.where(lane == 0, i0 - N_GROUPS,
                      jnp.where(lane == 1, i1 - N_GROUPS,
                                jnp.where(lane == 2, p_grp * p0,
                                          jnp.where(lane == 3, p_grp * p1, 0.0))))
    route_ref[...] = route


def _merge(merged, x_p, x_s, p):
    tm = MERGE_TM
    n_p = x_p.shape[0] // tm
    t = x_p.shape[0] + x_s.shape[0]
    row = pl.BlockSpec((tm, D), lambda i: (i, 0))
    prow = pl.BlockSpec((tm, D), lambda i: (jnp.minimum(i, n_p - 1), 0))
    srow = pl.BlockSpec((tm, D), lambda i: (jnp.maximum(i - n_p, 0), 0))
    res = lambda shape: pl.BlockSpec(shape, lambda i: (0, 0), pipeline_mode=pl.Buffered(1))
    wr_hi = p["wr"].astype(BF16)
    wr_lo = (p["wr"] - wr_hi.astype(F32)).astype(BF16)
    return pl.pallas_call(
        functools.partial(_merge_kernel, n_prompt=n_p),
        grid=(t // tm,),
        in_specs=[row, prow, srow, res((D, D)), res((1, D)), res((1, D)),
                  res((D, LANES)), res((D, LANES)), res((1, LANES))],
        out_specs=[pl.BlockSpec((tm * ROW_TILES, LANES), lambda i: (i, 0)),
                   pl.BlockSpec((tm, LANES), lambda i: (i, 0))],
        out_shape=[jax.ShapeDtypeStruct((t * ROW_TILES, LANES), F32), jax.ShapeDtypeStruct((t, LANES), F32)],
        compiler_params=pltpu.CompilerParams(
            dimension_semantics=("parallel",), vmem_limit_bytes=VMEM_LIMIT),
        name="merge",
    )(merged, x_p, x_s, p["w_out"], p["ln1_g"], p["ln1_b"], wr_hi, wr_lo, p["br"])


def _row_gather(src_hbm, dst, sem, idx_ref, base, n, unroll=8):
    def body(j, carry):
        r = idx_ref[base + j]
        pltpu.make_async_copy(src_hbm.at[pl.ds(pl.multiple_of(r * ROW_TILES, ROW_TILES), ROW_TILES), :],
                              dst.at[pl.ds(j * ROW_TILES, ROW_TILES), :], sem).start()
        return carry
    lax.fori_loop(0, n, body, 0, unroll=unroll)


def _expert_kernel(tstart_ref, ntile_ref, nv_ref, src_ref, x1_hbm, w1_ref, w3_ref, w2_ref, es_hbm,
                   xbuf, obuf, gsem, osem, w1b, w3b, w2b, *, tm, n_tiles):
    e = pl.program_id(0)
    n_e = ntile_ref[e]
    g0 = tstart_ref[e]
    nv = nv_ref[0]
    rows = tm * ROW_TILES

    def gather_wait(s):
        pltpu.make_async_copy(x1_hbm.at[pl.ds(0, rows), :], xbuf.at[s], gsem.at[s]).wait()

    def out_copy(g):
        return pltpu.make_async_copy(obuf, es_hbm.at[pl.ds(pl.multiple_of(g * rows, rows), rows), :], osem)

    @pl.when(e == 0)
    def _():
        _row_gather(x1_hbm, xbuf.at[0], gsem.at[0], src_ref, 0, tm)

    @pl.when(n_e > 0)
    def _():
        w1b[...] = w1_ref[0].astype(BF16)
        w3b[...] = w3_ref[0].astype(BF16)
        w2b[...] = w2_ref[0].astype(BF16)

    def tile(k, carry):
        g = g0 + k
        slot = lax.rem(g, 2)
        _row_gather(x1_hbm, xbuf.at[1 - slot], gsem.at[1 - slot], src_ref, (g + 1) * tm, tm, unroll=True)
        gather_wait(slot)
        xb = _from_token_tiles(xbuf.at[slot]).astype(BF16)
        h1 = jnp.dot(xb, w1b[...], preferred_element_type=F32)
        h3 = jnp.dot(xb, w3b[...], preferred_element_type=F32)
        h = (_silu(h1) * h3).astype(BF16)
        out = jnp.dot(h, w2b[...], preferred_element_type=F32)

        @pl.when(g > 0)
        def _():
            out_copy(g).wait()
        _to_token_tiles(obuf, out)
        out_copy(g).start()
        return carry

    lax.fori_loop(0, n_e, tile, 0)

    @pl.when(e == pl.num_programs(0) - 1)
    def _():
        gather_wait(lax.rem(nv, 2))
        out_copy(0).wait()
        obuf[...] = jnp.zeros_like(obuf)

        def zero_tile(g, carry):
            out_copy(g).start()
            out_copy(g).wait()
            return carry

        lax.fori_loop(nv, n_tiles, zero_tile, 0)


def _experts(x1, tstart, ntile, nv, src, w1, w3, w2, n_tiles, tm):
    wspec = lambda shape: pl.BlockSpec((1,) + shape, lambda e, ts, nt, nv, src: (e, 0, 0))
    return pl.pallas_call(
        functools.partial(_expert_kernel, tm=tm, n_tiles=n_tiles),
        grid_spec=pltpu.PrefetchScalarGridSpec(
            num_scalar_prefetch=4,
            grid=(N_EXPERTS,),
            in_specs=[pl.BlockSpec(memory_space=pl.ANY),
                      wspec((D, D_EXPERT)), wspec((D, D_EXPERT)), wspec((D_EXPERT, D))],
            out_specs=pl.BlockSpec(memory_space=pl.ANY),
            scratch_shapes=[pltpu.VMEM((2, tm * ROW_TILES, LANES), F32), pltpu.VMEM((tm * ROW_TILES, LANES), F32),
                            pltpu.SemaphoreType.DMA((2,)), pltpu.SemaphoreType.DMA(()),
                            pltpu.VMEM((D, D_EXPERT), BF16), pltpu.VMEM((D, D_EXPERT), BF16),
                            pltpu.VMEM((D_EXPERT, D), BF16)]),
        out_shape=jax.ShapeDtypeStruct((n_tiles * tm * ROW_TILES, LANES), F32),
        compiler_params=pltpu.CompilerParams(
            dimension_semantics=("arbitrary",), vmem_limit_bytes=VMEM_LIMIT, disable_bounds_checks=True),
        name="experts",
    )(tstart, ntile, nv, src, x1, w1, w3, w2)


def _combine_kernel(dest_ref, es_hbm, x1_ref, route_ref, g2_ref, b2_ref, op_ref, os_ref, gbuf, sems,
                    *, tm, n_prompt):
    i = pl.program_id(0)
    n = pl.num_programs(0)
    slot = lax.rem(i, 2)

    def start(step, s):
        for k in range(2):
            _row_gather(es_hbm, gbuf.at[s, k], sems.at[s], dest_ref, (k * n + step) * tm, tm)

    @pl.when(i == 0)
    def _():
        start(0, 0)

    @pl.when(i + 1 < n)
    def _():
        start(i + 1, 1 - slot)

    for k in range(2):
        pltpu.make_async_copy(es_hbm.at[pl.ds(0, tm * ROW_TILES), :], gbuf.at[slot, k], sems.at[slot]).wait()
    route = route_ref[...]
    w0 = route[:, 2:3]
    w1 = route[:, 3:4]
    moe = w0 * _from_token_tiles(gbuf.at[slot, 0]) + w1 * _from_token_tiles(gbuf.at[slot, 1])
    out = _layer_norm(ALPHA * _from_token_tiles(x1_ref) + moe, g2_ref[...], b2_ref[...])

    @pl.when(i < n_prompt)
    def _():
        op_ref[...] = out

    @pl.when(i >= n_prompt)
    def _():
        os_ref[...] = out


def _combine(es, dest_km, x1, route, p, tp):
    t = x1.shape[0] // ROW_TILES
    tm = ROW_TM
    n_p = tp // tm
    return pl.pallas_call(
        functools.partial(_combine_kernel, tm=tm, n_prompt=n_p),
        grid_spec=pltpu.PrefetchScalarGridSpec(
            num_scalar_prefetch=1,
            grid=(t // tm,),
            in_specs=[pl.BlockSpec(memory_space=pl.ANY),
                      pl.BlockSpec((tm * ROW_TILES, LANES), lambda i, d: (i, 0)),
                      pl.BlockSpec((tm, LANES), lambda i, d: (i, 0)),
                      pl.BlockSpec((1, D), lambda i, d: (0, 0)),
                      pl.BlockSpec((1, D), lambda i, d: (0, 0))],
            out_specs=[pl.BlockSpec((tm, D), lambda i, d: (jnp.minimum(i, n_p - 1), 0)),
                       pl.BlockSpec((tm, D), lambda i, d: (jnp.maximum(i - n_p, 0), 0))],
            scratch_shapes=[pltpu.VMEM((2, 2, tm * ROW_TILES, LANES), F32), pltpu.SemaphoreType.DMA((2,))]),
        out_shape=[jax.ShapeDtypeStruct((tp, D), F32), jax.ShapeDtypeStruct((t - tp, D), F32)],
        compiler_params=pltpu.CompilerParams(
            dimension_semantics=("arbitrary",), vmem_limit_bytes=VMEM_LIMIT, disable_bounds_checks=True),
        name="combine",
    )(dest_km, es, x1, route, p["ln2_g"], p["ln2_b"])


def _route_meta(e_flat, tm, n_tiles):
    a = e_flat.shape[0]
    ids = jnp.arange(N_EXPERTS, dtype=jnp.int32)
    onehot = (e_flat[:, None] == ids[None, :]).astype(jnp.int32)
    csum = jnp.cumsum(onehot, axis=0)
    rank = jnp.sum((csum - onehot) * onehot, axis=1)
    counts = csum[-1]
    tiles_per = (counts + tm - 1) // tm
    tile_end = jnp.cumsum(tiles_per)
    tile_start = tile_end - tiles_per
    nv = tile_end[-1]
    dest = tile_start[e_flat] * tm + rank
    src = jnp.zeros((n_tiles * tm,), jnp.int32).at[dest].set(jnp.arange(a, dtype=jnp.int32) // 2)
    return (tile_start.astype(jnp.int32), tiles_per.astype(jnp.int32), nv.reshape(1).astype(jnp.int32), src,
            dest.astype(jnp.int32))


def kernel(x_prompt, x_sample, state_ssm, state_conv, state_hgrn, w_in, conv_w, conv_b, dt_bias, a_log, d_skip, ssm_norm_g, hgrn_lb_logits, hgrn_norm_g, w_a, w_b, w_out, ln1_g, ln1_b, router_g_w, router_g_b, router_e_w, router_e_b, exp_w1, exp_w3, exp_w2, ln2_g, ln2_b):
    nbp, seqp, _ = x_prompt.shape
    nbs, seqs, _ = x_sample.shape
    tp, ts = nbp * seqp, nbs * seqs
    t = tp + ts
    l = 0
    x_p, x_s = x_prompt.reshape(tp, D), x_sample.reshape(ts, D)

    wt = jnp.transpose(w_in[l])
    o_dt = D + D + 2 * BCW
    o_q = o_dt + SSM_HEADS
    tm = PROJ_TM if tp % PROJ_TM == 0 and ts % PROJ_TM == 0 else MERGE_TM
    proj_a = _project(x_p, x_s, wt, 0, o_dt, tm, PROJ_TN, BF16)
    dt_raw = jnp.pad(_project(x_p, x_s, wt, o_dt, SSM_HEADS, tm, SSM_HEADS, F32),
                     ((0, 0), (0, LANES - SSM_HEADS)))
    proj_b = _project(x_p, x_s, wt, o_q, 6 * D, tm, PROJ_TN, BF16)

    lb_all = jnp.cumsum(jax.nn.softmax(hgrn_lb_logits.astype(F32), axis=0), axis=0)
    head_of = np.arange(D) // SSM_P
    pad128 = lambda v: jnp.pad(v, (0, LANES - v.shape[0])).reshape(1, LANES)
    p = {
        "cwx": conv_w[l][:, :D], "cwbc": conv_w[l][:, D:],
        "cbx": conv_b[l][:D].reshape(1, D), "cbbc": conv_b[l][D:].reshape(1, 2 * BCW),
        "dtb": pad128(dt_bias[l]), "alog": a_log[l], "alog_e": jnp.repeat(a_log[l], SSM_P).reshape(1, D),
        "dskip_e": jnp.repeat(d_skip[l], SSM_P).reshape(1, D),
        "ssm_norm_g": ssm_norm_g[l].reshape(1, D),
        "e64": jnp.asarray(np.arange(LANES)[:, None] == head_of[None, :], dtype=BF16),
        "lb": lb_all[l].reshape(1, D), "hgrn_norm_g": hgrn_norm_g[l].reshape(1, D),
        "w_a": w_a[l].astype(BF16), "w_b": w_b[l].astype(BF16), "w_out": w_out[l].astype(BF16),
        "ln1_g": ln1_g[l].reshape(1, D), "ln1_b": ln1_b[l].reshape(1, D),
        "ln2_g": ln2_g[l].reshape(1, D), "ln2_b": ln2_b[l].reshape(1, D),
        "wr": jnp.pad(jnp.concatenate([router_g_w[l], router_e_w[l]], axis=1),
                      ((0, 0), (0, LANES - N_GROUPS - N_EXPERTS))),
        "br": pad128(jnp.concatenate([router_g_b[l], router_e_b[l]])),
    }

    ya_p, ssm_p, conv_p = _ssd(proj_a, dt_raw, 0, jnp.zeros((nbp, D, SSM_N), F32),
                               jnp.zeros((nbp, CONV_K - 1, D + 2 * BCW), F32), p, nbp, seqp)
    ya_s, ssm_s, conv_s = _ssd(proj_a, dt_raw, tp, state_ssm[l].reshape(nbs, D, SSM_N), state_conv[l],
                               p, nbs, seqs)
    yb_p, hg_p = _gla(proj_b, 0, jnp.zeros((nbp, D, HG_DK), F32), p, nbp, seqp)
    yb_s, hg_s = _gla(proj_b, tp, state_hgrn[l].reshape(nbs, D, HG_DK), p, nbs, seqs)

    merged = _gate(ya_p, ya_s, yb_p, yb_s, proj_b, p)
    x1, route = _merge(merged, x_p, x_s, p)

    e_flat = route[:, :2].astype(jnp.int32).reshape(-1)
    n_tiles = (2 * t) // MOE_TM + N_EXPERTS + 1
    tstart, ntile, nv, src, dest = _route_meta(e_flat, MOE_TM, n_tiles)
    es = _experts(x1, tstart, ntile, nv, src, exp_w1[l], exp_w3[l], exp_w2[l], n_tiles, MOE_TM)
    dest_km = dest.reshape(t // ROW_TM, ROW_TM, 2).transpose(2, 0, 1).reshape(-1)
    x2_p, x2_s = _combine(es, dest_km, x1, route, p, tp)

    y_prompt = x2_p.reshape(nbp, seqp, D)
    y_sample = x2_s.reshape(nbs, seqs, D)
    return (y_prompt, y_sample,
            ssm_p.reshape(1, nbp, SSM_HEADS, SSM_P, SSM_N), conv_p[None],
            hg_p.reshape(1, nbp, HG_HEADS, HG_DK, HG_DK),
            ssm_s.reshape(1, nbs, SSM_HEADS, SSM_P, SSM_N), conv_s[None],
            hg_s.reshape(1, nbs, HG_HEADS, HG_DK, HG_DK))
```

```python
import functools

import jax
import jax.numpy as jnp
import numpy as np
from jax import lax
from jax.experimental import pallas as pl
from jax.experimental.pallas import tpu as pltpu

F32 = jnp.float32
BF16 = jnp.bfloat16
HI = lax.Precision.HIGHEST

D = 2048
SSM_HEADS = 32
SSM_P = 64
SSM_N = 128
SSM_G = 4
GW = D // SSM_G
BCW = SSM_G * SSM_N
CONV_K = 4
HG_HEADS = 16
HG_DK = 128
N_GROUPS = 4
EPG = 8
N_EXPERTS = 32
D_EXPERT = 512
EPS = 1e-5
ALPHA = 2.0 ** 0.25
NEG = -1e30
LANES = 128
ROW_TILES = D // LANES

VMEM_LIMIT = 56 * 1024 * 1024
SSD_CHUNK = 128
GLA_CHUNK = 64
STEP_ROWS = 64
PROJ_TM = 2304
PROJ_TN = 512
MOE_TM = 128
ROW_TM = 256
MERGE_TM = 512
MERGE_TN = 512
CARRY = 8


def _sigmoid(x):
    return 0.5 * jnp.tanh(0.5 * x) + 0.5


def _silu(x):
    return x * _sigmoid(x)


def _softplus(x):
    return jnp.maximum(x, 0.0) + jnp.log(1.0 + jnp.exp(-jnp.abs(x)))


def _split3(x):
    hi = x.astype(BF16)
    r = x - hi.astype(F32)
    mid = r.astype(BF16)
    lo = (r - mid.astype(F32)).astype(BF16)
    return hi, mid, lo


def _dot3_rhs(m_bf16, x):
    return sum(jnp.dot(m_bf16, part, preferred_element_type=F32) for part in _split3(x))


def _dot3_lhs(x, m_bf16):
    return sum(jnp.dot(part, m_bf16, preferred_element_type=F32) for part in _split3(x))


def _seq_masks(r, seq_len):
    shift = seq_len.bit_length() - 1
    row = lax.broadcasted_iota(jnp.int32, (r, r), 0)
    col = lax.broadcasted_iota(jnp.int32, (r, r), 1)
    same = lax.shift_right_logical(row, shift) == lax.shift_right_logical(col, shift)
    return same & (row >= col), same & (row <= col)


def _seq_totals(cum, nseq, seq_len):
    w = cum.shape[1]
    parts = [jnp.broadcast_to(cum[(b + 1) * seq_len - 1:(b + 1) * seq_len, :], (seq_len, w)) for b in range(nseq)]
    return parts[0] if nseq == 1 else jnp.concatenate(parts, axis=0)


def _pad_rows(x, rows):
    if x.shape[0] == rows:
        return x
    return jnp.concatenate([x, jnp.zeros((rows - x.shape[0], x.shape[1]), x.dtype)], axis=0)


def _to_token_tiles(ref, val):
    rows = val.shape[0]
    for s in range(ROW_TILES):
        ref[pl.ds(s, rows, stride=ROW_TILES), :] = val[:, s * LANES:(s + 1) * LANES]


def _from_token_tiles(ref):
    rows = ref.shape[0] // ROW_TILES
    return jnp.concatenate([ref[pl.ds(s, rows, stride=ROW_TILES), :] for s in range(ROW_TILES)], axis=1)


def _row_mask(x, b, seq_len):
    row = lax.broadcasted_iota(jnp.int32, x.shape, 0)
    return jnp.where((row >= b * seq_len) & (row < (b + 1) * seq_len), x, jnp.zeros_like(x))


def _mm_nt_kernel(x_ref, wt_ref, o_ref):
    w = wt_ref[...].astype(BF16)
    o_ref[...] = lax.dot_general(x_ref[...], w, (((1,), (1,)), ((), ())),
                                 preferred_element_type=F32).astype(o_ref.dtype)


def _project(x, wt, row0, n, tm, tn, out_dtype):
    m, k = x.shape
    assert n % tn == 0 and m % tm == 0 and row0 % 8 == 0
    if row0 % tn == 0:
        w_spec = pl.BlockSpec((tn, k), lambda i, j: (row0 // tn + j, 0))
    else:
        w_spec = pl.BlockSpec((pl.Element(tn), pl.Element(k)),
                              lambda i, j: (pl.multiple_of(row0 + j * tn, 8), 0))
    return pl.pallas_call(
        _mm_nt_kernel,
        grid=(m // tm, n // tn),
        in_specs=[pl.BlockSpec((tm, k), lambda i, j: (i, 0)), w_spec],
        out_specs=pl.BlockSpec((tm, tn), lambda i, j: (i, j)),
        out_shape=jax.ShapeDtypeStruct((m, n), out_dtype),
        compiler_params=pltpu.CompilerParams(
            dimension_semantics=("parallel", "arbitrary"), vmem_limit_bytes=VMEM_LIMIT),
        name="proj",
    )(x, wt)


def _conv_silu(buf_ref, carry0_ref, u, w_ref, b_ref, nseq, seq_len, first):
    ch = u.shape[1]

    @pl.when(first)
    def _():
        buf_ref[:, 0:CARRY, :] = carry0_ref[...]
    u3 = u.reshape(nseq, seq_len, ch)
    buf_ref[:, CARRY:CARRY + seq_len, :] = u3
    acc = b_ref[...] + w_ref[CONV_K - 1:CONV_K, :] * u3
    for k in range(CONV_K - 1):
        off = CARRY - (CONV_K - 1) + k
        acc = acc + w_ref[k:k + 1, :] * buf_ref[:, off:off + seq_len, :]
    tail = buf_ref[:, seq_len:seq_len + CARRY, :]
    buf_ref[:, 0:CARRY, :] = tail
    return _silu(acc).reshape(nseq * seq_len, ch), tail


def _pair_cols(tile):
    lane = lax.broadcasted_iota(jnp.int32, tile.shape, 1)
    swapped = pltpu.roll(tile, SSM_P, axis=1)
    return jnp.where(lane < SSM_P, tile, swapped), jnp.where(lane < SSM_P, swapped, tile)


def _ssd_kernel(xp_ref, bcp_ref, z_ref, dtr_ref, dtrt_ref, h0_ref, cx0_ref, cbc0_ref,
                cwx_ref, cwbc_ref, cbx_ref, cbbc_ref, dtb_ref, dtbt_ref, alogt_ref, aloge_ref,
                dskip_ref, normg_ref, e64_ref,
                y_ref, hout_ref, ctx_ref, ctbc_ref,
                xbuf, bcbuf, h_ref, *, nseq, seq_len):
    r = nseq * seq_len
    rp = max(r, LANES)
    zi = pl.program_id(1)
    first = zi == 0

    @pl.when(first)
    def _():
        h_ref[...] = h0_ref[...]

    xs, tail_x = _conv_silu(xbuf, cx0_ref, xp_ref[...].astype(F32), cwx_ref, cbx_ref, nseq, seq_len, first)
    bc, tail_bc = _conv_silu(bcbuf, cbc0_ref, bcp_ref[...].astype(F32), cwbc_ref, cbbc_ref, nseq, seq_len, first)
    ctx_ref[...] = tail_x
    ctbc_ref[...] = tail_bc

    causal, anti = _seq_masks(r, seq_len)
    dt = _softplus(dtr_ref[...] + dtb_ref[...])
    dt_e = _dot3_lhs(dt, e64_ref[...])
    cum = _dot3_rhs(causal.astype(BF16), dt * (-jnp.exp(aloge_ref[...])))
    cum_e = _dot3_lhs(cum, e64_ref[...])
    loga_t = _softplus(dtrt_ref[0] + dtbt_ref[...]) * (-jnp.exp(alogt_ref[...]))
    cum_t = _dot3_lhs(loga_t, anti.astype(BF16))
    tot_e = _seq_totals(cum_e, nseq, seq_len)
    xdt = xs * dt_e
    in_scale = jnp.exp(cum_e)
    xw = xdt * jnp.exp(tot_e - cum_e)
    xw_t = _pad_rows(xw, rp).T.astype(BF16)
    lane = lax.broadcasted_iota(jnp.int32, (r, LANES), 1)

    y_groups = []
    for g in range(SSM_G):
        bg = bc[:, g * SSM_N:(g + 1) * SSM_N]
        cg = bc[:, BCW + g * SSM_N:BCW + (g + 1) * SSM_N]
        cb = lax.dot_general(cg, bg, (((1,), (1,)), ((), ())), preferred_element_type=F32)
        rows = slice(g * GW, (g + 1) * GW)
        pieces = []
        for j in range(GW // LANES):
            col0 = g * GW + j * LANES
            xpair = xdt[:, col0:col0 + LANES]
            ms, rhs = [], []
            for half, colb in enumerate(_pair_cols(cum_e[:, col0:col0 + LANES])):
                head = col0 // SSM_P + half
                seg = colb[:, :r] - cum_t[head:head + 1, :]
                ms.append((cb * jnp.exp(jnp.where(causal, seg, NEG))).astype(BF16))
                rhs.append(jnp.where(lane >= SSM_P if half else lane < SSM_P, xpair, 0.0).astype(BF16))
            if r % LANES == 0:
                pieces.append(jnp.dot(jnp.concatenate(ms, axis=1), jnp.concatenate(rhs, axis=0),
                                      preferred_element_type=F32))
            else:
                pieces.append(jnp.dot(ms[0], rhs[0], preferred_element_type=F32)
                              + jnp.dot(ms[1], rhs[1], preferred_element_type=F32))
        y_intra = jnp.concatenate(pieces, axis=1)
        y_inter = []
        for b in range(nseq):
            rb = slice(b * seq_len, (b + 1) * seq_len)
            hg = h_ref[b, rows, :]
            y_inter.append(lax.dot_general(cg[rb, :], hg, (((1,), (1,)), ((), ())),
                                           preferred_element_type=F32))
            bmask = bg if nseq == 1 else _row_mask(bg, b, seq_len)
            st = jnp.dot(xw_t[rows, :], _pad_rows(bmask, rp).astype(BF16), preferred_element_type=F32)
            dec8 = jnp.exp(tot_e[b * seq_len:b * seq_len + 8, rows])
            for j in range(GW // LANES):
                for half, dcol in enumerate(_pair_cols(dec8[:, j * LANES:(j + 1) * LANES])):
                    h8 = 2 * j + half
                    hr = slice(g * GW + h8 * SSM_P, g * GW + (h8 + 1) * SSM_P)
                    h_ref[b, hr, :] = dcol[0:1, :] * h_ref[b, hr, :] + st[h8 * SSM_P:(h8 + 1) * SSM_P, :]
        y_inter = y_inter[0] if nseq == 1 else jnp.concatenate(y_inter, axis=0)
        y_groups.append(y_intra + y_inter * in_scale[:, rows])
    y = jnp.concatenate(y_groups, axis=1) + dskip_ref[...] * xs
    yz = y * _silu(z_ref[...].astype(F32))
    outs = []
    for g in range(SSM_G):
        blk = yz[:, g * GW:(g + 1) * GW]
        ms = jnp.sum(blk * blk, axis=1, keepdims=True) * (1.0 / GW)
        outs.append(blk * lax.rsqrt(ms + EPS))
    y_ref[...] = (jnp.concatenate(outs, axis=1) * normg_ref[...]).astype(y_ref.dtype)

    @pl.when(zi == pl.num_programs(1) - 1)
    def _():
        hout_ref[...] = h_ref[...]


def _ssd(proj, dt_raw, row0, h0, conv0, p, nb, seq):
    seq_len = SSD_CHUNK if seq % SSD_CHUNK == 0 else seq
    nseq = 1 if seq_len == SSD_CHUNK else max(1, min(nb, STEP_ROWS // seq_len))
    nz = seq // seq_len
    r = nseq * seq_len
    t = nb * seq
    r0 = row0 // r
    nblk = t // r
    dtrt = dt_raw[row0:row0 + t, :SSM_HEADS].reshape(nblk, r, SSM_HEADS).transpose(0, 2, 1)
    pad_rows = CARRY - (CONV_K - 1)
    cx0 = jnp.pad(conv0[:, :, :D], ((0, 0), (pad_rows, 0), (0, 0)))
    cbc0 = jnp.pad(conv0[:, :, D:], ((0, 0), (pad_rows, 0), (0, 0)))
    rowblk = lambda col: pl.BlockSpec((r, D), lambda b, z: (r0 + b * nz + z, col))
    const = lambda shape: pl.BlockSpec(shape, lambda b, z: tuple(0 for _ in shape))
    per_b = lambda shape: pl.BlockSpec((nseq,) + shape, lambda b, z: (b, 0, 0))
    y, hout, ctx, ctbc = pl.pallas_call(
        functools.partial(_ssd_kernel, nseq=nseq, seq_len=seq_len),
        grid=(nb // nseq, nz),
        in_specs=[rowblk(1),
                  pl.BlockSpec((r, 2 * BCW), lambda b, z: (r0 + b * nz + z, 4)),
                  rowblk(0),
                  pl.BlockSpec((r, LANES), lambda b, z: (r0 + b * nz + z, 0)),
                  pl.BlockSpec((1, SSM_HEADS, r), lambda b, z: (b * nz + z, 0, 0)),
                  per_b((D, SSM_N)), per_b((CARRY, D)), per_b((CARRY, 2 * BCW)),
                  const((CONV_K, D)), const((CONV_K, 2 * BCW)), const((1, D)), const((1, 2 * BCW)),
                  const((1, LANES)), const((SSM_HEADS, r)), const((SSM_HEADS, r)), const((1, LANES)),
                  const((1, D)), const((1, D)), const((LANES, D))],
        out_specs=[pl.BlockSpec((r, D), lambda b, z: (b * nz + z, 0)),
                   per_b((D, SSM_N)), per_b((CARRY, D)), per_b((CARRY, 2 * BCW))],
        out_shape=[jax.ShapeDtypeStruct((t, D), BF16),
                   jax.ShapeDtypeStruct((nb, D, SSM_N), F32),
                   jax.ShapeDtypeStruct((nb, CARRY, D), F32),
                   jax.ShapeDtypeStruct((nb, CARRY, 2 * BCW), F32)],
        scratch_shapes=[pltpu.VMEM((nseq, seq_len + CARRY, D), F32),
                        pltpu.VMEM((nseq, seq_len + CARRY, 2 * BCW), F32),
                        pltpu.VMEM((nseq, D, SSM_N), F32)],
        compiler_params=pltpu.CompilerParams(
            dimension_semantics=("parallel", "arbitrary"), vmem_limit_bytes=VMEM_LIMIT),
        name="ssd",
    )(proj, proj, proj, dt_raw, dtrt, h0, cx0, cbc0,
      p["cwx"], p["cwbc"], p["cbx"], p["cbbc"], p["dtb"],
      jnp.broadcast_to(p["dtb"][0, :SSM_HEADS, None], (SSM_HEADS, r)),
      jnp.broadcast_to(p["alog"][:, None], (SSM_HEADS, r)),
      p["alog_e"], p["dskip_e"], p["ssm_norm_g"], p["e64"])
    conv_new = jnp.concatenate([ctx[:, pad_rows:], ctbc[:, pad_rows:]], axis=-1)
    return y, hout, conv_new


def _gla_kernel(q_ref, f_ref, i_ref, g_ref, s0_ref, lb_ref, normg_ref, y_ref, sout_ref, s_ref,
                *, nseq, seq_len):
    r = nseq * seq_len
    rp = max(r, LANES)
    zi = pl.program_id(1)

    @pl.when(zi == 0)
    def _():
        s_ref[...] = s0_ref[...]

    lb = lb_ref[...]
    f = lb + (1.0 - lb) * _sigmoid(f_ref[...].astype(F32))
    k = 1.0 - f
    q = _silu(q_ref[...].astype(F32))
    vb = i_ref[...]
    causal, _ = _seq_masks(r, seq_len)
    bc = _dot3_rhs(causal.astype(BF16), jnp.log(f))
    tot = _seq_totals(bc, nseq, seq_len)
    qe = q * jnp.exp(bc)
    ke = k * jnp.exp(-bc)
    kd_t = _pad_rows(ke * jnp.exp(tot), rp).T.astype(BF16)
    gate = _silu(g_ref[...].astype(F32))
    outs = []
    for h in range(HG_HEADS):
        sl = slice(h * HG_DK, (h + 1) * HG_DK)
        qh = qe[:, sl]
        attn = lax.dot_general(qh, ke[:, sl], (((1,), (1,)), ((), ())), preferred_element_type=F32)
        attn = jnp.where(causal, attn, 0.0)
        o = jnp.dot(attn.astype(BF16), vb[:, sl], preferred_element_type=F32)
        o_inter = []
        for b in range(nseq):
            rb = slice(b * seq_len, (b + 1) * seq_len)
            sh = s_ref[b, sl, :]
            o_inter.append(jnp.dot(qh[rb, :], sh, preferred_element_type=F32))
            vmask = vb[:, sl] if nseq == 1 else _row_mask(vb[:, sl], b, seq_len)
            st = jnp.dot(kd_t[sl, :], _pad_rows(vmask, rp), preferred_element_type=F32)
            dec = jnp.exp(tot[b * seq_len:b * seq_len + 1, sl])
            dec_col = jnp.broadcast_to(dec, (HG_DK, HG_DK)).T
            s_ref[b, sl, :] = dec_col * sh + st
        o = o + (o_inter[0] if nseq == 1 else jnp.concatenate(o_inter, axis=0))
        ms = jnp.sum(o * o, axis=1, keepdims=True) * (1.0 / HG_DK)
        outs.append(o * lax.rsqrt(ms + EPS))
    y_ref[...] = (jnp.concatenate(outs, axis=1) * normg_ref[...] * gate).astype(y_ref.dtype)

    @pl.when(zi == pl.num_programs(1) - 1)
    def _():
        sout_ref[...] = s_ref[...]


def _gla(proj, row0, s0, p, nb, seq):
    seq_len = GLA_CHUNK if seq % GLA_CHUNK == 0 else seq
    nseq = 1 if seq_len == GLA_CHUNK else max(1, min(nb, STEP_ROWS // seq_len))
    nz = seq // seq_len
    r = nseq * seq_len
    r0 = row0 // r
    rowblk = lambda col: pl.BlockSpec((r, D), lambda b, z: (r0 + b * nz + z, col))
    const = lambda shape: pl.BlockSpec(shape, lambda b, z: tuple(0 for _ in shape))
    per_b = lambda shape: pl.BlockSpec((nseq,) + shape, lambda b, z: (b, 0, 0))
    return pl.pallas_call(
        functools.partial(_gla_kernel, nseq=nseq, seq_len=seq_len),
        grid=(nb // nseq, nz),
        in_specs=[rowblk(0), rowblk(1), rowblk(2), rowblk(3), per_b((D, HG_DK)), const((1, D)), const((1, D))],
        out_specs=[pl.BlockSpec((r, D), lambda b, z: (b * nz + z, 0)), per_b((D, HG_DK))],
        out_shape=[jax.ShapeDtypeStruct((nb * seq, D), BF16), jax.ShapeDtypeStruct((nb, D, HG_DK), F32)],
        scratch_shapes=[pltpu.VMEM((nseq, D, HG_DK), F32)],
        compiler_params=pltpu.CompilerParams(
            dimension_semantics=("parallel", "arbitrary"), vmem_limit_bytes=VMEM_LIMIT),
        name="gla",
    )(proj, proj, proj, proj, s0, p["lb"], p["hgrn_norm_g"])


def _layer_norm(x, g, b):
    mu = jnp.mean(x, axis=1, keepdims=True)
    xc = x - mu
    var = jnp.mean(xc * xc, axis=1, keepdims=True)
    return xc * lax.rsqrt(var + EPS) * g + b


def _gate_kernel(yap_ref, yas_ref, ybp_ref, ybs_ref, ga_ref, gb_ref, wa_ref, wb_ref, o_ref, *, n_prompt):
    is_prompt = pl.program_id(0) < n_prompt
    ya = jnp.where(is_prompt, yap_ref[...], yas_ref[...])
    yb = jnp.where(is_prompt, ybp_ref[...], ybs_ref[...])
    a = jnp.dot(ya, wa_ref[...], preferred_element_type=F32)
    b = jnp.dot(yb, wb_ref[...], preferred_element_type=F32)
    ga = _sigmoid(ga_ref[...].astype(F32))
    gb = _sigmoid(gb_ref[...].astype(F32))
    o_ref[...] = (ga * a + gb * b).astype(o_ref.dtype)


def _gate(ya_p, ya_s, yb_p, yb_s, proj, p):
    tm, tn = MERGE_TM, MERGE_TN
    n_p = ya_p.shape[0] // tm
    t = ya_p.shape[0] + ya_s.shape[0]
    prow = pl.BlockSpec((tm, D), lambda i, j: (jnp.minimum(i, n_p - 1), 0))
    srow = pl.BlockSpec((tm, D), lambda i, j: (jnp.maximum(i - n_p, 0), 0))
    gcol = lambda seg: pl.BlockSpec((tm, tn), lambda i, j: (i, seg * (D // tn) + j))
    wcol = pl.BlockSpec((D, tn), lambda i, j: (0, j))
    return pl.pallas_call(
        functools.partial(_gate_kernel, n_prompt=n_p),
        grid=(t // tm, D // tn),
        in_specs=[prow, srow, prow, srow, gcol(4), gcol(5), wcol, wcol],
        out_specs=pl.BlockSpec((tm, tn), lambda i, j: (i, j)),
        out_shape=jax.ShapeDtypeStruct((t, D), BF16),
        compiler_params=pltpu.CompilerParams(
            dimension_semantics=("parallel", "arbitrary"), vmem_limit_bytes=VMEM_LIMIT),
        name="gate",
    )(ya_p, ya_s, yb_p, yb_s, proj, proj, p["w_a"], p["w_b"])


def _merge_kernel(m_ref, xp_ref, xs_ref, wo_ref, g1_ref, b1_ref, wrh_ref, wrl_ref, br_ref,
                  x1_ref, route_ref, *, n_prompt):
    x = jnp.where(pl.program_id(0) < n_prompt, xp_ref[...], xs_ref[...])
    mix = jnp.dot(m_ref[...], wo_ref[...], preferred_element_type=F32)
    x1 = _layer_norm(ALPHA * x + mix, g1_ref[...], b1_ref[...])
    _to_token_tiles(x1_ref, x1)
    hi = x1.astype(BF16)
    lo = (x1 - hi.astype(F32)).astype(BF16)
    logits = (jnp.dot(hi, wrh_ref[...], preferred_element_type=F32)
              + jnp.dot(lo, wrh_ref[...], preferred_element_type=F32)
              + jnp.dot(hi, wrl_ref[...], preferred_element_type=F32) + br_ref[...])
    lane = lax.broadcasted_iota(jnp.int32, logits.shape, 1).astype(F32)
    gl = jnp.where(lane < N_GROUPS, logits, NEG)
    gmax = jnp.max(gl, axis=1, keepdims=True)
    gsel = jnp.min(jnp.where(gl == gmax, lane, 1e9), axis=1, keepdims=True)
    p_grp = 1.0 / jnp.sum(jnp.exp(gl - gmax), axis=1, keepdims=True)
    lo = N_GROUPS + gsel * EPG
    el = jnp.where((lane >= lo) & (lane < lo + EPG), logits, NEG)
    v0 = jnp.max(el, axis=1, keepdims=True)
    i0 = jnp.min(jnp.where(el == v0, lane, 1e9), axis=1, keepdims=True)
    el2 = jnp.where(lane == i0, NEG, el)
    v1 = jnp.max(el2, axis=1, keepdims=True)
    i1 = jnp.min(jnp.where(el2 == v1, lane, 1e9), axis=1, keepdims=True)
    e1 = jnp.exp(v1 - v0)
    p0 = 1.0 / (1.0 + e1)
    p1 = e1 * p0
    route = jnp.where(lane == 0, i0 - N_GROUPS,
                      jnp.where(lane == 1, i1 - N_GROUPS,
                                jnp.where(lane == 2, p_grp * p0,
                                          jnp.where(lane == 3, p_grp * p1, 0.0))))
    route_ref[...] = route


def _merge(merged, x_p, x_s, p):
    tm = MERGE_TM
    n_p = x_p.shape[0] // tm
    t = x_p.shape[0] + x_s.shape[0]
    row = pl.BlockSpec((tm, D), lambda i: (i, 0))
    prow = pl.BlockSpec((tm, D), lambda i: (jnp.minimum(i, n_p - 1), 0))
    srow = pl.BlockSpec((tm, D), lambda i: (jnp.maximum(i - n_p, 0), 0))
    res = lambda shape: pl.BlockSpec(shape, lambda i: (0, 0), pipeline_mode=pl.Buffered(1))
    wr_hi = p["wr"].astype(BF16)
    wr_lo = (p["wr"] - wr_hi.astype(F32)).astype(BF16)
    return pl.pallas_call(
        functools.partial(_merge_kernel, n_prompt=n_p),
        grid=(t // tm,),
        in_specs=[row, prow, srow, res((D, D)), res((1, D)), res((1, D)),
                  res((D, LANES)), res((D, LANES)), res((1, LANES))],
        out_specs=[pl.BlockSpec((tm * ROW_TILES, LANES), lambda i: (i, 0)),
                   pl.BlockSpec((tm, LANES), lambda i: (i, 0))],
        out_shape=[jax.ShapeDtypeStruct((t * ROW_TILES, LANES), F32), jax.ShapeDtypeStruct((t, LANES), F32)],
        compiler_params=pltpu.CompilerParams(
            dimension_semantics=("parallel",), vmem_limit_bytes=VMEM_LIMIT),
        name="merge",
    )(merged, x_p, x_s, p["w_out"], p["ln1_g"], p["ln1_b"], wr_hi, wr_lo, p["br"])


def _row_gather(src_hbm, dst, sem, idx_ref, base, n, unroll=8):
    def body(j, carry):
        r = idx_ref[base + j]
        pltpu.make_async_copy(src_hbm.at[pl.ds(pl.multiple_of(r * ROW_TILES, ROW_TILES), ROW_TILES), :],
                              dst.at[pl.ds(j * ROW_TILES, ROW_TILES), :], sem).start()
        return carry
    lax.fori_loop(0, n, body, 0, unroll=unroll)


def _expert_kernel(tstart_ref, ntile_ref, nv_ref, src_ref, x1_hbm, w1_hbm, w3_hbm, w2_hbm, es_hbm,
                   xbuf, obuf, gsem, osem, w1f, w3f, w2f, wsem, w1b, w3b, w2b, *, tm, n_tiles):
    e = pl.program_id(0)
    n_e = ntile_ref[e]
    g0 = tstart_ref[e]
    nv = nv_ref[0]
    rows = tm * ROW_TILES

    def gather_wait(s):
        pltpu.make_async_copy(x1_hbm.at[pl.ds(0, rows), :], xbuf.at[s], gsem.at[s]).wait()

    def out_copy(g):
        return pltpu.make_async_copy(obuf, es_hbm.at[pl.ds(pl.multiple_of(g * rows, rows), rows), :], osem)

    def weight_copies(ex, s):
        return [pltpu.make_async_copy(w_hbm.at[ex], w_f.at[s], wsem.at[s])
                for w_hbm, w_f in ((w1_hbm, w1f), (w3_hbm, w3f), (w2_hbm, w2f))]

    wslot = lax.rem(e, 2)

    @pl.when(e == 0)
    def _():
        _row_gather(x1_hbm, xbuf.at[0], gsem.at[0], src_ref, 0, tm)
        for c in weight_copies(0, 0):
            c.start(priority=1)

    @pl.when(e + 1 < pl.num_programs(0))
    def _():
        for c in weight_copies(e + 1, 1 - wslot):
            c.start(priority=1)

    for c in weight_copies(e, wslot):
        c.wait()

    @pl.when(n_e > 0)
    def _():
        w1b[...] = w1f[wslot].astype(BF16)
        w3b[...] = w3f[wslot].astype(BF16)
        w2b[...] = w2f[wslot].astype(BF16)

    def tile(k, carry):
        g = g0 + k
        slot = lax.rem(g, 2)
        _row_gather(x1_hbm, xbuf.at[1 - slot], gsem.at[1 - slot], src_ref, (g + 1) * tm, tm, unroll=True)
        gather_wait(slot)
        xb = _from_token_tiles(xbuf.at[slot]).astype(BF16)
        h1 = jnp.dot(xb, w1b[...], preferred_element_type=F32)
        h3 = jnp.dot(xb, w3b[...], preferred_element_type=F32)
        h = (_silu(h1) * h3).astype(BF16)
        out = jnp.dot(h, w2b[...], preferred_element_type=F32)

        @pl.when(g > 0)
        def _():
            out_copy(g).wait()
        _to_token_tiles(obuf, out)
        out_copy(g).start()
        return carry

    lax.fori_loop(0, n_e, tile, 0)

    @pl.when(e == pl.num_programs(0) - 1)
    def _():
        gather_wait(lax.rem(nv, 2))
        out_copy(0).wait()
        obuf[...] = jnp.zeros_like(obuf)

        def zero_tile(g, carry):
            out_copy(g).start()
            out_copy(g).wait()
            return carry

        lax.fori_loop(nv, n_tiles, zero_tile, 0)


def _experts(x1, tstart, ntile, nv, src, w1, w3, w2, n_tiles, tm):
    any_spec = pl.BlockSpec(memory_space=pl.ANY)
    return pl.pallas_call(
        functools.partial(_expert_kernel, tm=tm, n_tiles=n_tiles),
        grid_spec=pltpu.PrefetchScalarGridSpec(
            num_scalar_prefetch=4,
            grid=(N_EXPERTS,),
            in_specs=[any_spec, any_spec, any_spec, any_spec],
            out_specs=any_spec,
            scratch_shapes=[pltpu.VMEM((2, tm * ROW_TILES, LANES), F32), pltpu.VMEM((tm * ROW_TILES, LANES), F32),
                            pltpu.SemaphoreType.DMA((2,)), pltpu.SemaphoreType.DMA(()),
                            pltpu.VMEM((2, D, D_EXPERT), F32), pltpu.VMEM((2, D, D_EXPERT), F32),
                            pltpu.VMEM((2, D_EXPERT, D), F32), pltpu.SemaphoreType.DMA((2,)),
                            pltpu.VMEM((D, D_EXPERT), BF16), pltpu.VMEM((D, D_EXPERT), BF16),
                            pltpu.VMEM((D_EXPERT, D), BF16)]),
        out_shape=jax.ShapeDtypeStruct((n_tiles * tm * ROW_TILES, LANES), F32),
        compiler_params=pltpu.CompilerParams(
            dimension_semantics=("arbitrary",), vmem_limit_bytes=VMEM_LIMIT, disable_bounds_checks=True),
        name="experts",
    )(tstart, ntile, nv, src, x1, w1, w3, w2)


def _combine_kernel(dest_ref, es_hbm, x1_ref, route_ref, g2_ref, b2_ref, op_ref, os_ref, gbuf, sems,
                    *, tm, n_prompt):
    i = pl.program_id(0)
    n = pl.num_programs(0)
    slot = lax.rem(i, 2)

    def start(step, s):
        for k in range(2):
            _row_gather(es_hbm, gbuf.at[s, k], sems.at[s], dest_ref, (k * n + step) * tm, tm)

    @pl.when(i == 0)
    def _():
        start(0, 0)

    @pl.when(i + 1 < n)
    def _():
        start(i + 1, 1 - slot)

    for k in range(2):
        pltpu.make_async_copy(es_hbm.at[pl.ds(0, tm * ROW_TILES), :], gbuf.at[slot, k], sems.at[slot]).wait()
    route = route_ref[...]
    w0 = route[:, 2:3]
    w1 = route[:, 3:4]
    moe = w0 * _from_token_tiles(gbuf.at[slot, 0]) + w1 * _from_token_tiles(gbuf.at[slot, 1])
    out = _layer_norm(ALPHA * _from_token_tiles(x1_ref) + moe, g2_ref[...], b2_ref[...])

    @pl.when(i < n_prompt)
    def _():
        op_ref[...] = out

    @pl.when(i >= n_prompt)
    def _():
        os_ref[...] = out


def _combine(es, dest_km, x1, route, p, tp):
    t = x1.shape[0] // ROW_TILES
    tm = ROW_TM
    n_p = tp // tm
    return pl.pallas_call(
        functools.partial(_combine_kernel, tm=tm, n_prompt=n_p),
        grid_spec=pltpu.PrefetchScalarGridSpec(
            num_scalar_prefetch=1,
            grid=(t // tm,),
            in_specs=[pl.BlockSpec(memory_space=pl.ANY),
                      pl.BlockSpec((tm * ROW_TILES, LANES), lambda i, d: (i, 0)),
                      pl.BlockSpec((tm, LANES), lambda i, d: (i, 0)),
                      pl.BlockSpec((1, D), lambda i, d: (0, 0)),
                      pl.BlockSpec((1, D), lambda i, d: (0, 0))],
            out_specs=[pl.BlockSpec((tm, D), lambda i, d: (jnp.minimum(i, n_p - 1), 0)),
                       pl.BlockSpec((tm, D), lambda i, d: (jnp.maximum(i - n_p, 0), 0))],
            scratch_shapes=[pltpu.VMEM((2, 2, tm * ROW_TILES, LANES), F32), pltpu.SemaphoreType.DMA((2,))]),
        out_shape=[jax.ShapeDtypeStruct((tp, D), F32), jax.ShapeDtypeStruct((t - tp, D), F32)],
        compiler_params=pltpu.CompilerParams(
            dimension_semantics=("arbitrary",), vmem_limit_bytes=VMEM_LIMIT, disable_bounds_checks=True),
        name="combine",
    )(dest_km, es, x1, route, p["ln2_g"], p["ln2_b"])


def _route_meta(e_flat, tm, n_tiles):
    a = e_flat.shape[0]
    ids = jnp.arange(N_EXPERTS, dtype=jnp.int32)
    onehot = (e_flat[:, None] == ids[None, :]).astype(jnp.int32)
    csum = jnp.cumsum(onehot, axis=0)
    rank = jnp.sum((csum - onehot) * onehot, axis=1)
    counts = csum[-1]
    tiles_per = (counts + tm - 1) // tm
    tile_end = jnp.cumsum(tiles_per)
    tile_start = tile_end - tiles_per
    nv = tile_end[-1]
    dest = tile_start[e_flat] * tm + rank
    src = jnp.zeros((n_tiles * tm,), jnp.int32).at[dest].set(jnp.arange(a, dtype=jnp.int32) // 2)
    return (tile_start.astype(jnp.int32), tiles_per.astype(jnp.int32), nv.reshape(1).astype(jnp.int32), src,
            dest.astype(jnp.int32))


def kernel(x_prompt, x_sample, state_ssm, state_conv, state_hgrn, w_in, conv_w, conv_b, dt_bias, a_log, d_skip, ssm_norm_g, hgrn_lb_logits, hgrn_norm_g, w_a, w_b, w_out, ln1_g, ln1_b, router_g_w, router_g_b, router_e_w, router_e_b, exp_w1, exp_w3, exp_w2, ln2_g, ln2_b):
    nbp, seqp, _ = x_prompt.shape
    nbs, seqs, _ = x_sample.shape
    tp, ts = nbp * seqp, nbs * seqs
    t = tp + ts
    l = 0
    x_p, x_s = x_prompt.reshape(tp, D), x_sample.reshape(ts, D)

    wt = jnp.transpose(w_in[l])
    o_dt = D + D + 2 * BCW
    o_q = o_dt + SSM_HEADS
    xb = jnp.concatenate([x_p.astype(BF16), x_s.astype(BF16)], axis=0)
    tm = PROJ_TM if t % PROJ_TM == 0 else 1024
    proj_a = _project(xb, wt, 0, o_dt, tm, PROJ_TN, BF16)
    dt_raw = jnp.pad(_project(xb, wt, o_dt, SSM_HEADS, tm, SSM_HEADS, F32), ((0, 0), (0, LANES - SSM_HEADS)))
    proj_b = _project(xb, wt, o_q, 6 * D, tm, PROJ_TN, BF16)

    lb_all = jnp.cumsum(jax.nn.softmax(hgrn_lb_logits.astype(F32), axis=0), axis=0)
    head_of = np.arange(D) // SSM_P
    pad128 = lambda v: jnp.pad(v, (0, LANES - v.shape[0])).reshape(1, LANES)
    p = {
        "cwx": conv_w[l][:, :D], "cwbc": conv_w[l][:, D:],
        "cbx": conv_b[l][:D].reshape(1, D), "cbbc": conv_b[l][D:].reshape(1, 2 * BCW),
        "dtb": pad128(dt_bias[l]), "alog": a_log[l], "alog_e": pad128(a_log[l]),
        "dskip_e": jnp.repeat(d_skip[l], SSM_P).reshape(1, D),
        "ssm_norm_g": ssm_norm_g[l].reshape(1, D),
        "e64": jnp.asarray(np.arange(LANES)[:, None] == head_of[None, :], dtype=BF16),
        "lb": lb_all[l].reshape(1, D), "hgrn_norm_g": hgrn_norm_g[l].reshape(1, D),
        "w_a": w_a[l].astype(BF16), "w_b": w_b[l].astype(BF16), "w_out": w_out[l].astype(BF16),
        "ln1_g": ln1_g[l].reshape(1, D), "ln1_b": ln1_b[l].reshape(1, D),
        "ln2_g": ln2_g[l].reshape(1, D), "ln2_b": ln2_b[l].reshape(1, D),
        "wr": jnp.pad(jnp.concatenate([router_g_w[l], router_e_w[l]], axis=1),
                      ((0, 0), (0, LANES - N_GROUPS - N_EXPERTS))),
        "br": pad128(jnp.concatenate([router_g_b[l], router_e_b[l]])),
    }

    ya_p, ssm_p, conv_p = _ssd(proj_a, dt_raw, 0, jnp.zeros((nbp, D, SSM_N), F32),
                               jnp.zeros((nbp, CONV_K - 1, D + 2 * BCW), F32), p, nbp, seqp)
    ya_s, ssm_s, conv_s = _ssd(proj_a, dt_raw, tp, state_ssm[l].reshape(nbs, D, SSM_N), state_conv[l],
                               p, nbs, seqs)
    yb_p, hg_p = _gla(proj_b, 0, jnp.zeros((nbp, D, HG_DK), F32), p, nbp, seqp)
    yb_s, hg_s = _gla(proj_b, tp, state_hgrn[l].reshape(nbs, D, HG_DK), p, nbs, seqs)

    merged = _gate(ya_p, ya_s, yb_p, yb_s, proj_b, p)
    x1, route = _merge(merged, x_p, x_s, p)

    e_flat = route[:, :2].astype(jnp.int32).reshape(-1)
    n_tiles = (2 * t) // MOE_TM + N_EXPERTS + 1
    tstart, ntile, nv, src, dest = _route_meta(e_flat, MOE_TM, n_tiles)
    es = _experts(x1, tstart, ntile, nv, src, exp_w1[l], exp_w3[l], exp_w2[l], n_tiles, MOE_TM)
    dest_km = dest.reshape(t // ROW_TM, ROW_TM, 2).transpose(2, 0, 1).reshape(-1)
    x2_p, x2_s = _combine(es, dest_km, x1, route, p, tp)

    y_prompt = x2_p.reshape(nbp, seqp, D)
    y_sample = x2_s.reshape(nbs, seqs, D)
    return (y_prompt, y_sample,
            ssm_p.reshape(1, nbp, SSM_HEADS, SSM_P, SSM_N), conv_p[None],
            hg_p.reshape(1, nbp, HG_HEADS, HG_DK, HG_DK),
            ssm_s.reshape(1, nbs, SSM_HEADS, SSM_P, SSM_N), conv_s[None],
            hg_s.reshape(1, nbs, HG_HEADS, HG_DK, HG_DK))
```

```python
import functools

import jax
import jax.numpy as jnp
import numpy as np
from jax import lax
from jax.experimental import pallas as pl
from jax.experimental.pallas import tpu as pltpu

F32 = jnp.float32
BF16 = jnp.bfloat16

D = 2048
SSM_HEADS = 32
SSM_P = 64
SSM_N = 128
SSM_G = 4
GW = D // SSM_G
BCW = SSM_G * SSM_N
CONV_K = 4
HG_HEADS = 16
HG_DK = 128
N_GROUPS = 4
EPG = 8
N_EXPERTS = 32
D_EXPERT = 512
EPS = 1e-5
ALPHA = 2.0 ** 0.25
NEG = -1e30
LANES = 128
ROW_TILES = D // LANES

VMEM_LIMIT = 56 * 1024 * 1024
SSD_CHUNK = 128
GLA_CHUNK = 64
GLA_SUBCHUNKS = 2
STEP_ROWS = 64
PROJ_TM = 2304
PROJ_TN = 512
MOE_TM = 128
ROW_TM = 256
MERGE_TM = 512
GATE_TM = 1024
MERGE_TN = 512
CARRY = 8


def _sigmoid(x):
    return 0.5 * jnp.tanh(0.5 * x) + 0.5


def _silu(x):
    return x * _sigmoid(x)


def _softplus(x):
    return jnp.maximum(x, 0.0) + jnp.log(1.0 + jnp.exp(-jnp.abs(x)))


def _split3(x):
    hi = x.astype(BF16)
    r = x - hi.astype(F32)
    mid = r.astype(BF16)
    lo = (r - mid.astype(F32)).astype(BF16)
    return hi, mid, lo


def _dot3_rhs(m_bf16, x):
    return sum(jnp.dot(m_bf16, part, preferred_element_type=F32) for part in _split3(x))


def _dot3_lhs(x, m_bf16):
    return sum(jnp.dot(part, m_bf16, preferred_element_type=F32) for part in _split3(x))


def _seq_masks(r, seq_len):
    shift = seq_len.bit_length() - 1
    row = lax.broadcasted_iota(jnp.int32, (r, r), 0)
    col = lax.broadcasted_iota(jnp.int32, (r, r), 1)
    same = lax.shift_right_logical(row, shift) == lax.shift_right_logical(col, shift)
    return same & (row >= col), same & (row <= col)


def _seq_totals(cum, nseq, seq_len):
    w = cum.shape[1]
    parts = [jnp.broadcast_to(cum[(b + 1) * seq_len - 1:(b + 1) * seq_len, :], (seq_len, w)) for b in range(nseq)]
    return parts[0] if nseq == 1 else jnp.concatenate(parts, axis=0)


def _pad_rows(x, rows):
    if x.shape[0] == rows:
        return x
    return jnp.concatenate([x, jnp.zeros((rows - x.shape[0], x.shape[1]), x.dtype)], axis=0)


def _to_token_tiles(ref, val):
    rows = val.shape[0]
    for s in range(ROW_TILES):
        ref[pl.ds(s, rows, stride=ROW_TILES), :] = val[:, s * LANES:(s + 1) * LANES]


def _from_token_tiles(ref):
    rows = ref.shape[0] // ROW_TILES
    return jnp.concatenate([ref[pl.ds(s, rows, stride=ROW_TILES), :] for s in range(ROW_TILES)], axis=1)


def _row_mask(x, b, seq_len):
    row = lax.broadcasted_iota(jnp.int32, x.shape, 0)
    return jnp.where((row >= b * seq_len) & (row < (b + 1) * seq_len), x, jnp.zeros_like(x))


def _mm_nt_kernel(x_ref, wt_ref, o_ref):
    w = wt_ref[...].astype(BF16)
    o_ref[...] = lax.dot_general(x_ref[...], w, (((1,), (1,)), ((), ())),
                                 preferred_element_type=F32).astype(o_ref.dtype)


def _project(x, wt, row0, n, tm, tn, out_dtype):
    m, k = x.shape
    assert n % tn == 0 and m % tm == 0 and row0 % 8 == 0
    if row0 % tn == 0:
        w_spec = pl.BlockSpec((tn, k), lambda i, j: (row0 // tn + j, 0))
    else:
        w_spec = pl.BlockSpec((pl.Element(tn), pl.Element(k)),
                              lambda i, j: (pl.multiple_of(row0 + j * tn, 8), 0))
    return pl.pallas_call(
        _mm_nt_kernel,
        grid=(m // tm, n // tn),
        in_specs=[pl.BlockSpec((tm, k), lambda i, j: (i, 0)), w_spec],
        out_specs=pl.BlockSpec((tm, tn), lambda i, j: (i, j)),
        out_shape=jax.ShapeDtypeStruct((m, n), out_dtype),
        compiler_params=pltpu.CompilerParams(
            dimension_semantics=("parallel", "arbitrary"), vmem_limit_bytes=VMEM_LIMIT),
        name="proj",
    )(x, wt)


def _conv_silu(buf_ref, carry0_ref, u, w_ref, b_ref, nseq, seq_len, first):
    ch = u.shape[1]

    @pl.when(first)
    def _():
        buf_ref[:, 0:CARRY, :] = carry0_ref[...]
    u3 = u.reshape(nseq, seq_len, ch)
    buf_ref[:, CARRY:CARRY + seq_len, :] = u3
    acc = b_ref[...] + w_ref[CONV_K - 1:CONV_K, :] * u3
    for k in range(CONV_K - 1):
        off = CARRY - (CONV_K - 1) + k
        acc = acc + w_ref[k:k + 1, :] * buf_ref[:, off:off + seq_len, :]
    tail = buf_ref[:, seq_len:seq_len + CARRY, :]
    buf_ref[:, 0:CARRY, :] = tail
    return _silu(acc).reshape(nseq * seq_len, ch), tail


def _pair_cols(tile):
    lane = lax.broadcasted_iota(jnp.int32, tile.shape, 1)
    swapped = pltpu.roll(tile, SSM_P, axis=1)
    return jnp.where(lane < SSM_P, tile, swapped), jnp.where(lane < SSM_P, swapped, tile)


def _ssd_kernel(xp_ref, bcp_ref, z_ref, dtr_ref, dtrt_ref, h0_ref, cx0_ref, cbc0_ref,
                cwx_ref, cwbc_ref, cbx_ref, cbbc_ref, dtb_ref, dtbt_ref, alogt_ref, aloge_ref,
                dskip_ref, normg_ref, e64_ref,
                y_ref, hout_ref, ctx_ref, ctbc_ref,
                xbuf, bcbuf, h_ref, *, nseq, seq_len):
    r = nseq * seq_len
    rp = max(r, LANES)
    zi = pl.program_id(1)
    first = zi == 0

    @pl.when(first)
    def _():
        h_ref[...] = h0_ref[...]

    xs, tail_x = _conv_silu(xbuf, cx0_ref, xp_ref[...].astype(F32), cwx_ref, cbx_ref, nseq, seq_len, first)
    bc, tail_bc = _conv_silu(bcbuf, cbc0_ref, bcp_ref[...].astype(F32), cwbc_ref, cbbc_ref, nseq, seq_len, first)
    ctx_ref[...] = tail_x
    ctbc_ref[...] = tail_bc

    causal, anti = _seq_masks(r, seq_len)
    dt = _softplus(dtr_ref[...] + dtb_ref[...])
    dt_e = _dot3_lhs(dt, e64_ref[...])
    cum = _dot3_rhs(causal.astype(BF16), dt * (-jnp.exp(aloge_ref[...])))
    cum_e = _dot3_lhs(cum, e64_ref[...])
    loga_t = _softplus(dtrt_ref[0] + dtbt_ref[...]) * (-jnp.exp(alogt_ref[...]))
    cum_t = _dot3_lhs(loga_t, anti.astype(BF16))
    tot_e = _seq_totals(cum_e, nseq, seq_len)
    xdt = xs * dt_e
    in_scale = jnp.exp(cum_e)
    xw = xdt * jnp.exp(tot_e - cum_e)
    xw_t = _pad_rows(xw, rp).T.astype(BF16)
    lane = lax.broadcasted_iota(jnp.int32, (r, LANES), 1)

    y_groups = []
    for g in range(SSM_G):
        bg = bc[:, g * SSM_N:(g + 1) * SSM_N]
        cg = bc[:, BCW + g * SSM_N:BCW + (g + 1) * SSM_N]
        cb = lax.dot_general(cg, bg, (((1,), (1,)), ((), ())), preferred_element_type=F32)
        rows = slice(g * GW, (g + 1) * GW)
        pieces = []
        for j in range(GW // LANES):
            col0 = g * GW + j * LANES
            xpair = xdt[:, col0:col0 + LANES]
            ms, rhs = [], []
            for half, colb in enumerate(_pair_cols(cum_e[:, col0:col0 + LANES])):
                head = col0 // SSM_P + half
                seg = colb[:, :r] - cum_t[head:head + 1, :]
                ms.append((cb * jnp.exp(jnp.where(causal, seg, NEG))).astype(BF16))
                rhs.append(jnp.where(lane >= SSM_P if half else lane < SSM_P, xpair, 0.0).astype(BF16))
            if r % LANES == 0:
                pieces.append(jnp.dot(jnp.concatenate(ms, axis=1), jnp.concatenate(rhs, axis=0),
                                      preferred_element_type=F32))
            else:
                pieces.append(jnp.dot(ms[0], rhs[0], preferred_element_type=F32)
                              + jnp.dot(ms[1], rhs[1], preferred_element_type=F32))
        y_intra = jnp.concatenate(pieces, axis=1)
        y_inter = []
        for b in range(nseq):
            rb = slice(b * seq_len, (b + 1) * seq_len)
            hg = h_ref[b, rows, :]
            y_inter.append(lax.dot_general(cg[rb, :], hg, (((1,), (1,)), ((), ())),
                                           preferred_element_type=F32))
            bmask = bg if nseq == 1 else _row_mask(bg, b, seq_len)
            st = jnp.dot(xw_t[rows, :], _pad_rows(bmask, rp).astype(BF16), preferred_element_type=F32)
            dec8 = jnp.exp(tot_e[b * seq_len:b * seq_len + 8, rows])
            for j in range(GW // LANES):
                for half, dcol in enumerate(_pair_cols(dec8[:, j * LANES:(j + 1) * LANES])):
                    h8 = 2 * j + half
                    hr = slice(g * GW + h8 * SSM_P, g * GW + (h8 + 1) * SSM_P)
                    h_ref[b, hr, :] = dcol[0:1, :] * h_ref[b, hr, :] + st[h8 * SSM_P:(h8 + 1) * SSM_P, :]
        y_inter = y_inter[0] if nseq == 1 else jnp.concatenate(y_inter, axis=0)
        y_groups.append(y_intra + y_inter * in_scale[:, rows])
    y = jnp.concatenate(y_groups, axis=1) + dskip_ref[...] * xs
    yz = y * _silu(z_ref[...].astype(F32))
    outs = []
    for g in range(SSM_G):
        blk = yz[:, g * GW:(g + 1) * GW]
        ms = jnp.sum(blk * blk, axis=1, keepdims=True) * (1.0 / GW)
        outs.append(blk * lax.rsqrt(ms + EPS))
    y_ref[...] = (jnp.concatenate(outs, axis=1) * normg_ref[...]).astype(y_ref.dtype)

    @pl.when(zi == pl.num_programs(1) - 1)
    def _():
        hout_ref[...] = h_ref[...]


def _ssd(proj, dt_raw, row0, h0, conv0, p, nb, seq):
    seq_len = SSD_CHUNK if seq % SSD_CHUNK == 0 else seq
    nseq = 1 if seq_len == SSD_CHUNK else max(1, min(nb, STEP_ROWS // seq_len))
    nz = seq // seq_len
    r = nseq * seq_len
    t = nb * seq
    r0 = row0 // r
    nblk = t // r
    dtrt = dt_raw[row0:row0 + t, :SSM_HEADS].reshape(nblk, r, SSM_HEADS).transpose(0, 2, 1)
    pad_rows = CARRY - (CONV_K - 1)
    cx0 = jnp.pad(conv0[:, :, :D], ((0, 0), (pad_rows, 0), (0, 0)))
    cbc0 = jnp.pad(conv0[:, :, D:], ((0, 0), (pad_rows, 0), (0, 0)))
    rowblk = lambda col: pl.BlockSpec((r, D), lambda b, z: (r0 + b * nz + z, col))
    const = lambda shape: pl.BlockSpec(shape, lambda b, z: tuple(0 for _ in shape))
    per_b = lambda shape: pl.BlockSpec((nseq,) + shape, lambda b, z: (b, 0, 0))
    y, hout, ctx, ctbc = pl.pallas_call(
        functools.partial(_ssd_kernel, nseq=nseq, seq_len=seq_len),
        grid=(nb // nseq, nz),
        in_specs=[rowblk(1),
                  pl.BlockSpec((r, 2 * BCW), lambda b, z: (r0 + b * nz + z, 4)),
                  rowblk(0),
                  pl.BlockSpec((r, LANES), lambda b, z: (r0 + b * nz + z, 0)),
                  pl.BlockSpec((1, SSM_HEADS, r), lambda b, z: (b * nz + z, 0, 0)),
                  per_b((D, SSM_N)), per_b((CARRY, D)), per_b((CARRY, 2 * BCW)),
                  const((CONV_K, D)), const((CONV_K, 2 * BCW)), const((1, D)), const((1, 2 * BCW)),
                  const((1, LANES)), const((SSM_HEADS, r)), const((SSM_HEADS, r)), const((1, LANES)),
                  const((1, D)), const((1, D)), const((LANES, D))],
        out_specs=[pl.BlockSpec((r, D), lambda b, z: (b * nz + z, 0)),
                   per_b((D, SSM_N)), per_b((CARRY, D)), per_b((CARRY, 2 * BCW))],
        out_shape=[jax.ShapeDtypeStruct((t, D), BF16),
                   jax.ShapeDtypeStruct((nb, D, SSM_N), F32),
                   jax.ShapeDtypeStruct((nb, CARRY, D), F32),
                   jax.ShapeDtypeStruct((nb, CARRY, 2 * BCW), F32)],
        scratch_shapes=[pltpu.VMEM((nseq, seq_len + CARRY, D), F32),
                        pltpu.VMEM((nseq, seq_len + CARRY, 2 * BCW), F32),
                        pltpu.VMEM((nseq, D, SSM_N), F32)],
        compiler_params=pltpu.CompilerParams(
            dimension_semantics=("parallel", "arbitrary"), vmem_limit_bytes=VMEM_LIMIT),
        name="ssd",
    )(proj, proj, proj, dt_raw, dtrt, h0, cx0, cbc0,
      p["cwx"], p["cwbc"], p["cbx"], p["cbbc"], p["dtb"],
      jnp.broadcast_to(p["dtb"][0, :SSM_HEADS, None], (SSM_HEADS, r)),
      jnp.broadcast_to(p["alog"][:, None], (SSM_HEADS, r)),
      p["alog_e"], p["dskip_e"], p["ssm_norm_g"], p["e64"])
    conv_new = jnp.concatenate([ctx[:, pad_rows:], ctbc[:, pad_rows:]], axis=-1)
    return y, hout, conv_new


def _gla_kernel(q_ref, f_ref, i_ref, g_ref, s0_ref, lb_ref, normg_ref, y_ref, sout_ref, s_ref,
                *, nseq, seq_len, nsub):
    r = nseq * seq_len
    rp = max(r, LANES)
    zi = pl.program_id(1)

    @pl.when(zi == 0)
    def _():
        s_ref[...] = s0_ref[...]

    lb = lb_ref[...]
    causal, _ = _seq_masks(r, seq_len)
    for sub in range(nsub):
        rows = slice(sub * r, (sub + 1) * r)
        f = lb + (1.0 - lb) * _sigmoid(f_ref[rows, :].astype(F32))
        k = 1.0 - f
        q = _silu(q_ref[rows, :].astype(F32))
        vb = i_ref[rows, :]
        bc = _dot3_rhs(causal.astype(BF16), jnp.log(f))
        tot = _seq_totals(bc, nseq, seq_len)
        qe = q * jnp.exp(bc)
        ke = k * jnp.exp(-bc)
        kd_t = _pad_rows(ke * jnp.exp(tot), rp).T.astype(BF16)
        gate = _silu(g_ref[rows, :].astype(F32))
        outs = []
        for h in range(HG_HEADS):
            sl = slice(h * HG_DK, (h + 1) * HG_DK)
            qh = qe[:, sl]
            attn = lax.dot_general(qh, ke[:, sl], (((1,), (1,)), ((), ())), preferred_element_type=F32)
            attn = jnp.where(causal, attn, 0.0)
            o = jnp.dot(attn.astype(BF16), vb[:, sl], preferred_element_type=F32)
            o_inter = []
            for b in range(nseq):
                rb = slice(b * seq_len, (b + 1) * seq_len)
                sh = s_ref[b, sl, :]
                o_inter.append(jnp.dot(qh[rb, :], sh, preferred_element_type=F32))
                vmask = vb[:, sl] if nseq == 1 else _row_mask(vb[:, sl], b, seq_len)
                st = jnp.dot(kd_t[sl, :], _pad_rows(vmask, rp), preferred_element_type=F32)
                dec = jnp.exp(tot[b * seq_len:b * seq_len + 1, sl])
                dec_col = jnp.broadcast_to(dec, (HG_DK, HG_DK)).T
                s_ref[b, sl, :] = dec_col * sh + st
            o = o + (o_inter[0] if nseq == 1 else jnp.concatenate(o_inter, axis=0))
            ms = jnp.sum(o * o, axis=1, keepdims=True) * (1.0 / HG_DK)
            outs.append(o * lax.rsqrt(ms + EPS))
        y_ref[rows, :] = (jnp.concatenate(outs, axis=1) * normg_ref[...] * gate).astype(y_ref.dtype)

    @pl.when(zi == pl.num_programs(1) - 1)
    def _():
        sout_ref[...] = s_ref[...]


def _gla(proj, row0, s0, p, nb, seq):
    seq_len = GLA_CHUNK if seq % GLA_CHUNK == 0 else seq
    nseq = 1 if seq_len == GLA_CHUNK else max(1, min(nb, STEP_ROWS // seq_len))
    nsub = GLA_SUBCHUNKS if nseq == 1 and (seq // seq_len) % GLA_SUBCHUNKS == 0 else 1
    nz = seq // (seq_len * nsub)
    r = nseq * seq_len * nsub
    r0 = row0 // r
    rowblk = lambda col: pl.BlockSpec((r, D), lambda b, z: (r0 + b * nz + z, col))
    const = lambda shape: pl.BlockSpec(shape, lambda b, z: tuple(0 for _ in shape))
    per_b = lambda shape: pl.BlockSpec((nseq,) + shape, lambda b, z: (b, 0, 0))
    return pl.pallas_call(
        functools.partial(_gla_kernel, nseq=nseq, seq_len=seq_len, nsub=nsub),
        grid=(nb // nseq, nz),
        in_specs=[rowblk(0), rowblk(1), rowblk(2), rowblk(3), per_b((D, HG_DK)), const((1, D)), const((1, D))],
        out_specs=[pl.BlockSpec((r, D), lambda b, z: (b * nz + z, 0)), per_b((D, HG_DK))],
        out_shape=[jax.ShapeDtypeStruct((nb * seq, D), BF16), jax.ShapeDtypeStruct((nb, D, HG_DK), F32)],
        scratch_shapes=[pltpu.VMEM((nseq, D, HG_DK), F32)],
        compiler_params=pltpu.CompilerParams(
            dimension_semantics=("parallel", "arbitrary"), vmem_limit_bytes=VMEM_LIMIT),
        name="gla",
    )(proj, proj, proj, proj, s0, p["lb"], p["hgrn_norm_g"])


def _layer_norm(x, g, b):
    mu = jnp.mean(x, axis=1, keepdims=True)
    xc = x - mu
    var = jnp.mean(xc * xc, axis=1, keepdims=True)
    return xc * lax.rsqrt(var + EPS) * g + b


def _gate_kernel(yap_ref, yas_ref, ybp_ref, ybs_ref, ga_ref, gb_ref, wa_ref, wb_ref, o_ref, *, n_prompt):
    is_prompt = pl.program_id(0) < n_prompt
    ya = jnp.where(is_prompt, yap_ref[...], yas_ref[...])
    yb = jnp.where(is_prompt, ybp_ref[...], ybs_ref[...])
    a = jnp.dot(ya, wa_ref[...], preferred_element_type=F32)
    b = jnp.dot(yb, wb_ref[...], preferred_element_type=F32)
    ga = _sigmoid(ga_ref[...].astype(F32))
    gb = _sigmoid(gb_ref[...].astype(F32))
    o_ref[...] = (ga * a + gb * b).astype(o_ref.dtype)


def _gate(ya_p, ya_s, yb_p, yb_s, proj, p):
    tm = GATE_TM if ya_p.shape[0] % GATE_TM == 0 and ya_s.shape[0] % GATE_TM == 0 else MERGE_TM
    tn = MERGE_TN
    n_p = ya_p.shape[0] // tm
    t = ya_p.shape[0] + ya_s.shape[0]
    prow = pl.BlockSpec((tm, D), lambda i, j: (jnp.minimum(i, n_p - 1), 0))
    srow = pl.BlockSpec((tm, D), lambda i, j: (jnp.maximum(i - n_p, 0), 0))
    gcol = lambda seg: pl.BlockSpec((tm, tn), lambda i, j: (i, seg * (D // tn) + j))
    wcol = pl.BlockSpec((D, tn), lambda i, j: (0, j))
    return pl.pallas_call(
        functools.partial(_gate_kernel, n_prompt=n_p),
        grid=(t // tm, D // tn),
        in_specs=[prow, srow, prow, srow, gcol(4), gcol(5), wcol, wcol],
        out_specs=pl.BlockSpec((tm, tn), lambda i, j: (i, j)),
        out_shape=jax.ShapeDtypeStruct((t, D), BF16),
        compiler_params=pltpu.CompilerParams(
            dimension_semantics=("parallel", "arbitrary"), vmem_limit_bytes=VMEM_LIMIT),
        name="gate",
    )(ya_p, ya_s, yb_p, yb_s, proj, proj, p["w_a"], p["w_b"])


def _merge_kernel(m_ref, xp_ref, xs_ref, wo_ref, g1_ref, b1_ref, wrh_ref, wrl_ref, br_ref,
                  x1_ref, route_ref, *, n_prompt):
    x = jnp.where(pl.program_id(0) < n_prompt, xp_ref[...], xs_ref[...])
    mix = jnp.dot(m_ref[...], wo_ref[...], preferred_element_type=F32)
    x1 = _layer_norm(ALPHA * x + mix, g1_ref[...], b1_ref[...])
    _to_token_tiles(x1_ref, x1)
    hi = x1.astype(BF16)
    lo = (x1 - hi.astype(F32)).astype(BF16)
    logits = (jnp.dot(hi, wrh_ref[...], preferred_element_type=F32)
              + jnp.dot(lo, wrh_ref[...], preferred_element_type=F32)
              + jnp.dot(hi, wrl_ref[...], preferred_element_type=F32) + br_ref[...])
    lane = lax.broadcasted_iota(jnp.int32, logits.shape, 1).astype(F32)
    gl = jnp.where(lane < N_GROUPS, logits, NEG)
    gmax = jnp.max(gl, axis=1, keepdims=True)
    gsel = jnp.min(jnp.where(gl == gmax, lane, 1e9), axis=1, keepdims=True)
    p_grp = 1.0 / jnp.sum(jnp.exp(gl - gmax), axis=1, keepdims=True)
    lo = N_GROUPS + gsel * EPG
    el = jnp.where((lane >= lo) & (lane < lo + EPG), logits, NEG)
    v0 = jnp.max(el, axis=1, keepdims=True)
    i0 = jnp.min(jnp.where(el == v0, lane, 1e9), axis=1, keepdims=True)
    el2 = jnp.where(lane == i0, NEG, el)
    v1 = jnp.max(el2, axis=1, keepdims=True)
    i1 = jnp.min(jnp.where(el2 == v1, lane, 1e9), axis=1, keepdims=True)
    e1 = jnp.exp(v1 - v0)
    p0 = 1.0 / (1.0 + e1)
    p1 = e1 * p0
    route = jnp.where(lane == 0, i0 - N_GROUPS,
                      jnp.where(lane == 1, i1 - N_GROUPS,
                                jnp.where(lane == 2, p_grp * p0,
                                          jnp.where(lane == 3, p_grp * p1, 0.0))))
    route_ref[...] = route


def _merge(merged, x_p, x_s, p):
    tm = MERGE_TM
    n_p = x_p.shape[0] // tm
    t = x_p.shape[0] + x_s.shape[0]
    row = pl.BlockSpec((tm, D), lambda i: (i, 0))
    prow = pl.BlockSpec((tm, D), lambda i: (jnp.minimum(i, n_p - 1), 0))
    srow = pl.BlockSpec((tm, D), lambda i: (jnp.maximum(i - n_p, 0), 0))
    res = lambda shape: pl.BlockSpec(shape, lambda i: (0, 0), pipeline_mode=pl.Buffered(1))
    wr_hi = p["wr"].astype(BF16)
    wr_lo = (p["wr"] - wr_hi.astype(F32)).astype(BF16)
    return pl.pallas_call(
        functools.partial(_merge_kernel, n_prompt=n_p),
        grid=(t // tm,),
        in_specs=[row, prow, srow, res((D, D)), res((1, D)), res((1, D)),
                  res((D, LANES)), res((D, LANES)), res((1, LANES))],
        out_specs=[pl.BlockSpec((tm * ROW_TILES, LANES), lambda i: (i, 0)),
                   pl.BlockSpec((tm, LANES), lambda i: (i, 0))],
        out_shape=[jax.ShapeDtypeStruct((t * ROW_TILES, LANES), F32), jax.ShapeDtypeStruct((t, LANES), F32)],
        compiler_params=pltpu.CompilerParams(
            dimension_semantics=("parallel",), vmem_limit_bytes=VMEM_LIMIT),
        name="merge",
    )(merged, x_p, x_s, p["w_out"], p["ln1_g"], p["ln1_b"], wr_hi, wr_lo, p["br"])


def _row_gather(src_hbm, dst, sem, idx_ref, base, n, unroll=8):
    def body(j, carry):
        r = idx_ref[base + j]
        pltpu.make_async_copy(src_hbm.at[pl.ds(pl.multiple_of(r * ROW_TILES, ROW_TILES), ROW_TILES), :],
                              dst.at[pl.ds(j * ROW_TILES, ROW_TILES), :], sem).start()
        return carry
    lax.fori_loop(0, n, body, 0, unroll=unroll)


def _expert_kernel(tstart_ref, ntile_ref, nv_ref, src_ref, x1_hbm, w1_hbm, w3_hbm, w2_hbm, es_hbm,
                   xbuf, obuf, gsem, osem, w1f, w3f, w2f, wsem, w1b, w3b, w2b, *, tm, n_tiles):
    e = pl.program_id(0)
    n_e = ntile_ref[e]
    g0 = tstart_ref[e]
    nv = nv_ref[0]
    rows = tm * ROW_TILES

    def gather_wait(s):
        pltpu.make_async_copy(x1_hbm.at[pl.ds(0, rows), :], xbuf.at[s], gsem.at[s]).wait()

    def out_copy(g):
        return pltpu.make_async_copy(obuf, es_hbm.at[pl.ds(pl.multiple_of(g * rows, rows), rows), :], osem)

    def weight_copies(ex, s):
        return [pltpu.make_async_copy(w_hbm.at[ex], w_f.at[s], wsem.at[s])
                for w_hbm, w_f in ((w1_hbm, w1f), (w3_hbm, w3f), (w2_hbm, w2f))]

    wslot = lax.rem(e, 2)

    @pl.when(e == 0)
    def _():
        _row_gather(x1_hbm, xbuf.at[0], gsem.at[0], src_ref, 0, tm)
        for c in weight_copies(0, 0):
            c.start(priority=1)

    @pl.when(e + 1 < pl.num_programs(0))
    def _():
        for c in weight_copies(e + 1, 1 - wslot):
            c.start(priority=1)

    for c in weight_copies(e, wslot):
        c.wait()

    @pl.when(n_e > 0)
    def _():
        w1b[...] = w1f[wslot].astype(BF16)
        w3b[...] = w3f[wslot].astype(BF16)
        w2b[...] = w2f[wslot].astype(BF16)

    def tile(k, carry):
        g = g0 + k
        slot = lax.rem(g, 2)
        gather_wait(slot)
        _row_gather(x1_hbm, xbuf.at[1 - slot], gsem.at[1 - slot], src_ref, (g + 1) * tm, tm, unroll=True)
        xb = _from_token_tiles(xbuf.at[slot]).astype(BF16)
        h1 = jnp.dot(xb, w1b[...], preferred_element_type=F32)
        h3 = jnp.dot(xb, w3b[...], preferred_element_type=F32)
        h = (_silu(h1) * h3).astype(BF16)
        out = jnp.dot(h, w2b[...], preferred_element_type=F32)

        @pl.when(g > 0)
        def _():
            out_copy(g).wait()
        _to_token_tiles(obuf, out)
        out_copy(g).start()
        return carry

    lax.fori_loop(0, n_e, tile, 0)

    @pl.when(e == pl.num_programs(0) - 1)
    def _():
        gather_wait(lax.rem(nv, 2))
        out_copy(0).wait()
        obuf[...] = jnp.zeros_like(obuf)

        def zero_tile(g, carry):
            out_copy(g).start()
            out_copy(g).wait()
            return carry

        lax.fori_loop(nv, n_tiles, zero_tile, 0)


def _experts(x1, tstart, ntile, nv, src, w1, w3, w2, n_tiles, tm):
    any_spec = pl.BlockSpec(memory_space=pl.ANY)
    return pl.pallas_call(
        functools.partial(_expert_kernel, tm=tm, n_tiles=n_tiles),
        grid_spec=pltpu.PrefetchScalarGridSpec(
            num_scalar_prefetch=4,
            grid=(N_EXPERTS,),
            in_specs=[any_spec, any_spec, any_spec, any_spec],
            out_specs=any_spec,
            scratch_shapes=[pltpu.VMEM((2, tm * ROW_TILES, LANES), F32), pltpu.VMEM((tm * ROW_TILES, LANES), F32),
                            pltpu.SemaphoreType.DMA((2,)), pltpu.SemaphoreType.DMA(()),
                            pltpu.VMEM((2, D, D_EXPERT), F32), pltpu.VMEM((2, D, D_EXPERT), F32),
                            pltpu.VMEM((2, D_EXPERT, D), F32), pltpu.SemaphoreType.DMA((2,)),
                            pltpu.VMEM((D, D_EXPERT), BF16), pltpu.VMEM((D, D_EXPERT), BF16),
                            pltpu.VMEM((D_EXPERT, D), BF16)]),
        out_shape=jax.ShapeDtypeStruct((n_tiles * tm * ROW_TILES, LANES), F32),
        compiler_params=pltpu.CompilerParams(
            dimension_semantics=("arbitrary",), vmem_limit_bytes=VMEM_LIMIT, disable_bounds_checks=True),
        name="experts",
    )(tstart, ntile, nv, src, x1, w1, w3, w2)


def _combine_kernel(dest_ref, es_hbm, x1_ref, route_ref, g2_ref, b2_ref, op_ref, os_ref, gbuf, sems,
                    *, tm, n_prompt):
    i = pl.program_id(0)
    n = pl.num_programs(0)
    slot = lax.rem(i, 2)

    def start(step, s):
        for k in range(2):
            _row_gather(es_hbm, gbuf.at[s, k], sems.at[s], dest_ref, (k * n + step) * tm, tm)

    @pl.when(i == 0)
    def _():
        start(0, 0)

    @pl.when(i + 1 < n)
    def _():
        start(i + 1, 1 - slot)

    for k in range(2):
        pltpu.make_async_copy(es_hbm.at[pl.ds(0, tm * ROW_TILES), :], gbuf.at[slot, k], sems.at[slot]).wait()
    route = route_ref[...]
    w0 = route[:, 2:3]
    w1 = route[:, 3:4]
    moe = w0 * _from_token_tiles(gbuf.at[slot, 0]) + w1 * _from_token_tiles(gbuf.at[slot, 1])
    out = _layer_norm(ALPHA * _from_token_tiles(x1_ref) + moe, g2_ref[...], b2_ref[...])

    @pl.when(i < n_prompt)
    def _():
        op_ref[...] = out

    @pl.when(i >= n_prompt)
    def _():
        os_ref[...] = out


def _combine(es, dest_km, x1, route, p, tp):
    t = x1.shape[0] // ROW_TILES
    tm = ROW_TM
    n_p = tp // tm
    return pl.pallas_call(
        functools.partial(_combine_kernel, tm=tm, n_prompt=n_p),
        grid_spec=pltpu.PrefetchScalarGridSpec(
            num_scalar_prefetch=1,
            grid=(t // tm,),
            in_specs=[pl.BlockSpec(memory_space=pl.ANY),
                      pl.BlockSpec((tm * ROW_TILES, LANES), lambda i, d: (i, 0)),
                      pl.BlockSpec((tm, LANES), lambda i, d: (i, 0)),
                      pl.BlockSpec((1, D), lambda i, d: (0, 0)),
                      pl.BlockSpec((1, D), lambda i, d: (0, 0))],
            out_specs=[pl.BlockSpec((tm, D), lambda i, d: (jnp.minimum(i, n_p - 1), 0)),
                       pl.BlockSpec((tm, D), lambda i, d: (jnp.maximum(i - n_p, 0), 0))],
            scratch_shapes=[pltpu.VMEM((2, 2, tm * ROW_TILES, LANES), F32), pltpu.SemaphoreType.DMA((2,))]),
        out_shape=[jax.ShapeDtypeStruct((tp, D), F32), jax.ShapeDtypeStruct((t - tp, D), F32)],
        compiler_params=pltpu.CompilerParams(
            dimension_semantics=("arbitrary",), vmem_limit_bytes=VMEM_LIMIT, disable_bounds_checks=True),
        name="combine",
    )(dest_km, es, x1, route, p["ln2_g"], p["ln2_b"])


def _route_meta(e_flat, tm, n_tiles):
    a = e_flat.shape[0]
    ids = jnp.arange(N_EXPERTS, dtype=jnp.int32)
    onehot = (e_flat[:, None] == ids[None, :]).astype(jnp.int32)
    csum = jnp.cumsum(onehot, axis=0)
    rank = jnp.sum((csum - onehot) * onehot, axis=1)
    counts = csum[-1]
    tiles_per = (counts + tm - 1) // tm
    tile_end = jnp.cumsum(tiles_per)
    tile_start = tile_end - tiles_per
    nv = tile_end[-1]
    dest = tile_start[e_flat] * tm + rank
    src = jnp.zeros((n_tiles * tm,), jnp.int32).at[dest].set(jnp.arange(a, dtype=jnp.int32) // 2)
    return (tile_start.astype(jnp.int32), tiles_per.astype(jnp.int32), nv.reshape(1).astype(jnp.int32), src,
            dest.astype(jnp.int32))


def kernel(x_prompt, x_sample, state_ssm, state_conv, state_hgrn, w_in, conv_w, conv_b, dt_bias, a_log, d_skip, ssm_norm_g, hgrn_lb_logits, hgrn_norm_g, w_a, w_b, w_out, ln1_g, ln1_b, router_g_w, router_g_b, router_e_w, router_e_b, exp_w1, exp_w3, exp_w2, ln2_g, ln2_b):
    nbp, seqp, _ = x_prompt.shape
    nbs, seqs, _ = x_sample.shape
    tp, ts = nbp * seqp, nbs * seqs
    t = tp + ts
    l = 0
    x_p, x_s = x_prompt.reshape(tp, D), x_sample.reshape(ts, D)

    wt = jnp.transpose(w_in[l])
    o_dt = D + D + 2 * BCW
    o_q = o_dt + SSM_HEADS
    xb = jnp.concatenate([x_p.astype(BF16), x_s.astype(BF16)], axis=0)
    tm = PROJ_TM if t % PROJ_TM == 0 else 1024
    proj_a = _project(xb, wt, 0, o_dt, tm, PROJ_TN, BF16)
    dt_raw = jnp.pad(_project(xb, wt, o_dt, SSM_HEADS, tm, SSM_HEADS, F32), ((0, 0), (0, LANES - SSM_HEADS)))
    proj_b = _project(xb, wt, o_q, 6 * D, tm, PROJ_TN, BF16)

    lb_all = jnp.cumsum(jax.nn.softmax(hgrn_lb_logits.astype(F32), axis=0), axis=0)
    head_of = np.arange(D) // SSM_P
    pad128 = lambda v: jnp.pad(v, (0, LANES - v.shape[0])).reshape(1, LANES)
    p = {
        "cwx": conv_w[l][:, :D], "cwbc": conv_w[l][:, D:],
        "cbx": conv_b[l][:D].reshape(1, D), "cbbc": conv_b[l][D:].reshape(1, 2 * BCW),
        "dtb": pad128(dt_bias[l]), "alog": a_log[l], "alog_e": pad128(a_log[l]),
        "dskip_e": jnp.repeat(d_skip[l], SSM_P).reshape(1, D),
        "ssm_norm_g": ssm_norm_g[l].reshape(1, D),
        "e64": jnp.asarray(np.arange(LANES)[:, None] == head_of[None, :], dtype=BF16),
        "lb": lb_all[l].reshape(1, D), "hgrn_norm_g": hgrn_norm_g[l].reshape(1, D),
        "w_a": w_a[l].astype(BF16), "w_b": w_b[l].astype(BF16), "w_out": w_out[l].astype(BF16),
        "ln1_g": ln1_g[l].reshape(1, D), "ln1_b": ln1_b[l].reshape(1, D),
        "ln2_g": ln2_g[l].reshape(1, D), "ln2_b": ln2_b[l].reshape(1, D),
        "wr": jnp.pad(jnp.concatenate([router_g_w[l], router_e_w[l]], axis=1),
                      ((0, 0), (0, LANES - N_GROUPS - N_EXPERTS))),
        "br": pad128(jnp.concatenate([router_g_b[l], router_e_b[l]])),
    }

    ya_p, ssm_p, conv_p = _ssd(proj_a, dt_raw, 0, jnp.zeros((nbp, D, SSM_N), F32),
                               jnp.zeros((nbp, CONV_K - 1, D + 2 * BCW), F32), p, nbp, seqp)
    ya_s, ssm_s, conv_s = _ssd(proj_a, dt_raw, tp, state_ssm[l].reshape(nbs, D, SSM_N), state_conv[l],
                               p, nbs, seqs)
    yb_p, hg_p = _gla(proj_b, 0, jnp.zeros((nbp, D, HG_DK), F32), p, nbp, seqp)
    yb_s, hg_s = _gla(proj_b, tp, state_hgrn[l].reshape(nbs, D, HG_DK), p, nbs, seqs)

    merged = _gate(ya_p, ya_s, yb_p, yb_s, proj_b, p)
    x1, route = _merge(merged, x_p, x_s, p)

    e_flat = route[:, :2].astype(jnp.int32).reshape(-1)
    n_tiles = (2 * t) // MOE_TM + N_EXPERTS + 1
    tstart, ntile, nv, src, dest = _route_meta(e_flat, MOE_TM, n_tiles)
    es = _experts(x1, tstart, ntile, nv, src, exp_w1[l], exp_w3[l], exp_w2[l], n_tiles, MOE_TM)
    dest_km = dest.reshape(t // ROW_TM, ROW_TM, 2).transpose(2, 0, 1).reshape(-1)
    x2_p, x2_s = _combine(es, dest_km, x1, route, p, tp)

    y_prompt = x2_p.reshape(nbp, seqp, D)
    y_sample = x2_s.reshape(nbs, seqs, D)
    return (y_prompt, y_sample,
            ssm_p.reshape(1, nbp, SSM_HEADS, SSM_P, SSM_N), conv_p[None],
            hg_p.reshape(1, nbp, HG_HEADS, HG_DK, HG_DK),
            ssm_s.reshape(1, nbs, SSM_HEADS, SSM_P, SSM_N), conv_s[None],
            hg_s.reshape(1, nbs, HG_HEADS, HG_DK, HG_DK))
```

```python
import functools

import jax
import jax.numpy as jnp
import numpy as np
from jax import lax
from jax.experimental import pallas as pl
from jax.experimental.pallas import tpu as pltpu

F32 = jnp.float32
BF16 = jnp.bfloat16

D = 2048
SSM_HEADS = 32
SSM_P = 64
SSM_N = 128
SSM_G = 4
GW = D // SSM_G
BCW = SSM_G * SSM_N
CONV_K = 4
HG_HEADS = 16
HG_DK = 128
N_GROUPS = 4
EPG = 8
N_EXPERTS = 32
D_EXPERT = 512
EPS = 1e-5
ALPHA = 2.0 ** 0.25
NEG = -1e30
LOG2E = 1.4426950408889634
LANES = 128
ROW_TILES = D // LANES

VMEM_LIMIT = 56 * 1024 * 1024
SSD_CHUNK = 128
GLA_CHUNK = 64
GLA_SUBCHUNKS = 2
STEP_ROWS = 64
PROJ_TM = 2304
PROJ_TN = 512
MOE_TM = 128
ROW_TM = 256
MERGE_TM = 512
GATE_TM = 1024
MERGE_TN = 512
CARRY = 8


def _sigmoid(x):
    return 0.5 * jnp.tanh(0.5 * x) + 0.5


def _silu(x):
    h = 0.5 * x
    return h * jnp.tanh(h) + h


def _softplus(x):
    return jnp.maximum(x, 0.0) + jnp.log(1.0 + jnp.exp(-jnp.abs(x)))


def _split3(x):
    hi = x.astype(BF16)
    r = x - hi.astype(F32)
    mid = r.astype(BF16)
    lo = (r - mid.astype(F32)).astype(BF16)
    return hi, mid, lo


def _dot3_rhs(m_bf16, x):
    return sum(jnp.dot(m_bf16, part, preferred_element_type=F32) for part in _split3(x))


def _dot3_lhs(x, m_bf16):
    return sum(jnp.dot(part, m_bf16, preferred_element_type=F32) for part in _split3(x))


def _seq_masks(r, seq_len):
    shift = seq_len.bit_length() - 1
    row = lax.broadcasted_iota(jnp.int32, (r, r), 0)
    col = lax.broadcasted_iota(jnp.int32, (r, r), 1)
    same = lax.shift_right_logical(row, shift) == lax.shift_right_logical(col, shift)
    return same & (row >= col), same & (row <= col)


def _seq_totals(cum, nseq, seq_len):
    w = cum.shape[1]
    parts = [jnp.broadcast_to(cum[(b + 1) * seq_len - 1:(b + 1) * seq_len, :], (seq_len, w)) for b in range(nseq)]
    return parts[0] if nseq == 1 else jnp.concatenate(parts, axis=0)


def _pad_rows(x, rows):
    if x.shape[0] == rows:
        return x
    return jnp.concatenate([x, jnp.zeros((rows - x.shape[0], x.shape[1]), x.dtype)], axis=0)


def _to_token_tiles(ref, val):
    rows = val.shape[0]
    for s in range(ROW_TILES):
        ref[pl.ds(s, rows, stride=ROW_TILES), :] = val[:, s * LANES:(s + 1) * LANES]


def _from_token_tiles(ref):
    rows = ref.shape[0] // ROW_TILES
    return jnp.concatenate([ref[pl.ds(s, rows, stride=ROW_TILES), :] for s in range(ROW_TILES)], axis=1)


def _row_mask(x, b, seq_len):
    row = lax.broadcasted_iota(jnp.int32, x.shape, 0)
    return jnp.where((row >= b * seq_len) & (row < (b + 1) * seq_len), x, jnp.zeros_like(x))


def _cast_kernel(xp_ref, xs_ref, o_ref, *, n_prompt):
    o_ref[...] = jnp.where(pl.program_id(0) < n_prompt, xp_ref[...], xs_ref[...]).astype(o_ref.dtype)


def _cast_rows(x_p, x_s, tm):
    k = x_p.shape[1]
    n_p = x_p.shape[0] // tm
    return pl.pallas_call(
        functools.partial(_cast_kernel, n_prompt=n_p),
        grid=((x_p.shape[0] + x_s.shape[0]) // tm,),
        in_specs=[pl.BlockSpec((tm, k), lambda i: (jnp.minimum(i, n_p - 1), 0)),
                  pl.BlockSpec((tm, k), lambda i: (jnp.maximum(i - n_p, 0), 0))],
        out_specs=pl.BlockSpec((tm, k), lambda i: (i, 0)),
        out_shape=jax.ShapeDtypeStruct((x_p.shape[0] + x_s.shape[0], k), BF16),
        compiler_params=pltpu.CompilerParams(dimension_semantics=("parallel",), vmem_limit_bytes=VMEM_LIMIT),
        name="cast",
    )(x_p, x_s)


def _mm_nt_kernel(x_ref, wt_ref, o_ref):
    w = wt_ref[...].astype(BF16)
    o_ref[...] = lax.dot_general(x_ref[...], w, (((1,), (1,)), ((), ())),
                                 preferred_element_type=F32).astype(o_ref.dtype)


def _project(x, wt, row0, n, tm, tn, out_dtype):
    m, k = x.shape
    assert n % tn == 0 and m % tm == 0 and row0 % 8 == 0
    if row0 % tn == 0:
        w_spec = pl.BlockSpec((tn, k), lambda i, j: (row0 // tn + j, 0))
    else:
        w_spec = pl.BlockSpec((pl.Element(tn), pl.Element(k)),
                              lambda i, j: (pl.multiple_of(row0 + j * tn, 8), 0))
    return pl.pallas_call(
        _mm_nt_kernel,
        grid=(m // tm, n // tn),
        in_specs=[pl.BlockSpec((tm, k), lambda i, j: (i, 0)), w_spec],
        out_specs=pl.BlockSpec((tm, tn), lambda i, j: (i, j)),
        out_shape=jax.ShapeDtypeStruct((m, n), out_dtype),
        compiler_params=pltpu.CompilerParams(
            dimension_semantics=("parallel", "arbitrary"), vmem_limit_bytes=VMEM_LIMIT),
        name="proj",
    )(x, wt)


def _conv_silu(buf_ref, carry0_ref, u, w_ref, b_ref, nseq, seq_len, first):
    ch = u.shape[1]

    @pl.when(first)
    def _():
        buf_ref[:, 0:CARRY, :] = carry0_ref[...]
    u3 = u.reshape(nseq, seq_len, ch)
    buf_ref[:, CARRY:CARRY + seq_len, :] = u3
    acc = b_ref[...] + w_ref[CONV_K - 1:CONV_K, :] * u3
    for k in range(CONV_K - 1):
        off = CARRY - (CONV_K - 1) + k
        acc = acc + w_ref[k:k + 1, :] * buf_ref[:, off:off + seq_len, :]
    tail = buf_ref[:, seq_len:seq_len + CARRY, :]
    buf_ref[:, 0:CARRY, :] = tail
    return _silu(acc).reshape(nseq * seq_len, ch), tail


def _pair_cols(tile):
    lane = lax.broadcasted_iota(jnp.int32, tile.shape, 1)
    swapped = pltpu.roll(tile, SSM_P, axis=1)
    return jnp.where(lane < SSM_P, tile, swapped), jnp.where(lane < SSM_P, swapped, tile)


def _ssd_kernel(xp_ref, bcp_ref, z_ref, dtr_ref, dtrt_ref, h0_ref, cx0_ref, cbc0_ref,
                cwx_ref, cwbc_ref, cbx_ref, cbbc_ref, dtb_ref, dtbt_ref, alogt_ref, aloge_ref,
                dskip_ref, normg_ref, e64_ref,
                y_ref, hout_ref, ctx_ref, ctbc_ref,
                xbuf, bcbuf, h_ref, *, nseq, seq_len):
    r = nseq * seq_len
    rp = max(r, LANES)
    zi = pl.program_id(1)
    first = zi == 0

    @pl.when(first)
    def _():
        h_ref[...] = h0_ref[...]

    xs, tail_x = _conv_silu(xbuf, cx0_ref, xp_ref[...].astype(F32), cwx_ref, cbx_ref, nseq, seq_len, first)
    bc, tail_bc = _conv_silu(bcbuf, cbc0_ref, bcp_ref[...].astype(F32), cwbc_ref, cbbc_ref, nseq, seq_len, first)
    ctx_ref[...] = tail_x
    ctbc_ref[...] = tail_bc

    causal, anti = _seq_masks(r, seq_len)
    dt = _softplus(dtr_ref[...] + dtb_ref[...])
    dt_e = _dot3_lhs(dt, e64_ref[...])
    cum = _dot3_rhs(causal.astype(BF16), dt * (-LOG2E * jnp.exp(aloge_ref[...])))
    cum_e = _dot3_lhs(cum, e64_ref[...])
    loga_t = _softplus(dtrt_ref[0] + dtbt_ref[...]) * (-LOG2E * jnp.exp(alogt_ref[...]))
    cum_t = _dot3_lhs(loga_t, anti.astype(BF16))
    tot_e = _seq_totals(cum_e, nseq, seq_len)
    xdt = xs * dt_e
    in_scale = jnp.exp2(cum_e)
    xw = xdt * jnp.exp2(tot_e - cum_e)
    xw_t = _pad_rows(xw, rp).T.astype(BF16)
    lane = lax.broadcasted_iota(jnp.int32, (r, LANES), 1)

    y_groups = []
    for g in range(SSM_G):
        bg = bc[:, g * SSM_N:(g + 1) * SSM_N]
        cg = bc[:, BCW + g * SSM_N:BCW + (g + 1) * SSM_N]
        cb = lax.dot_general(cg, bg, (((1,), (1,)), ((), ())), preferred_element_type=F32)
        rows = slice(g * GW, (g + 1) * GW)
        pieces = []
        for j in range(GW // LANES):
            col0 = g * GW + j * LANES
            xpair = xdt[:, col0:col0 + LANES]
            ms, rhs = [], []
            for half, colb in enumerate(_pair_cols(cum_e[:, col0:col0 + LANES])):
                head = col0 // SSM_P + half
                seg = colb[:, :r] - cum_t[head:head + 1, :]
                ms.append((cb * jnp.exp2(jnp.where(causal, seg, NEG))).astype(BF16))
                rhs.append(jnp.where(lane >= SSM_P if half else lane < SSM_P, xpair, 0.0).astype(BF16))
            if r % LANES == 0:
                pieces.append(jnp.dot(jnp.concatenate(ms, axis=1), jnp.concatenate(rhs, axis=0),
                                      preferred_element_type=F32))
            else:
                pieces.append(jnp.dot(ms[0], rhs[0], preferred_element_type=F32)
                              + jnp.dot(ms[1], rhs[1], preferred_element_type=F32))
        y_intra = jnp.concatenate(pieces, axis=1)
        y_inter = []
        for b in range(nseq):
            rb = slice(b * seq_len, (b + 1) * seq_len)
            hg = h_ref[b, rows, :]
            y_inter.append(lax.dot_general(cg[rb, :], hg, (((1,), (1,)), ((), ())),
                                           preferred_element_type=F32))
            bmask = bg if nseq == 1 else _row_mask(bg, b, seq_len)
            st = jnp.dot(xw_t[rows, :], _pad_rows(bmask, rp).astype(BF16), preferred_element_type=F32)
            dec8 = jnp.exp2(tot_e[b * seq_len:b * seq_len + 8, rows])
            for j in range(GW // LANES):
                for half, dcol in enumerate(_pair_cols(dec8[:, j * LANES:(j + 1) * LANES])):
                    h8 = 2 * j + half
                    hr = slice(g * GW + h8 * SSM_P, g * GW + (h8 + 1) * SSM_P)
                    h_ref[b, hr, :] = dcol[0:1, :] * h_ref[b, hr, :] + st[h8 * SSM_P:(h8 + 1) * SSM_P, :]
        y_inter = y_inter[0] if nseq == 1 else jnp.concatenate(y_inter, axis=0)
        y_groups.append(y_intra + y_inter * in_scale[:, rows])
    y = jnp.concatenate(y_groups, axis=1) + dskip_ref[...] * xs
    yz = y * _silu(z_ref[...].astype(F32))
    outs = []
    for g in range(SSM_G):
        blk = yz[:, g * GW:(g + 1) * GW]
        ms = jnp.sum(blk * blk, axis=1, keepdims=True) * (1.0 / GW)
        outs.append(blk * lax.rsqrt(ms + EPS))
    y_ref[...] = (jnp.concatenate(outs, axis=1) * normg_ref[...]).astype(y_ref.dtype)

    @pl.when(zi == pl.num_programs(1) - 1)
    def _():
        hout_ref[...] = h_ref[...]


def _ssd(proj, dt_raw, row0, h0, conv0, p, nb, seq):
    seq_len = SSD_CHUNK if seq % SSD_CHUNK == 0 else seq
    nseq = 1 if seq_len == SSD_CHUNK else max(1, min(nb, STEP_ROWS // seq_len))
    nz = seq // seq_len
    r = nseq * seq_len
    t = nb * seq
    r0 = row0 // r
    nblk = t // r
    dtrt = dt_raw[row0:row0 + t, :SSM_HEADS].reshape(nblk, r, SSM_HEADS).transpose(0, 2, 1)
    pad_rows = CARRY - (CONV_K - 1)
    cx0 = jnp.pad(conv0[:, :, :D], ((0, 0), (pad_rows, 0), (0, 0)))
    cbc0 = jnp.pad(conv0[:, :, D:], ((0, 0), (pad_rows, 0), (0, 0)))
    rowblk = lambda col: pl.BlockSpec((r, D), lambda b, z: (r0 + b * nz + z, col))
    const = lambda shape: pl.BlockSpec(shape, lambda b, z: tuple(0 for _ in shape))
    per_b = lambda shape: pl.BlockSpec((nseq,) + shape, lambda b, z: (b, 0, 0))
    y, hout, ctx, ctbc = pl.pallas_call(
        functools.partial(_ssd_kernel, nseq=nseq, seq_len=seq_len),
        grid=(nb // nseq, nz),
        in_specs=[rowblk(1),
                  pl.BlockSpec((r, 2 * BCW), lambda b, z: (r0 + b * nz + z, 4)),
                  rowblk(0),
                  pl.BlockSpec((r, LANES), lambda b, z: (r0 + b * nz + z, 0)),
                  pl.BlockSpec((1, SSM_HEADS, r), lambda b, z: (b * nz + z, 0, 0)),
                  per_b((D, SSM_N)), per_b((CARRY, D)), per_b((CARRY, 2 * BCW)),
                  const((CONV_K, D)), const((CONV_K, 2 * BCW)), const((1, D)), const((1, 2 * BCW)),
                  const((1, LANES)), const((SSM_HEADS, r)), const((SSM_HEADS, r)), const((1, LANES)),
                  const((1, D)), const((1, D)), const((LANES, D))],
        out_specs=[pl.BlockSpec((r, D), lambda b, z: (b * nz + z, 0)),
                   per_b((D, SSM_N)), per_b((CARRY, D)), per_b((CARRY, 2 * BCW))],
        out_shape=[jax.ShapeDtypeStruct((t, D), BF16),
                   jax.ShapeDtypeStruct((nb, D, SSM_N), F32),
                   jax.ShapeDtypeStruct((nb, CARRY, D), F32),
                   jax.ShapeDtypeStruct((nb, CARRY, 2 * BCW), F32)],
        scratch_shapes=[pltpu.VMEM((nseq, seq_len + CARRY, D), F32),
                        pltpu.VMEM((nseq, seq_len + CARRY, 2 * BCW), F32),
                        pltpu.VMEM((nseq, D, SSM_N), F32)],
        compiler_params=pltpu.CompilerParams(
            dimension_semantics=("parallel", "arbitrary"), vmem_limit_bytes=VMEM_LIMIT),
        name="ssd",
    )(proj, proj, proj, dt_raw, dtrt, h0, cx0, cbc0,
      p["cwx"], p["cwbc"], p["cbx"], p["cbbc"], p["dtb"],
      jnp.broadcast_to(p["dtb"][0, :SSM_HEADS, None], (SSM_HEADS, r)),
      jnp.broadcast_to(p["alog"][:, None], (SSM_HEADS, r)),
      p["alog_e"], p["dskip_e"], p["ssm_norm_g"], p["e64"])
    conv_new = jnp.concatenate([ctx[:, pad_rows:], ctbc[:, pad_rows:]], axis=-1)
    return y, hout, conv_new


def _gla_kernel(q_ref, f_ref, i_ref, g_ref, s0_ref, lb_ref, normg_ref, y_ref, sout_ref, s_ref,
                *, nseq, seq_len, nsub):
    r = nseq * seq_len
    rp = max(r, LANES)
    zi = pl.program_id(1)

    @pl.when(zi == 0)
    def _():
        s_ref[...] = s0_ref[...]

    lb = lb_ref[...]
    causal, _ = _seq_masks(r, seq_len)
    for sub in range(nsub):
        rows = slice(sub * r, (sub + 1) * r)
        f = lb + (1.0 - lb) * _sigmoid(f_ref[rows, :].astype(F32))
        k = 1.0 - f
        q = _silu(q_ref[rows, :].astype(F32))
        vb = i_ref[rows, :]
        bc = _dot3_rhs(causal.astype(BF16), jnp.log2(f))
        tot = _seq_totals(bc, nseq, seq_len)
        qe = q * jnp.exp2(bc)
        ke = k * jnp.exp2(-bc)
        kd_t = _pad_rows(ke * jnp.exp2(tot), rp).T.astype(BF16)
        gate = _silu(g_ref[rows, :].astype(F32))
        outs = []
        for h in range(HG_HEADS):
            sl = slice(h * HG_DK, (h + 1) * HG_DK)
            qh = qe[:, sl]
            attn = lax.dot_general(qh, ke[:, sl], (((1,), (1,)), ((), ())), preferred_element_type=F32)
            attn = jnp.where(causal, attn, 0.0)
            o = jnp.dot(attn.astype(BF16), vb[:, sl], preferred_element_type=F32)
            o_inter = []
            for b in range(nseq):
                rb = slice(b * seq_len, (b + 1) * seq_len)
                sh = s_ref[b, sl, :]
                o_inter.append(jnp.dot(qh[rb, :], sh, preferred_element_type=F32))
                vmask = vb[:, sl] if nseq == 1 else _row_mask(vb[:, sl], b, seq_len)
                st = jnp.dot(kd_t[sl, :], _pad_rows(vmask, rp), preferred_element_type=F32)
                dec = jnp.exp2(tot[b * seq_len:b * seq_len + 1, sl])
                dec_col = jnp.broadcast_to(dec, (HG_DK, HG_DK)).T
                s_ref[b, sl, :] = dec_col * sh + st
            o = o + (o_inter[0] if nseq == 1 else jnp.concatenate(o_inter, axis=0))
            ms = jnp.sum(o * o, axis=1, keepdims=True) * (1.0 / HG_DK)
            outs.append(o * lax.rsqrt(ms + EPS))
        y_ref[rows, :] = (jnp.concatenate(outs, axis=1) * normg_ref[...] * gate).astype(y_ref.dtype)

    @pl.when(zi == pl.num_programs(1) - 1)
    def _():
        sout_ref[...] = s_ref[...]


def _gla(proj, row0, s0, p, nb, seq):
    seq_len = GLA_CHUNK if seq % GLA_CHUNK == 0 else seq
    nseq = 1 if seq_len == GLA_CHUNK else max(1, min(nb, STEP_ROWS // seq_len))
    nsub = GLA_SUBCHUNKS if nseq == 1 and (seq // seq_len) % GLA_SUBCHUNKS == 0 else 1
    nz = seq // (seq_len * nsub)
    r = nseq * seq_len * nsub
    r0 = row0 // r
    rowblk = lambda col: pl.BlockSpec((r, D), lambda b, z: (r0 + b * nz + z, col))
    const = lambda shape: pl.BlockSpec(shape, lambda b, z: tuple(0 for _ in shape))
    per_b = lambda shape: pl.BlockSpec((nseq,) + shape, lambda b, z: (b, 0, 0))
    return pl.pallas_call(
        functools.partial(_gla_kernel, nseq=nseq, seq_len=seq_len, nsub=nsub),
        grid=(nb // nseq, nz),
        in_specs=[rowblk(0), rowblk(1), rowblk(2), rowblk(3), per_b((D, HG_DK)), const((1, D)), const((1, D))],
        out_specs=[pl.BlockSpec((r, D), lambda b, z: (b * nz + z, 0)), per_b((D, HG_DK))],
        out_shape=[jax.ShapeDtypeStruct((nb * seq, D), BF16), jax.ShapeDtypeStruct((nb, D, HG_DK), F32)],
        scratch_shapes=[pltpu.VMEM((nseq, D, HG_DK), F32)],
        compiler_params=pltpu.CompilerParams(
            dimension_semantics=("parallel", "arbitrary"), vmem_limit_bytes=VMEM_LIMIT),
        name="gla",
    )(proj, proj, proj, proj, s0, p["lb"], p["hgrn_norm_g"])


def _layer_norm(x, g, b):
    mu = jnp.mean(x, axis=1, keepdims=True)
    xc = x - mu
    var = jnp.mean(xc * xc, axis=1, keepdims=True)
    return xc * lax.rsqrt(var + EPS) * g + b


def _gate_kernel(yap_ref, yas_ref, ybp_ref, ybs_ref, ga_ref, gb_ref, wa_ref, wb_ref, o_ref, *, n_prompt):
    is_prompt = pl.program_id(0) < n_prompt
    ya = jnp.where(is_prompt, yap_ref[...], yas_ref[...])
    yb = jnp.where(is_prompt, ybp_ref[...], ybs_ref[...])
    a = jnp.dot(ya, wa_ref[...], preferred_element_type=F32)
    b = jnp.dot(yb, wb_ref[...], preferred_element_type=F32)
    ga = _sigmoid(ga_ref[...].astype(F32))
    gb = _sigmoid(gb_ref[...].astype(F32))
    o_ref[...] = (ga * a + gb * b).astype(o_ref.dtype)


def _gate(ya_p, ya_s, yb_p, yb_s, proj, p):
    tm = GATE_TM if ya_p.shape[0] % GATE_TM == 0 and ya_s.shape[0] % GATE_TM == 0 else MERGE_TM
    tn = MERGE_TN
    n_p = ya_p.shape[0] // tm
    t = ya_p.shape[0] + ya_s.shape[0]
    prow = pl.BlockSpec((tm, D), lambda i, j: (jnp.minimum(i, n_p - 1), 0))
    srow = pl.BlockSpec((tm, D), lambda i, j: (jnp.maximum(i - n_p, 0), 0))
    gcol = lambda seg: pl.BlockSpec((tm, tn), lambda i, j: (i, seg * (D // tn) + j))
    wcol = pl.BlockSpec((D, tn), lambda i, j: (0, j))
    return pl.pallas_call(
        functools.partial(_gate_kernel, n_prompt=n_p),
        grid=(t // tm, D // tn),
        in_specs=[prow, srow, prow, srow, gcol(4), gcol(5), wcol, wcol],
        out_specs=pl.BlockSpec((tm, tn), lambda i, j: (i, j)),
        out_shape=jax.ShapeDtypeStruct((t, D), BF16),
        compiler_params=pltpu.CompilerParams(
            dimension_semantics=("parallel", "arbitrary"), vmem_limit_bytes=VMEM_LIMIT),
        name="gate",
    )(ya_p, ya_s, yb_p, yb_s, proj, proj, p["w_a"], p["w_b"])


def _merge_kernel(m_ref, xp_ref, xs_ref, wo_ref, g1_ref, b1_ref, wrh_ref, wrl_ref, br_ref,
                  x1_ref, route_ref, *, n_prompt):
    x = jnp.where(pl.program_id(0) < n_prompt, xp_ref[...], xs_ref[...])
    mix = jnp.dot(m_ref[...], wo_ref[...], preferred_element_type=F32)
    x1 = _layer_norm(ALPHA * x + mix, g1_ref[...], b1_ref[...])
    _to_token_tiles(x1_ref, x1)
    hi = x1.astype(BF16)
    lo = (x1 - hi.astype(F32)).astype(BF16)
    logits = (jnp.dot(hi, wrh_ref[...], preferred_element_type=F32)
              + jnp.dot(lo, wrh_ref[...], preferred_element_type=F32)
              + jnp.dot(hi, wrl_ref[...], preferred_element_type=F32) + br_ref[...])
    lane = lax.broadcasted_iota(jnp.int32, logits.shape, 1).astype(F32)
    gl = jnp.where(lane < N_GROUPS, logits, NEG)
    gmax = jnp.max(gl, axis=1, keepdims=True)
    gsel = jnp.min(jnp.where(gl == gmax, lane, 1e9), axis=1, keepdims=True)
    p_grp = 1.0 / jnp.sum(jnp.exp(gl - gmax), axis=1, keepdims=True)
    lo = N_GROUPS + gsel * EPG
    el = jnp.where((lane >= lo) & (lane < lo + EPG), logits, NEG)
    v0 = jnp.max(el, axis=1, keepdims=True)
    i0 = jnp.min(jnp.where(el == v0, lane, 1e9), axis=1, keepdims=True)
    el2 = jnp.where(lane == i0, NEG, el)
    v1 = jnp.max(el2, axis=1, keepdims=True)
    i1 = jnp.min(jnp.where(el2 == v1, lane, 1e9), axis=1, keepdims=True)
    e1 = jnp.exp(v1 - v0)
    p0 = 1.0 / (1.0 + e1)
    p1 = e1 * p0
    route = jnp.where(lane == 0, i0 - N_GROUPS,
                      jnp.where(lane == 1, i1 - N_GROUPS,
                                jnp.where(lane == 2, p_grp * p0,
                                          jnp.where(lane == 3, p_grp * p1, 0.0))))
    route_ref[...] = route


def _merge(merged, x_p, x_s, p):
    tm = MERGE_TM
    n_p = x_p.shape[0] // tm
    t = x_p.shape[0] + x_s.shape[0]
    row = pl.BlockSpec((tm, D), lambda i: (i, 0))
    prow = pl.BlockSpec((tm, D), lambda i: (jnp.minimum(i, n_p - 1), 0))
    srow = pl.BlockSpec((tm, D), lambda i: (jnp.maximum(i - n_p, 0), 0))
    res = lambda shape: pl.BlockSpec(shape, lambda i: (0, 0), pipeline_mode=pl.Buffered(1))
    wr_hi = p["wr"].astype(BF16)
    wr_lo = (p["wr"] - wr_hi.astype(F32)).astype(BF16)
    return pl.pallas_call(
        functools.partial(_merge_kernel, n_prompt=n_p),
        grid=(t // tm,),
        in_specs=[row, prow, srow, res((D, D)), res((1, D)), res((1, D)),
                  res((D, LANES)), res((D, LANES)), res((1, LANES))],
        out_specs=[pl.BlockSpec((tm * ROW_TILES, LANES), lambda i: (i, 0)),
                   pl.BlockSpec((tm, LANES), lambda i: (i, 0))],
        out_shape=[jax.ShapeDtypeStruct((t * ROW_TILES, LANES), F32), jax.ShapeDtypeStruct((t, LANES), F32)],
        compiler_params=pltpu.CompilerParams(
            dimension_semantics=("parallel",), vmem_limit_bytes=VMEM_LIMIT),
        name="merge",
    )(merged, x_p, x_s, p["w_out"], p["ln1_g"], p["ln1_b"], wr_hi, wr_lo, p["br"])


def _row_gather(src_hbm, dst, sem, idx_ref, base, n, unroll=8):
    def body(j, carry):
        r = idx_ref[base + j]
        pltpu.make_async_copy(src_hbm.at[pl.ds(pl.multiple_of(r * ROW_TILES, ROW_TILES), ROW_TILES), :],
                              dst.at[pl.ds(j * ROW_TILES, ROW_TILES), :], sem).start()
        return carry
    lax.fori_loop(0, n, body, 0, unroll=unroll)


def _expert_kernel(tstart_ref, ntile_ref, nv_ref, src_ref, x1_hbm, w1_hbm, w3_hbm, w2_hbm, es_hbm,
                   xbuf, obuf, gsem, osem, w1f, w3f, w2f, wsem, w1b, w3b, w2b, *, tm, n_tiles):
    e = pl.program_id(0)
    n_e = ntile_ref[e]
    g0 = tstart_ref[e]
    nv = nv_ref[0]
    rows = tm * ROW_TILES

    def gather_wait(s):
        pltpu.make_async_copy(x1_hbm.at[pl.ds(0, rows), :], xbuf.at[s], gsem.at[s]).wait()

    def out_copy(g):
        return pltpu.make_async_copy(obuf, es_hbm.at[pl.ds(pl.multiple_of(g * rows, rows), rows), :], osem)

    def weight_copies(ex, s):
        return [pltpu.make_async_copy(w_hbm.at[ex], w_f.at[s], wsem.at[s])
                for w_hbm, w_f in ((w1_hbm, w1f), (w3_hbm, w3f), (w2_hbm, w2f))]

    wslot = lax.rem(e, 2)

    @pl.when(e == 0)
    def _():
        _row_gather(x1_hbm, xbuf.at[0], gsem.at[0], src_ref, 0, tm)
        for c in weight_copies(0, 0):
            c.start(priority=1)

    @pl.when(e + 1 < pl.num_programs(0))
    def _():
        for c in weight_copies(e + 1, 1 - wslot):
            c.start(priority=1)

    for c in weight_copies(e, wslot):
        c.wait()

    @pl.when(n_e > 0)
    def _():
        w1b[...] = w1f[wslot].astype(BF16)
        w3b[...] = w3f[wslot].astype(BF16)
        w2b[...] = w2f[wslot].astype(BF16)

    def tile(k, carry):
        g = g0 + k
        slot = lax.rem(g, 2)
        _row_gather(x1_hbm, xbuf.at[1 - slot], gsem.at[1 - slot], src_ref, (g + 1) * tm, tm, unroll=True)
        gather_wait(slot)
        xb = _from_token_tiles(xbuf.at[slot]).astype(BF16)
        h1 = jnp.dot(xb, w1b[...], preferred_element_type=F32)
        h3 = jnp.dot(xb, w3b[...], preferred_element_type=F32)
        h = (_silu(h1) * h3).astype(BF16)
        out = jnp.dot(h, w2b[...], preferred_element_type=F32)

        @pl.when(g > 0)
        def _():
            out_copy(g).wait()
        _to_token_tiles(obuf, out)
        out_copy(g).start()
        return carry

    lax.fori_loop(0, n_e, tile, 0)

    @pl.when(e == pl.num_programs(0) - 1)
    def _():
        gather_wait(lax.rem(nv, 2))
        out_copy(0).wait()
        obuf[...] = jnp.zeros_like(obuf)

        def zero_tile(g, carry):
            out_copy(g).start()
            out_copy(g).wait()
            return carry

        lax.fori_loop(nv, n_tiles, zero_tile, 0)


def _experts(x1, tstart, ntile, nv, src, w1, w3, w2, n_tiles, tm):
    any_spec = pl.BlockSpec(memory_space=pl.ANY)
    return pl.pallas_call(
        functools.partial(_expert_kernel, tm=tm, n_tiles=n_tiles),
        grid_spec=pltpu.PrefetchScalarGridSpec(
            num_scalar_prefetch=4,
            grid=(N_EXPERTS,),
            in_specs=[any_spec, any_spec, any_spec, any_spec],
            out_specs=any_spec,
            scratch_shapes=[pltpu.VMEM((2, tm * ROW_TILES, LANES), F32), pltpu.VMEM((tm * ROW_TILES, LANES), F32),
                            pltpu.SemaphoreType.DMA((2,)), pltpu.SemaphoreType.DMA(()),
                            pltpu.VMEM((2, D, D_EXPERT), F32), pltpu.VMEM((2, D, D_EXPERT), F32),
                            pltpu.VMEM((2, D_EXPERT, D), F32), pltpu.SemaphoreType.DMA((2,)),
                            pltpu.VMEM((D, D_EXPERT), BF16), pltpu.VMEM((D, D_EXPERT), BF16),
                            pltpu.VMEM((D_EXPERT, D), BF16)]),
        out_shape=jax.ShapeDtypeStruct((n_tiles * tm * ROW_TILES, LANES), F32),
        compiler_params=pltpu.CompilerParams(
            dimension_semantics=("arbitrary",), vmem_limit_bytes=VMEM_LIMIT, disable_bounds_checks=True),
        name="experts",
    )(tstart, ntile, nv, src, x1, w1, w3, w2)


def _combine_kernel(dest_ref, es_hbm, x1_ref, route_ref, g2_ref, b2_ref, op_ref, os_ref, gbuf, sems,
                    *, tm, n_prompt):
    i = pl.program_id(0)
    n = pl.num_programs(0)
    slot = lax.rem(i, 2)

    def start(step, s):
        for k in range(2):
            _row_gather(es_hbm, gbuf.at[s, k], sems.at[s], dest_ref, (k * n + step) * tm, tm)

    @pl.when(i == 0)
    def _():
        start(0, 0)

    @pl.when(i + 1 < n)
    def _():
        start(i + 1, 1 - slot)

    for k in range(2):
        pltpu.make_async_copy(es_hbm.at[pl.ds(0, tm * ROW_TILES), :], gbuf.at[slot, k], sems.at[slot]).wait()
    route = route_ref[...]
    w0 = route[:, 2:3]
    w1 = route[:, 3:4]
    moe = w0 * _from_token_tiles(gbuf.at[slot, 0]) + w1 * _from_token_tiles(gbuf.at[slot, 1])
    out = _layer_norm(ALPHA * _from_token_tiles(x1_ref) + moe, g2_ref[...], b2_ref[...])

    @pl.when(i < n_prompt)
    def _():
        op_ref[...] = out

    @pl.when(i >= n_prompt)
    def _():
        os_ref[...] = out


def _combine(es, dest_km, x1, route, p, tp):
    t = x1.shape[0] // ROW_TILES
    tm = ROW_TM
    n_p = tp // tm
    return pl.pallas_call(
        functools.partial(_combine_kernel, tm=tm, n_prompt=n_p),
        grid_spec=pltpu.PrefetchScalarGridSpec(
            num_scalar_prefetch=1,
            grid=(t // tm,),
            in_specs=[pl.BlockSpec(memory_space=pl.ANY),
                      pl.BlockSpec((tm * ROW_TILES, LANES), lambda i, d: (i, 0)),
                      pl.BlockSpec((tm, LANES), lambda i, d: (i, 0)),
                      pl.BlockSpec((1, D), lambda i, d: (0, 0)),
                      pl.BlockSpec((1, D), lambda i, d: (0, 0))],
            out_specs=[pl.BlockSpec((tm, D), lambda i, d: (jnp.minimum(i, n_p - 1), 0)),
                       pl.BlockSpec((tm, D), lambda i, d: (jnp.maximum(i - n_p, 0), 0))],
            scratch_shapes=[pltpu.VMEM((2, 2, tm * ROW_TILES, LANES), F32), pltpu.SemaphoreType.DMA((2,))]),
        out_shape=[jax.ShapeDtypeStruct((tp, D), F32), jax.ShapeDtypeStruct((t - tp, D), F32)],
        compiler_params=pltpu.CompilerParams(
            dimension_semantics=("arbitrary",), vmem_limit_bytes=VMEM_LIMIT, disable_bounds_checks=True),
        name="combine",
    )(dest_km, es, x1, route, p["ln2_g"], p["ln2_b"])


def _route_meta(e_flat, tm, n_tiles):
    a = e_flat.shape[0]
    ids = jnp.arange(N_EXPERTS, dtype=jnp.int32)
    onehot = (e_flat[:, None] == ids[None, :]).astype(jnp.int32)
    csum = jnp.cumsum(onehot, axis=0)
    rank = jnp.sum((csum - onehot) * onehot, axis=1)
    counts = csum[-1]
    tiles_per = (counts + tm - 1) // tm
    tile_end = jnp.cumsum(tiles_per)
    tile_start = tile_end - tiles_per
    nv = tile_end[-1]
    dest = tile_start[e_flat] * tm + rank
    src = jnp.zeros((n_tiles * tm,), jnp.int32).at[dest].set(jnp.arange(a, dtype=jnp.int32) // 2)
    return (tile_start.astype(jnp.int32), tiles_per.astype(jnp.int32), nv.reshape(1).astype(jnp.int32), src,
            dest.astype(jnp.int32))


def kernel(x_prompt, x_sample, state_ssm, state_conv, state_hgrn, w_in, conv_w, conv_b, dt_bias, a_log, d_skip, ssm_norm_g, hgrn_lb_logits, hgrn_norm_g, w_a, w_b, w_out, ln1_g, ln1_b, router_g_w, router_g_b, router_e_w, router_e_b, exp_w1, exp_w3, exp_w2, ln2_g, ln2_b):
    nbp, seqp, _ = x_prompt.shape
    nbs, seqs, _ = x_sample.shape
    tp, ts = nbp * seqp, nbs * seqs
    t = tp + ts
    l = 0
    x_p, x_s = x_prompt.reshape(tp, D), x_sample.reshape(ts, D)

    wt = jnp.transpose(w_in[l])
    o_dt = D + D + 2 * BCW
    o_q = o_dt + SSM_HEADS
    xb = _cast_rows(x_p, x_s, MERGE_TM)
    tm = PROJ_TM if t % PROJ_TM == 0 else 1024
    proj_a = _project(xb, wt, 0, o_dt, tm, PROJ_TN, BF16)
    dt_raw = jnp.pad(_project(xb, wt, o_dt, SSM_HEADS, tm, SSM_HEADS, F32), ((0, 0), (0, LANES - SSM_HEADS)))
    proj_b = _project(xb, wt, o_q, 6 * D, tm, PROJ_TN, BF16)

    lb_all = jnp.cumsum(jax.nn.softmax(hgrn_lb_logits.astype(F32), axis=0), axis=0)
    head_of = np.arange(D) // SSM_P
    pad128 = lambda v: jnp.pad(v, (0, LANES - v.shape[0])).reshape(1, LANES)
    p = {
        "cwx": conv_w[l][:, :D], "cwbc": conv_w[l][:, D:],
        "cbx": conv_b[l][:D].reshape(1, D), "cbbc": conv_b[l][D:].reshape(1, 2 * BCW),
        "dtb": pad128(dt_bias[l]), "alog": a_log[l], "alog_e": pad128(a_log[l]),
        "dskip_e": jnp.repeat(d_skip[l], SSM_P).reshape(1, D),
        "ssm_norm_g": ssm_norm_g[l].reshape(1, D),
        "e64": jnp.asarray(np.arange(LANES)[:, None] == head_of[None, :], dtype=BF16),
        "lb": lb_all[l].reshape(1, D), "hgrn_norm_g": hgrn_norm_g[l].reshape(1, D),
        "w_a": w_a[l].astype(BF16), "w_b": w_b[l].astype(BF16), "w_out": w_out[l].astype(BF16),
        "ln1_g": ln1_g[l].reshape(1, D), "ln1_b": ln1_b[l].reshape(1, D),
        "ln2_g": ln2_g[l].reshape(1, D), "ln2_b": ln2_b[l].reshape(1, D),
        "wr": jnp.pad(jnp.concatenate([router_g_w[l], router_e_w[l]], axis=1),
                      ((0, 0), (0, LANES - N_GROUPS - N_EXPERTS))),
        "br": pad128(jnp.concatenate([router_g_b[l], router_e_b[l]])),
    }

    ya_p, ssm_p, conv_p = _ssd(proj_a, dt_raw, 0, jnp.zeros((nbp, D, SSM_N), F32),
                               jnp.zeros((nbp, CONV_K - 1, D + 2 * BCW), F32), p, nbp, seqp)
    ya_s, ssm_s, conv_s = _ssd(proj_a, dt_raw, tp, state_ssm[l].reshape(nbs, D, SSM_N), state_conv[l],
                               p, nbs, seqs)
    yb_p, hg_p = _gla(proj_b, 0, jnp.zeros((nbp, D, HG_DK), F32), p, nbp, seqp)
    yb_s, hg_s = _gla(proj_b, tp, state_hgrn[l].reshape(nbs, D, HG_DK), p, nbs, seqs)

    merged = _gate(ya_p, ya_s, yb_p, yb_s, proj_b, p)
    x1, route = _merge(merged, x_p, x_s, p)

    e_flat = route[:, :2].astype(jnp.int32).reshape(-1)
    n_tiles = (2 * t) // MOE_TM + N_EXPERTS + 1
    tstart, ntile, nv, src, dest = _route_meta(e_flat, MOE_TM, n_tiles)
    es = _experts(x1, tstart, ntile, nv, src, exp_w1[l], exp_w3[l], exp_w2[l], n_tiles, MOE_TM)
    dest_km = dest.reshape(t // ROW_TM, ROW_TM, 2).transpose(2, 0, 1).reshape(-1)
    x2_p, x2_s = _combine(es, dest_km, x1, route, p, tp)

    y_prompt = x2_p.reshape(nbp, seqp, D)
    y_sample = x2_s.reshape(nbs, seqs, D)
    return (y_prompt, y_sample,
            ssm_p.reshape(1, nbp, SSM_HEADS, SSM_P, SSM_N), conv_p[None],
            hg_p.reshape(1, nbp, HG_HEADS, HG_DK, HG_DK),
            ssm_s.reshape(1, nbs, SSM_HEADS, SSM_P, SSM_N), conv_s[None],
            hg_s.reshape(1, nbs, HG_HEADS, HG_DK, HG_DK))
```

```python
import functools

import jax
import jax.numpy as jnp
import numpy as np
from jax import lax
from jax.experimental import pallas as pl
from jax.experimental.pallas import tpu as pltpu

F32 = jnp.float32
BF16 = jnp.bfloat16

D = 2048
SSM_HEADS = 32
SSM_P = 64
SSM_N = 128
SSM_G = 4
GW = D // SSM_G
BCW = SSM_G * SSM_N
CONV_K = 4
HG_HEADS = 16
HG_DK = 128
N_GROUPS = 4
EPG = 8
N_EXPERTS = 32
D_EXPERT = 512
EPS = 1e-5
ALPHA = 2.0 ** 0.25
NEG = -1e30
LOG2E = 1.4426950408889634
LANES = 128
ROW_TILES = D // LANES

VMEM_LIMIT = 56 * 1024 * 1024
SSD_CHUNK = 128
GLA_CHUNK = 64
GLA_SUBCHUNKS = 2
STEP_ROWS = 64
PROJ_TM = 2304
PROJ_TN = 512
MOE_TM = 128
ROW_TM = 256
MERGE_TM = 512
GATE_TM = 1024
MERGE_TN = 512
CARRY = 8


def _sigmoid(x):
    return 0.5 * jnp.tanh(0.5 * x) + 0.5


def _silu(x):
    h = 0.5 * x
    return h * jnp.tanh(h) + h


def _softplus(x):
    return jnp.maximum(x, 0.0) + jnp.log(1.0 + jnp.exp(-jnp.abs(x)))


def _split3(x):
    hi = x.astype(BF16)
    r = x - hi.astype(F32)
    mid = r.astype(BF16)
    lo = (r - mid.astype(F32)).astype(BF16)
    return hi, mid, lo


def _dot3_rhs(m_bf16, x):
    return sum(jnp.dot(m_bf16, part, preferred_element_type=F32) for part in _split3(x))


def _dot3_lhs(x, m_bf16):
    return sum(jnp.dot(part, m_bf16, preferred_element_type=F32) for part in _split3(x))


def _seq_masks(r, seq_len):
    shift = seq_len.bit_length() - 1
    row = lax.broadcasted_iota(jnp.int32, (r, r), 0)
    col = lax.broadcasted_iota(jnp.int32, (r, r), 1)
    same = lax.shift_right_logical(row, shift) == lax.shift_right_logical(col, shift)
    return same & (row >= col), same & (row <= col)


def _seq_totals(cum, nseq, seq_len):
    w = cum.shape[1]
    parts = [jnp.broadcast_to(cum[(b + 1) * seq_len - 1:(b + 1) * seq_len, :], (seq_len, w)) for b in range(nseq)]
    return parts[0] if nseq == 1 else jnp.concatenate(parts, axis=0)


def _pad_rows(x, rows):
    if x.shape[0] == rows:
        return x
    return jnp.concatenate([x, jnp.zeros((rows - x.shape[0], x.shape[1]), x.dtype)], axis=0)


def _to_token_tiles(ref, val):
    rows = val.shape[0]
    for s in range(ROW_TILES):
        ref[pl.ds(s, rows, stride=ROW_TILES), :] = val[:, s * LANES:(s + 1) * LANES]


def _from_token_tiles(ref):
    rows = ref.shape[0] // ROW_TILES
    return jnp.concatenate([ref[pl.ds(s, rows, stride=ROW_TILES), :] for s in range(ROW_TILES)], axis=1)


def _row_mask(x, b, seq_len):
    row = lax.broadcasted_iota(jnp.int32, x.shape, 0)
    return jnp.where((row >= b * seq_len) & (row < (b + 1) * seq_len), x, jnp.zeros_like(x))


def _cast_kernel(xp_ref, xs_ref, wdt_ref, o_ref, dt_ref, *, n_prompt):
    xb = jnp.where(pl.program_id(0) < n_prompt, xp_ref[...], xs_ref[...]).astype(BF16)
    o_ref[...] = xb
    dt_ref[...] = lax.dot_general(xb, wdt_ref[...].astype(BF16), (((1,), (1,)), ((), ())),
                                  preferred_element_type=F32)


def _cast_rows(x_p, x_s, wt, dt_row0, tm):
    k = x_p.shape[1]
    n_p = x_p.shape[0] // tm
    m = x_p.shape[0] + x_s.shape[0]
    return pl.pallas_call(
        functools.partial(_cast_kernel, n_prompt=n_p),
        grid=(m // tm,),
        in_specs=[pl.BlockSpec((tm, k), lambda i: (jnp.minimum(i, n_p - 1), 0)),
                  pl.BlockSpec((tm, k), lambda i: (jnp.maximum(i - n_p, 0), 0)),
                  pl.BlockSpec((SSM_HEADS, k), lambda i: (dt_row0 // SSM_HEADS, 0))],
        out_specs=[pl.BlockSpec((tm, k), lambda i: (i, 0)), pl.BlockSpec((tm, SSM_HEADS), lambda i: (i, 0))],
        out_shape=[jax.ShapeDtypeStruct((m, k), BF16), jax.ShapeDtypeStruct((m, SSM_HEADS), F32)],
        compiler_params=pltpu.CompilerParams(dimension_semantics=("parallel",), vmem_limit_bytes=VMEM_LIMIT),
        name="cast",
    )(x_p, x_s, wt)


def _mm_nt_kernel(x_ref, wt_ref, o_ref):
    w = wt_ref[...].astype(BF16)
    o_ref[...] = lax.dot_general(x_ref[...], w, (((1,), (1,)), ((), ())),
                                 preferred_element_type=F32).astype(o_ref.dtype)


def _project(x, wt, row0, n, tm, tn, out_dtype):
    m, k = x.shape
    assert n % tn == 0 and m % tm == 0 and row0 % 8 == 0
    if row0 % tn == 0:
        w_spec = pl.BlockSpec((tn, k), lambda i, j: (row0 // tn + j, 0))
    else:
        w_spec = pl.BlockSpec((pl.Element(tn), pl.Element(k)),
                              lambda i, j: (pl.multiple_of(row0 + j * tn, 8), 0))
    return pl.pallas_call(
        _mm_nt_kernel,
        grid=(m // tm, n // tn),
        in_specs=[pl.BlockSpec((tm, k), lambda i, j: (i, 0)), w_spec],
        out_specs=pl.BlockSpec((tm, tn), lambda i, j: (i, j)),
        out_shape=jax.ShapeDtypeStruct((m, n), out_dtype),
        compiler_params=pltpu.CompilerParams(
            dimension_semantics=("parallel", "arbitrary"), vmem_limit_bytes=VMEM_LIMIT),
        name="proj",
    )(x, wt)


def _conv_silu(buf_ref, carry0_ref, u, w_ref, b_ref, nseq, seq_len, first):
    ch = u.shape[1]

    @pl.when(first)
    def _():
        buf_ref[:, 0:CARRY, :] = carry0_ref[...]
    u3 = u.reshape(nseq, seq_len, ch)
    buf_ref[:, CARRY:CARRY + seq_len, :] = u3
    acc = b_ref[...] + w_ref[CONV_K - 1:CONV_K, :] * u3
    for k in range(CONV_K - 1):
        off = CARRY - (CONV_K - 1) + k
        acc = acc + w_ref[k:k + 1, :] * buf_ref[:, off:off + seq_len, :]
    tail = buf_ref[:, seq_len:seq_len + CARRY, :]
    buf_ref[:, 0:CARRY, :] = tail
    return _silu(acc).reshape(nseq * seq_len, ch), tail


def _pair_cols(tile):
    lane = lax.broadcasted_iota(jnp.int32, tile.shape, 1)
    swapped = pltpu.roll(tile, SSM_P, axis=1)
    return jnp.where(lane < SSM_P, tile, swapped), jnp.where(lane < SSM_P, swapped, tile)


def _ssd_kernel(xp_ref, bcp_ref, z_ref, dtr_ref, dtrt_ref, h0_ref, cx0_ref, cbc0_ref,
                cwx_ref, cwbc_ref, cbx_ref, cbbc_ref, dtb_ref, dtbt_ref, alogt_ref, aloge_ref,
                dskip_ref, normg_ref, e64_ref,
                y_ref, hout_ref, ctx_ref, ctbc_ref,
                xbuf, bcbuf, h_ref, *, nseq, seq_len):
    r = nseq * seq_len
    rp = max(r, LANES)
    zi = pl.program_id(1)
    first = zi == 0

    @pl.when(first)
    def _():
        h_ref[...] = h0_ref[...]

    xs, tail_x = _conv_silu(xbuf, cx0_ref, xp_ref[...].astype(F32), cwx_ref, cbx_ref, nseq, seq_len, first)
    bc, tail_bc = _conv_silu(bcbuf, cbc0_ref, bcp_ref[...].astype(F32), cwbc_ref, cbbc_ref, nseq, seq_len, first)
    ctx_ref[...] = tail_x
    ctbc_ref[...] = tail_bc

    causal, anti = _seq_masks(r, seq_len)
    dt = _softplus(dtr_ref[...] + dtb_ref[...])
    dt_e = _dot3_lhs(dt, e64_ref[...])
    cum = _dot3_rhs(causal.astype(BF16), dt * (-LOG2E * jnp.exp(aloge_ref[...])))
    cum_e = _dot3_lhs(cum, e64_ref[...])
    loga_t = _softplus(dtrt_ref[0] + dtbt_ref[...]) * (-LOG2E * jnp.exp(alogt_ref[...]))
    cum_t = _dot3_lhs(loga_t, anti.astype(BF16))
    tot_e = _seq_totals(cum_e, nseq, seq_len)
    xdt = xs * dt_e
    in_scale = jnp.exp2(cum_e)
    xw = xdt * jnp.exp2(tot_e - cum_e)
    xw_t = _pad_rows(xw, rp).T.astype(BF16)
    lane = lax.broadcasted_iota(jnp.int32, (r, LANES), 1)

    y_groups = []
    for g in range(SSM_G):
        bg = bc[:, g * SSM_N:(g + 1) * SSM_N]
        cg = bc[:, BCW + g * SSM_N:BCW + (g + 1) * SSM_N]
        cb = lax.dot_general(cg, bg, (((1,), (1,)), ((), ())), preferred_element_type=F32)
        rows = slice(g * GW, (g + 1) * GW)
        pieces = []
        for j in range(GW // LANES):
            col0 = g * GW + j * LANES
            xpair = xdt[:, col0:col0 + LANES]
            ms, rhs = [], []
            for half, colb in enumerate(_pair_cols(cum_e[:, col0:col0 + LANES])):
                head = col0 // SSM_P + half
                seg = colb[:, :r] - cum_t[head:head + 1, :]
                ms.append((cb * jnp.exp2(jnp.where(causal, seg, NEG))).astype(BF16))
                rhs.append(jnp.where(lane >= SSM_P if half else lane < SSM_P, xpair, 0.0).astype(BF16))
            if r % LANES == 0:
                pieces.append(jnp.dot(jnp.concatenate(ms, axis=1), jnp.concatenate(rhs, axis=0),
                                      preferred_element_type=F32))
            else:
                pieces.append(jnp.dot(ms[0], rhs[0], preferred_element_type=F32)
                              + jnp.dot(ms[1], rhs[1], preferred_element_type=F32))
        y_intra = jnp.concatenate(pieces, axis=1)
        y_inter = []
        for b in range(nseq):
            rb = slice(b * seq_len, (b + 1) * seq_len)
            hg = h_ref[b, rows, :]
            y_inter.append(lax.dot_general(cg[rb, :], hg, (((1,), (1,)), ((), ())),
                                           preferred_element_type=F32))
            bmask = bg if nseq == 1 else _row_mask(bg, b, seq_len)
            st = jnp.dot(xw_t[rows, :], _pad_rows(bmask, rp).astype(BF16), preferred_element_type=F32)
            dec8 = jnp.exp2(tot_e[b * seq_len:b * seq_len + 8, rows])
            for j in range(GW // LANES):
                for half, dcol in enumerate(_pair_cols(dec8[:, j * LANES:(j + 1) * LANES])):
                    h8 = 2 * j + half
                    hr = slice(g * GW + h8 * SSM_P, g * GW + (h8 + 1) * SSM_P)
                    h_ref[b, hr, :] = dcol[0:1, :] * h_ref[b, hr, :] + st[h8 * SSM_P:(h8 + 1) * SSM_P, :]
        y_inter = y_inter[0] if nseq == 1 else jnp.concatenate(y_inter, axis=0)
        y_groups.append(y_intra + y_inter * in_scale[:, rows])
    y = jnp.concatenate(y_groups, axis=1) + dskip_ref[...] * xs
    yz = y * _silu(z_ref[...].astype(F32))
    outs = []
    for g in range(SSM_G):
        blk = yz[:, g * GW:(g + 1) * GW]
        ms = jnp.sum(blk * blk, axis=1, keepdims=True) * (1.0 / GW)
        outs.append(blk * lax.rsqrt(ms + EPS))
    y_ref[...] = (jnp.concatenate(outs, axis=1) * normg_ref[...]).astype(y_ref.dtype)

    @pl.when(zi == pl.num_programs(1) - 1)
    def _():
        hout_ref[...] = h_ref[...]


def _ssd(proj, dt_raw, row0, h0, conv0, p, nb, seq):
    seq_len = SSD_CHUNK if seq % SSD_CHUNK == 0 else seq
    nseq = 1 if seq_len == SSD_CHUNK else max(1, min(nb, STEP_ROWS // seq_len))
    nz = seq // seq_len
    r = nseq * seq_len
    t = nb * seq
    r0 = row0 // r
    nblk = t // r
    dtrt = dt_raw[row0:row0 + t, :SSM_HEADS].reshape(nblk, r, SSM_HEADS).transpose(0, 2, 1)
    pad_rows = CARRY - (CONV_K - 1)
    cx0 = jnp.pad(conv0[:, :, :D], ((0, 0), (pad_rows, 0), (0, 0)))
    cbc0 = jnp.pad(conv0[:, :, D:], ((0, 0), (pad_rows, 0), (0, 0)))
    rowblk = lambda col: pl.BlockSpec((r, D), lambda b, z: (r0 + b * nz + z, col))
    const = lambda shape: pl.BlockSpec(shape, lambda b, z: tuple(0 for _ in shape))
    per_b = lambda shape: pl.BlockSpec((nseq,) + shape, lambda b, z: (b, 0, 0))
    y, hout, ctx, ctbc = pl.pallas_call(
        functools.partial(_ssd_kernel, nseq=nseq, seq_len=seq_len),
        grid=(nb // nseq, nz),
        in_specs=[rowblk(1),
                  pl.BlockSpec((r, 2 * BCW), lambda b, z: (r0 + b * nz + z, 4)),
                  rowblk(0),
                  pl.BlockSpec((r, LANES), lambda b, z: (r0 + b * nz + z, 0)),
                  pl.BlockSpec((1, SSM_HEADS, r), lambda b, z: (b * nz + z, 0, 0)),
                  per_b((D, SSM_N)), per_b((CARRY, D)), per_b((CARRY, 2 * BCW)),
                  const((CONV_K, D)), const((CONV_K, 2 * BCW)), const((1, D)), const((1, 2 * BCW)),
                  const((1, LANES)), const((SSM_HEADS, r)), const((SSM_HEADS, r)), const((1, LANES)),
                  const((1, D)), const((1, D)), const((LANES, D))],
        out_specs=[pl.BlockSpec((r, D), lambda b, z: (b * nz + z, 0)),
                   per_b((D, SSM_N)), per_b((CARRY, D)), per_b((CARRY, 2 * BCW))],
        out_shape=[jax.ShapeDtypeStruct((t, D), BF16),
                   jax.ShapeDtypeStruct((nb, D, SSM_N), F32),
                   jax.ShapeDtypeStruct((nb, CARRY, D), F32),
                   jax.ShapeDtypeStruct((nb, CARRY, 2 * BCW), F32)],
        scratch_shapes=[pltpu.VMEM((nseq, seq_len + CARRY, D), F32),
                        pltpu.VMEM((nseq, seq_len + CARRY, 2 * BCW), F32),
                        pltpu.VMEM((nseq, D, SSM_N), F32)],
        compiler_params=pltpu.CompilerParams(
            dimension_semantics=("parallel", "arbitrary"), vmem_limit_bytes=VMEM_LIMIT),
        name="ssd",
    )(proj, proj, proj, dt_raw, dtrt, h0, cx0, cbc0,
      p["cwx"], p["cwbc"], p["cbx"], p["cbbc"], p["dtb"],
      jnp.broadcast_to(p["dtb"][0, :SSM_HEADS, None], (SSM_HEADS, r)),
      jnp.broadcast_to(p["alog"][:, None], (SSM_HEADS, r)),
      p["alog_e"], p["dskip_e"], p["ssm_norm_g"], p["e64"])
    conv_new = jnp.concatenate([ctx[:, pad_rows:], ctbc[:, pad_rows:]], axis=-1)
    return y, hout, conv_new


def _gla_kernel(q_ref, f_ref, i_ref, g_ref, s0_ref, lb_ref, normg_ref, y_ref, sout_ref, s_ref,
                *, nseq, seq_len, nsub):
    r = nseq * seq_len
    rp = max(r, LANES)
    zi = pl.program_id(1)

    @pl.when(zi == 0)
    def _():
        s_ref[...] = s0_ref[...]

    lb = lb_ref[...]
    causal, _ = _seq_masks(r, seq_len)
    for sub in range(nsub):
        rows = slice(sub * r, (sub + 1) * r)
        f = lb + (1.0 - lb) * _sigmoid(f_ref[rows, :].astype(F32))
        k = 1.0 - f
        q = _silu(q_ref[rows, :].astype(F32))
        vb = i_ref[rows, :]
        bc = _dot3_rhs(causal.astype(BF16), jnp.log2(f))
        tot = _seq_totals(bc, nseq, seq_len)
        qe = q * jnp.exp2(bc)
        ke = k * jnp.exp2(-bc)
        kd_t = _pad_rows(ke * jnp.exp2(tot), rp).T.astype(BF16)
        gate = _silu(g_ref[rows, :].astype(F32))
        outs = []
        for h in range(HG_HEADS):
            sl = slice(h * HG_DK, (h + 1) * HG_DK)
            qh = qe[:, sl]
            attn = lax.dot_general(qh, ke[:, sl], (((1,), (1,)), ((), ())), preferred_element_type=F32)
            attn = jnp.where(causal, attn, 0.0)
            o = jnp.dot(attn.astype(BF16), vb[:, sl], preferred_element_type=F32)
            o_inter = []
            for b in range(nseq):
                rb = slice(b * seq_len, (b + 1) * seq_len)
                sh = s_ref[b, sl, :]
                o_inter.append(jnp.dot(qh[rb, :], sh, preferred_element_type=F32))
                vmask = vb[:, sl] if nseq == 1 else _row_mask(vb[:, sl], b, seq_len)
                st = jnp.dot(kd_t[sl, :], _pad_rows(vmask, rp), preferred_element_type=F32)
                dec = jnp.exp2(tot[b * seq_len:b * seq_len + 1, sl])
                dec_col = jnp.broadcast_to(dec, (HG_DK, HG_DK)).T
                s_ref[b, sl, :] = dec_col * sh + st
            o = o + (o_inter[0] if nseq == 1 else jnp.concatenate(o_inter, axis=0))
            ms = jnp.sum(o * o, axis=1, keepdims=True) * (1.0 / HG_DK)
            outs.append(o * lax.rsqrt(ms + EPS))
        y_ref[rows, :] = (jnp.concatenate(outs, axis=1) * normg_ref[...] * gate).astype(y_ref.dtype)

    @pl.when(zi == pl.num_programs(1) - 1)
    def _():
        sout_ref[...] = s_ref[...]


def _gla(proj, row0, s0, p, nb, seq):
    seq_len = GLA_CHUNK if seq % GLA_CHUNK == 0 else seq
    nseq = 1 if seq_len == GLA_CHUNK else max(1, min(nb, STEP_ROWS // seq_len))
    nsub = GLA_SUBCHUNKS if nseq == 1 and (seq // seq_len) % GLA_SUBCHUNKS == 0 else 1
    nz = seq // (seq_len * nsub)
    r = nseq * seq_len * nsub
    r0 = row0 // r
    rowblk = lambda col: pl.BlockSpec((r, D), lambda b, z: (r0 + b * nz + z, col))
    const = lambda shape: pl.BlockSpec(shape, lambda b, z: tuple(0 for _ in shape))
    per_b = lambda shape: pl.BlockSpec((nseq,) + shape, lambda b, z: (b, 0, 0))
    return pl.pallas_call(
        functools.partial(_gla_kernel, nseq=nseq, seq_len=seq_len, nsub=nsub),
        grid=(nb // nseq, nz),
        in_specs=[rowblk(0), rowblk(1), rowblk(2), rowblk(3), per_b((D, HG_DK)), const((1, D)), const((1, D))],
        out_specs=[pl.BlockSpec((r, D), lambda b, z: (b * nz + z, 0)), per_b((D, HG_DK))],
        out_shape=[jax.ShapeDtypeStruct((nb * seq, D), BF16), jax.ShapeDtypeStruct((nb, D, HG_DK), F32)],
        scratch_shapes=[pltpu.VMEM((nseq, D, HG_DK), F32)],
        compiler_params=pltpu.CompilerParams(
            dimension_semantics=("parallel", "arbitrary"), vmem_limit_bytes=VMEM_LIMIT),
        name="gla",
    )(proj, proj, proj, proj, s0, p["lb"], p["hgrn_norm_g"])


def _layer_norm(x, g, b):
    mu = jnp.mean(x, axis=1, keepdims=True)
    xc = x - mu
    var = jnp.mean(xc * xc, axis=1, keepdims=True)
    return xc * lax.rsqrt(var + EPS) * g + b


def _gate_kernel(yap_ref, yas_ref, ybp_ref, ybs_ref, ga_ref, gb_ref, wa_ref, wb_ref, o_ref, *, n_prompt):
    is_prompt = pl.program_id(0) < n_prompt
    ya = jnp.where(is_prompt, yap_ref[...], yas_ref[...])
    yb = jnp.where(is_prompt, ybp_ref[...], ybs_ref[...])
    a = jnp.dot(ya, wa_ref[...], preferred_element_type=F32)
    b = jnp.dot(yb, wb_ref[...], preferred_element_type=F32)
    ga = _sigmoid(ga_ref[...].astype(F32))
    gb = _sigmoid(gb_ref[...].astype(F32))
    o_ref[...] = (ga * a + gb * b).astype(o_ref.dtype)


def _gate(ya_p, ya_s, yb_p, yb_s, proj, p):
    tm = GATE_TM if ya_p.shape[0] % GATE_TM == 0 and ya_s.shape[0] % GATE_TM == 0 else MERGE_TM
    tn = MERGE_TN
    n_p = ya_p.shape[0] // tm
    t = ya_p.shape[0] + ya_s.shape[0]
    prow = pl.BlockSpec((tm, D), lambda i, j: (jnp.minimum(i, n_p - 1), 0))
    srow = pl.BlockSpec((tm, D), lambda i, j: (jnp.maximum(i - n_p, 0), 0))
    gcol = lambda seg: pl.BlockSpec((tm, tn), lambda i, j: (i, seg * (D // tn) + j))
    wcol = pl.BlockSpec((D, tn), lambda i, j: (0, j))
    return pl.pallas_call(
        functools.partial(_gate_kernel, n_prompt=n_p),
        grid=(t // tm, D // tn),
        in_specs=[prow, srow, prow, srow, gcol(4), gcol(5), wcol, wcol],
        out_specs=pl.BlockSpec((tm, tn), lambda i, j: (i, j)),
        out_shape=jax.ShapeDtypeStruct((t, D), BF16),
        compiler_params=pltpu.CompilerParams(
            dimension_semantics=("parallel", "arbitrary"), vmem_limit_bytes=VMEM_LIMIT),
        name="gate",
    )(ya_p, ya_s, yb_p, yb_s, proj, proj, p["w_a"], p["w_b"])


def _merge_kernel(m_ref, xp_ref, xs_ref, wo_ref, g1_ref, b1_ref, wrh_ref, wrl_ref, br_ref,
                  x1_ref, route_ref, *, n_prompt):
    x = jnp.where(pl.program_id(0) < n_prompt, xp_ref[...], xs_ref[...])
    mix = jnp.dot(m_ref[...], wo_ref[...], preferred_element_type=F32)
    x1 = _layer_norm(ALPHA * x + mix, g1_ref[...], b1_ref[...])
    _to_token_tiles(x1_ref, x1)
    hi = x1.astype(BF16)
    lo = (x1 - hi.astype(F32)).astype(BF16)
    logits = (jnp.dot(hi, wrh_ref[...], preferred_element_type=F32)
              + jnp.dot(lo, wrh_ref[...], preferred_element_type=F32)
              + jnp.dot(hi, wrl_ref[...], preferred_element_type=F32) + br_ref[...])
    lane = lax.broadcasted_iota(jnp.int32, logits.shape, 1).astype(F32)
    gl = jnp.where(lane < N_GROUPS, logits, NEG)
    gmax = jnp.max(gl, axis=1, keepdims=True)
    gsel = jnp.min(jnp.where(gl == gmax, lane, 1e9), axis=1, keepdims=True)
    p_grp = 1.0 / jnp.sum(jnp.exp(gl - gmax), axis=1, keepdims=True)
    lo = N_GROUPS + gsel * EPG
    el = jnp.where((lane >= lo) & (lane < lo + EPG), logits, NEG)
    v0 = jnp.max(el, axis=1, keepdims=True)
    i0 = jnp.min(jnp.where(el == v0, lane, 1e9), axis=1, keepdims=True)
    el2 = jnp.where(lane == i0, NEG, el)
    v1 = jnp.max(el2, axis=1, keepdims=True)
    i1 = jnp.min(jnp.where(el2 == v1, lane, 1e9), axis=1, keepdims=True)
    e1 = jnp.exp(v1 - v0)
    p0 = 1.0 / (1.0 + e1)
    p1 = e1 * p0
    route = jnp.where(lane == 0, i0 - N_GROUPS,
                      jnp.where(lane == 1, i1 - N_GROUPS,
                                jnp.where(lane == 2, p_grp * p0,
                                          jnp.where(lane == 3, p_grp * p1, 0.0))))
    route_ref[...] = route


def _merge(merged, x_p, x_s, p):
    tm = MERGE_TM
    n_p = x_p.shape[0] // tm
    t = x_p.shape[0] + x_s.shape[0]
    row = pl.BlockSpec((tm, D), lambda i: (i, 0))
    prow = pl.BlockSpec((tm, D), lambda i: (jnp.minimum(i, n_p - 1), 0))
    srow = pl.BlockSpec((tm, D), lambda i: (jnp.maximum(i - n_p, 0), 0))
    res = lambda shape: pl.BlockSpec(shape, lambda i: (0, 0), pipeline_mode=pl.Buffered(1))
    wr_hi = p["wr"].astype(BF16)
    wr_lo = (p["wr"] - wr_hi.astype(F32)).astype(BF16)
    return pl.pallas_call(
        functools.partial(_merge_kernel, n_prompt=n_p),
        grid=(t // tm,),
        in_specs=[row, prow, srow, res((D, D)), res((1, D)), res((1, D)),
                  res((D, LANES)), res((D, LANES)), res((1, LANES))],
        out_specs=[pl.BlockSpec((tm * ROW_TILES, LANES), lambda i: (i, 0)),
                   pl.BlockSpec((tm, LANES), lambda i: (i, 0))],
        out_shape=[jax.ShapeDtypeStruct((t * ROW_TILES, LANES), F32), jax.ShapeDtypeStruct((t, LANES), F32)],
        compiler_params=pltpu.CompilerParams(
            dimension_semantics=("parallel",), vmem_limit_bytes=VMEM_LIMIT),
        name="merge",
    )(merged, x_p, x_s, p["w_out"], p["ln1_g"], p["ln1_b"], wr_hi, wr_lo, p["br"])


def _row_gather(src_hbm, dst, sem, idx_ref, base, n, unroll=8):
    def body(j, carry):
        r = idx_ref[base + j]
        pltpu.make_async_copy(src_hbm.at[pl.ds(pl.multiple_of(r * ROW_TILES, ROW_TILES), ROW_TILES), :],
                              dst.at[pl.ds(j * ROW_TILES, ROW_TILES), :], sem).start()
        return carry
    lax.fori_loop(0, n, body, 0, unroll=unroll)


def _expert_kernel(tstart_ref, ntile_ref, nv_ref, src_ref, x1_hbm, w1_hbm, w3_hbm, w2_hbm, es_hbm,
                   xbuf, obuf, gsem, osem, w1f, w3f, w2f, wsem, w1b, w3b, w2b, *, tm, n_tiles):
    e = pl.program_id(0)
    n_e = ntile_ref[e]
    g0 = tstart_ref[e]
    nv = nv_ref[0]
    rows = tm * ROW_TILES

    def gather_wait(s):
        pltpu.make_async_copy(x1_hbm.at[pl.ds(0, rows), :], xbuf.at[s], gsem.at[s]).wait()

    def out_copy(g):
        return pltpu.make_async_copy(obuf, es_hbm.at[pl.ds(pl.multiple_of(g * rows, rows), rows), :], osem)

    def weight_copies(ex, s):
        return [pltpu.make_async_copy(w_hbm.at[ex], w_f.at[s], wsem.at[s])
                for w_hbm, w_f in ((w1_hbm, w1f), (w3_hbm, w3f), (w2_hbm, w2f))]

    wslot = lax.rem(e, 2)

    @pl.when(e == 0)
    def _():
        _row_gather(x1_hbm, xbuf.at[0], gsem.at[0], src_ref, 0, tm)
        for c in weight_copies(0, 0):
            c.start(priority=1)

    @pl.when(e + 1 < pl.num_programs(0))
    def _():
        for c in weight_copies(e + 1, 1 - wslot):
            c.start(priority=1)

    for c in weight_copies(e, wslot):
        c.wait()

    @pl.when(n_e > 0)
    def _():
        w1b[...] = w1f[wslot].astype(BF16)
        w3b[...] = w3f[wslot].astype(BF16)
        w2b[...] = w2f[wslot].astype(BF16)

    def tile(k, carry):
        g = g0 + k
        slot = lax.rem(g, 2)
        _row_gather(x1_hbm, xbuf.at[1 - slot], gsem.at[1 - slot], src_ref, (g + 1) * tm, tm, unroll=True)
        gather_wait(slot)
        xb = _from_token_tiles(xbuf.at[slot]).astype(BF16)
        h1 = jnp.dot(xb, w1b[...], preferred_element_type=F32)
        h3 = jnp.dot(xb, w3b[...], preferred_element_type=F32)
        h = (_silu(h1) * h3).astype(BF16)
        out = jnp.dot(h, w2b[...], preferred_element_type=F32)

        @pl.when(g > 0)
        def _():
            out_copy(g).wait()
        _to_token_tiles(obuf, out)
        out_copy(g).start()
        return carry

    lax.fori_loop(0, n_e, tile, 0)

    @pl.when(e == pl.num_programs(0) - 1)
    def _():
        gather_wait(lax.rem(nv, 2))
        out_copy(0).wait()
        obuf[...] = jnp.zeros_like(obuf)

        def zero_tile(g, carry):
            out_copy(g).start()
            out_copy(g).wait()
            return carry

        lax.fori_loop(nv, n_tiles, zero_tile, 0)


def _experts(x1, tstart, ntile, nv, src, w1, w3, w2, n_tiles, tm):
    any_spec = pl.BlockSpec(memory_space=pl.ANY)
    return pl.pallas_call(
        functools.partial(_expert_kernel, tm=tm, n_tiles=n_tiles),
        grid_spec=pltpu.PrefetchScalarGridSpec(
            num_scalar_prefetch=4,
            grid=(N_EXPERTS,),
            in_specs=[any_spec, any_spec, any_spec, any_spec],
            out_specs=any_spec,
            scratch_shapes=[pltpu.VMEM((2, tm * ROW_TILES, LANES), F32), pltpu.VMEM((tm * ROW_TILES, LANES), F32),
                            pltpu.SemaphoreType.DMA((2,)), pltpu.SemaphoreType.DMA(()),
                            pltpu.VMEM((2, D, D_EXPERT), F32), pltpu.VMEM((2, D, D_EXPERT), F32),
                            pltpu.VMEM((2, D_EXPERT, D), F32), pltpu.SemaphoreType.DMA((2,)),
                            pltpu.VMEM((D, D_EXPERT), BF16), pltpu.VMEM((D, D_EXPERT), BF16),
                            pltpu.VMEM((D_EXPERT, D), BF16)]),
        out_shape=jax.ShapeDtypeStruct((n_tiles * tm * ROW_TILES, LANES), F32),
        compiler_params=pltpu.CompilerParams(
            dimension_semantics=("arbitrary",), vmem_limit_bytes=VMEM_LIMIT, disable_bounds_checks=True),
        name="experts",
    )(tstart, ntile, nv, src, x1, w1, w3, w2)


def _combine_kernel(dest_ref, es_hbm, x1_ref, route_ref, g2_ref, b2_ref, op_ref, os_ref, gbuf, sems,
                    *, tm, n_prompt):
    i = pl.program_id(0)
    n = pl.num_programs(0)
    slot = lax.rem(i, 2)

    def start(step, s):
        for k in range(2):
            _row_gather(es_hbm, gbuf.at[s, k], sems.at[s], dest_ref, (k * n + step) * tm, tm)

    @pl.when(i == 0)
    def _():
        start(0, 0)

    @pl.when(i + 1 < n)
    def _():
        start(i + 1, 1 - slot)

    for k in range(2):
        pltpu.make_async_copy(es_hbm.at[pl.ds(0, tm * ROW_TILES), :], gbuf.at[slot, k], sems.at[slot]).wait()
    route = route_ref[...]
    w0 = route[:, 2:3]
    w1 = route[:, 3:4]
    moe = w0 * _from_token_tiles(gbuf.at[slot, 0]) + w1 * _from_token_tiles(gbuf.at[slot, 1])
    out = _layer_norm(ALPHA * _from_token_tiles(x1_ref) + moe, g2_ref[...], b2_ref[...])

    @pl.when(i < n_prompt)
    def _():
        op_ref[...] = out

    @pl.when(i >= n_prompt)
    def _():
        os_ref[...] = out


def _combine(es, dest_km, x1, route, p, tp):
    t = x1.shape[0] // ROW_TILES
    tm = ROW_TM
    n_p = tp // tm
    return pl.pallas_call(
        functools.partial(_combine_kernel, tm=tm, n_prompt=n_p),
        grid_spec=pltpu.PrefetchScalarGridSpec(
            num_scalar_prefetch=1,
            grid=(t // tm,),
            in_specs=[pl.BlockSpec(memory_space=pl.ANY),
                      pl.BlockSpec((tm * ROW_TILES, LANES), lambda i, d: (i, 0)),
                      pl.BlockSpec((tm, LANES), lambda i, d: (i, 0)),
                      pl.BlockSpec((1, D), lambda i, d: (0, 0)),
                      pl.BlockSpec((1, D), lambda i, d: (0, 0))],
            out_specs=[pl.BlockSpec((tm, D), lambda i, d: (jnp.minimum(i, n_p - 1), 0)),
                       pl.BlockSpec((tm, D), lambda i, d: (jnp.maximum(i - n_p, 0), 0))],
            scratch_shapes=[pltpu.VMEM((2, 2, tm * ROW_TILES, LANES), F32), pltpu.SemaphoreType.DMA((2,))]),
        out_shape=[jax.ShapeDtypeStruct((tp, D), F32), jax.ShapeDtypeStruct((t - tp, D), F32)],
        compiler_params=pltpu.CompilerParams(
            dimension_semantics=("arbitrary",), vmem_limit_bytes=VMEM_LIMIT, disable_bounds_checks=True),
        name="combine",
    )(dest_km, es, x1, route, p["ln2_g"], p["ln2_b"])


def _route_meta(e_flat, tm, n_tiles):
    a = e_flat.shape[0]
    ids = jnp.arange(N_EXPERTS, dtype=jnp.int32)
    onehot = (e_flat[:, None] == ids[None, :]).astype(jnp.int32)
    csum = jnp.cumsum(onehot, axis=0)
    rank = jnp.sum((csum - onehot) * onehot, axis=1)
    counts = csum[-1]
    tiles_per = (counts + tm - 1) // tm
    tile_end = jnp.cumsum(tiles_per)
    tile_start = tile_end - tiles_per
    nv = tile_end[-1]
    dest = tile_start[e_flat] * tm + rank
    src = jnp.zeros((n_tiles * tm,), jnp.int32).at[dest].set(
        jnp.arange(a, dtype=jnp.int32) // 2, unique_indices=True, mode="promise_in_bounds")
    return (tile_start.astype(jnp.int32), tiles_per.astype(jnp.int32), nv.reshape(1).astype(jnp.int32), src,
            dest.astype(jnp.int32))


def kernel(x_prompt, x_sample, state_ssm, state_conv, state_hgrn, w_in, conv_w, conv_b, dt_bias, a_log, d_skip, ssm_norm_g, hgrn_lb_logits, hgrn_norm_g, w_a, w_b, w_out, ln1_g, ln1_b, router_g_w, router_g_b, router_e_w, router_e_b, exp_w1, exp_w3, exp_w2, ln2_g, ln2_b):
    nbp, seqp, _ = x_prompt.shape
    nbs, seqs, _ = x_sample.shape
    tp, ts = nbp * seqp, nbs * seqs
    t = tp + ts
    l = 0
    x_p, x_s = x_prompt.reshape(tp, D), x_sample.reshape(ts, D)

    wt = jnp.transpose(w_in[l])
    o_dt = D + D + 2 * BCW
    o_q = o_dt + SSM_HEADS
    xb, dt_raw = _cast_rows(x_p, x_s, wt, o_dt, MERGE_TM)
    dt_raw = jnp.pad(dt_raw, ((0, 0), (0, LANES - SSM_HEADS)))
    tm = PROJ_TM if t % PROJ_TM == 0 else 1024
    proj_a = _project(xb, wt, 0, o_dt, tm, PROJ_TN, BF16)
    proj_b = _project(xb, wt, o_q, 6 * D, tm, PROJ_TN, BF16)

    lb_all = jnp.cumsum(jax.nn.softmax(hgrn_lb_logits.astype(F32), axis=0), axis=0)
    head_of = np.arange(D) // SSM_P
    pad128 = lambda v: jnp.pad(v, (0, LANES - v.shape[0])).reshape(1, LANES)
    p = {
        "cwx": conv_w[l][:, :D], "cwbc": conv_w[l][:, D:],
        "cbx": conv_b[l][:D].reshape(1, D), "cbbc": conv_b[l][D:].reshape(1, 2 * BCW),
        "dtb": pad128(dt_bias[l]), "alog": a_log[l], "alog_e": pad128(a_log[l]),
        "dskip_e": jnp.repeat(d_skip[l], SSM_P).reshape(1, D),
        "ssm_norm_g": ssm_norm_g[l].reshape(1, D),
        "e64": jnp.asarray(np.arange(LANES)[:, None] == head_of[None, :], dtype=BF16),
        "lb": lb_all[l].reshape(1, D), "hgrn_norm_g": hgrn_norm_g[l].reshape(1, D),
        "w_a": w_a[l].astype(BF16), "w_b": w_b[l].astype(BF16), "w_out": w_out[l].astype(BF16),
        "ln1_g": ln1_g[l].reshape(1, D), "ln1_b": ln1_b[l].reshape(1, D),
        "ln2_g": ln2_g[l].reshape(1, D), "ln2_b": ln2_b[l].reshape(1, D),
        "wr": jnp.pad(jnp.concatenate([router_g_w[l], router_e_w[l]], axis=1),
                      ((0, 0), (0, LANES - N_GROUPS - N_EXPERTS))),
        "br": pad128(jnp.concatenate([router_g_b[l], router_e_b[l]])),
    }

    ya_p, ssm_p, conv_p = _ssd(proj_a, dt_raw, 0, jnp.zeros((nbp, D, SSM_N), F32),
                               jnp.zeros((nbp, CONV_K - 1, D + 2 * BCW), F32), p, nbp, seqp)
    ya_s, ssm_s, conv_s = _ssd(proj_a, dt_raw, tp, state_ssm[l].reshape(nbs, D, SSM_N), state_conv[l],
                               p, nbs, seqs)
    yb_p, hg_p = _gla(proj_b, 0, jnp.zeros((nbp, D, HG_DK), F32), p, nbp, seqp)
    yb_s, hg_s = _gla(proj_b, tp, state_hgrn[l].reshape(nbs, D, HG_DK), p, nbs, seqs)

    merged = _gate(ya_p, ya_s, yb_p, yb_s, proj_b, p)
    x1, route = _merge(merged, x_p, x_s, p)

    e_flat = route[:, :2].astype(jnp.int32).reshape(-1)
    n_tiles = (2 * t) // MOE_TM + N_EXPERTS + 1
    tstart, ntile, nv, src, dest = _route_meta(e_flat, MOE_TM, n_tiles)
    es = _experts(x1, tstart, ntile, nv, src, exp_w1[l], exp_w3[l], exp_w2[l], n_tiles, MOE_TM)
    dest_km = dest.reshape(t // ROW_TM, ROW_TM, 2).transpose(2, 0, 1).reshape(-1)
    x2_p, x2_s = _combine(es, dest_km, x1, route, p, tp)

    y_prompt = x2_p.reshape(nbp, seqp, D)
    y_sample = x2_s.reshape(nbs, seqs, D)
    return (y_prompt, y_sample,
            ssm_p.reshape(1, nbp, SSM_HEADS, SSM_P, SSM_N), conv_p[None],
            hg_p.reshape(1, nbp, HG_HEADS, HG_DK, HG_DK),
            ssm_s.reshape(1, nbs, SSM_HEADS, SSM_P, SSM_N), conv_s[None],
            hg_s.reshape(1, nbs, HG_HEADS, HG_DK, HG_DK))
```

```python
import functools

import jax
import jax.numpy as jnp
import numpy as np
from jax import lax
from jax.experimental import pallas as pl
from jax.experimental.pallas import tpu as pltpu

F32 = jnp.float32
BF16 = jnp.bfloat16

D = 2048
SSM_HEADS = 32
SSM_P = 64
SSM_N = 128
SSM_G = 4
GW = D // SSM_G
BCW = SSM_G * SSM_N
CONV_K = 4
HG_HEADS = 16
HG_DK = 128
N_GROUPS = 4
EPG = 8
N_EXPERTS = 32
D_EXPERT = 512
EPS = 1e-5
ALPHA = 2.0 ** 0.25
NEG = -1e30
LOG2E = 1.4426950408889634
LANES = 128
ROW_TILES = D // LANES

VMEM_LIMIT = 56 * 1024 * 1024
SSD_CHUNK = 128
GLA_CHUNK = 64
GLA_SUBCHUNKS = 2
STEP_ROWS = 64
PROJ_TM = 2304
PROJ_TN = 512
MOE_TM = 128
GATHER_SLOTS = 3
ROW_TM = 256
MERGE_TM = 512
GATE_TM = 1024
MERGE_TN = 512
CARRY = 8


def _sigmoid(x):
    return 0.5 * jnp.tanh(0.5 * x) + 0.5


def _silu(x):
    h = 0.5 * x
    return h * jnp.tanh(h) + h


def _softplus(x):
    return jnp.maximum(x, 0.0) + jnp.log(1.0 + jnp.exp(-jnp.abs(x)))


def _split3(x):
    hi = x.astype(BF16)
    r = x - hi.astype(F32)
    mid = r.astype(BF16)
    lo = (r - mid.astype(F32)).astype(BF16)
    return hi, mid, lo


def _dot3_rhs(m_bf16, x):
    return sum(jnp.dot(m_bf16, part, preferred_element_type=F32) for part in _split3(x))


def _dot3_lhs(x, m_bf16):
    return sum(jnp.dot(part, m_bf16, preferred_element_type=F32) for part in _split3(x))


def _seq_masks(r, seq_len):
    shift = seq_len.bit_length() - 1
    row = lax.broadcasted_iota(jnp.int32, (r, r), 0)
    col = lax.broadcasted_iota(jnp.int32, (r, r), 1)
    same = lax.shift_right_logical(row, shift) == lax.shift_right_logical(col, shift)
    return same & (row >= col), same & (row <= col)


def _seq_totals(cum, nseq, seq_len):
    w = cum.shape[1]
    parts = [jnp.broadcast_to(cum[(b + 1) * seq_len - 1:(b + 1) * seq_len, :], (seq_len, w)) for b in range(nseq)]
    return parts[0] if nseq == 1 else jnp.concatenate(parts, axis=0)


def _pad_rows(x, rows):
    if x.shape[0] == rows:
        return x
    return jnp.concatenate([x, jnp.zeros((rows - x.shape[0], x.shape[1]), x.dtype)], axis=0)


def _to_token_tiles(ref, val):
    rows = val.shape[0]
    for s in range(ROW_TILES):
        ref[pl.ds(s, rows, stride=ROW_TILES), :] = val[:, s * LANES:(s + 1) * LANES]


def _from_token_tiles(ref):
    rows = ref.shape[0] // ROW_TILES
    return jnp.concatenate([ref[pl.ds(s, rows, stride=ROW_TILES), :] for s in range(ROW_TILES)], axis=1)


def _row_mask(x, b, seq_len):
    row = lax.broadcasted_iota(jnp.int32, x.shape, 0)
    return jnp.where((row >= b * seq_len) & (row < (b + 1) * seq_len), x, jnp.zeros_like(x))


def _cast_kernel(xp_ref, xs_ref, wdt_ref, o_ref, dt_ref, *, n_prompt):
    xb = jnp.where(pl.program_id(0) < n_prompt, xp_ref[...], xs_ref[...]).astype(BF16)
    o_ref[...] = xb
    dt_ref[...] = lax.dot_general(xb, wdt_ref[...].astype(BF16), (((1,), (1,)), ((), ())),
                                  preferred_element_type=F32)


def _cast_rows(x_p, x_s, wt, dt_row0, tm):
    k = x_p.shape[1]
    n_p = x_p.shape[0] // tm
    m = x_p.shape[0] + x_s.shape[0]
    return pl.pallas_call(
        functools.partial(_cast_kernel, n_prompt=n_p),
        grid=(m // tm,),
        in_specs=[pl.BlockSpec((tm, k), lambda i: (jnp.minimum(i, n_p - 1), 0)),
                  pl.BlockSpec((tm, k), lambda i: (jnp.maximum(i - n_p, 0), 0)),
                  pl.BlockSpec((SSM_HEADS, k), lambda i: (dt_row0 // SSM_HEADS, 0))],
        out_specs=[pl.BlockSpec((tm, k), lambda i: (i, 0)), pl.BlockSpec((tm, SSM_HEADS), lambda i: (i, 0))],
        out_shape=[jax.ShapeDtypeStruct((m, k), BF16), jax.ShapeDtypeStruct((m, SSM_HEADS), F32)],
        compiler_params=pltpu.CompilerParams(dimension_semantics=("parallel",), vmem_limit_bytes=VMEM_LIMIT),
        name="cast",
    )(x_p, x_s, wt)


def _mm_nt_kernel(x_ref, wt_ref, o_ref):
    w = wt_ref[...].astype(BF16)
    o_ref[...] = lax.dot_general(x_ref[...], w, (((1,), (1,)), ((), ())),
                                 preferred_element_type=F32).astype(o_ref.dtype)


def _project(x, wt, row0, n, tm, tn, out_dtype):
    m, k = x.shape
    assert n % tn == 0 and m % tm == 0 and row0 % 8 == 0
    if row0 % tn == 0:
        w_spec = pl.BlockSpec((tn, k), lambda i, j: (row0 // tn + j, 0))
    else:
        w_spec = pl.BlockSpec((pl.Element(tn), pl.Element(k)),
                              lambda i, j: (pl.multiple_of(row0 + j * tn, 8), 0))
    return pl.pallas_call(
        _mm_nt_kernel,
        grid=(m // tm, n // tn),
        in_specs=[pl.BlockSpec((tm, k), lambda i, j: (i, 0)), w_spec],
        out_specs=pl.BlockSpec((tm, tn), lambda i, j: (i, j)),
        out_shape=jax.ShapeDtypeStruct((m, n), out_dtype),
        compiler_params=pltpu.CompilerParams(
            dimension_semantics=("parallel", "arbitrary"), vmem_limit_bytes=VMEM_LIMIT),
        name="proj",
    )(x, wt)


def _conv_silu(buf_ref, carry0_ref, u, w_ref, b_ref, nseq, seq_len, first):
    ch = u.shape[1]

    @pl.when(first)
    def _():
        buf_ref[:, 0:CARRY, :] = carry0_ref[...]
    u3 = u.reshape(nseq, seq_len, ch)
    buf_ref[:, CARRY:CARRY + seq_len, :] = u3
    acc = b_ref[...] + w_ref[CONV_K - 1:CONV_K, :] * u3
    for k in range(CONV_K - 1):
        off = CARRY - (CONV_K - 1) + k
        acc = acc + w_ref[k:k + 1, :] * buf_ref[:, off:off + seq_len, :]
    tail = buf_ref[:, seq_len:seq_len + CARRY, :]
    buf_ref[:, 0:CARRY, :] = tail
    return _silu(acc).reshape(nseq * seq_len, ch), tail


def _pair_cols(tile):
    lane = lax.broadcasted_iota(jnp.int32, tile.shape, 1)
    swapped = pltpu.roll(tile, SSM_P, axis=1)
    return jnp.where(lane < SSM_P, tile, swapped), jnp.where(lane < SSM_P, swapped, tile)


def _ssd_kernel(xp_ref, bcp_ref, z_ref, dtr_ref, dtrt_ref, h0_ref, cx0_ref, cbc0_ref,
                cwx_ref, cwbc_ref, cbx_ref, cbbc_ref, dtb_ref, dtbt_ref, alogt_ref, aloge_ref,
                dskip_ref, normg_ref, e64_ref,
                y_ref, hout_ref, ctx_ref, ctbc_ref,
                xbuf, bcbuf, h_ref, *, nseq, seq_len):
    r = nseq * seq_len
    rp = max(r, LANES)
    zi = pl.program_id(1)
    first = zi == 0

    @pl.when(first)
    def _():
        h_ref[...] = h0_ref[...]

    xs, tail_x = _conv_silu(xbuf, cx0_ref, xp_ref[...].astype(F32), cwx_ref, cbx_ref, nseq, seq_len, first)
    bc, tail_bc = _conv_silu(bcbuf, cbc0_ref, bcp_ref[...].astype(F32), cwbc_ref, cbbc_ref, nseq, seq_len, first)
    ctx_ref[...] = tail_x
    ctbc_ref[...] = tail_bc

    causal, anti = _seq_masks(r, seq_len)
    dt = _softplus(dtr_ref[...] + dtb_ref[...])
    dt_e = _dot3_lhs(dt, e64_ref[...])
    cum = _dot3_rhs(causal.astype(BF16), dt * (-LOG2E * jnp.exp(aloge_ref[...])))
    cum_e = _dot3_lhs(cum, e64_ref[...])
    loga_t = _softplus(dtrt_ref[0] + dtbt_ref[...]) * (-LOG2E * jnp.exp(alogt_ref[...]))
    cum_t = _dot3_lhs(loga_t, anti.astype(BF16))
    tot_e = _seq_totals(cum_e, nseq, seq_len)
    xdt = xs * dt_e
    in_scale = jnp.exp2(cum_e)
    xw = xdt * jnp.exp2(tot_e - cum_e)
    xw_t = _pad_rows(xw, rp).T.astype(BF16)
    lane = lax.broadcasted_iota(jnp.int32, (r, LANES), 1)

    y_groups = []
    for g in range(SSM_G):
        bg = bc[:, g * SSM_N:(g + 1) * SSM_N]
        cg = bc[:, BCW + g * SSM_N:BCW + (g + 1) * SSM_N]
        cb = lax.dot_general(cg, bg, (((1,), (1,)), ((), ())), preferred_element_type=F32)
        rows = slice(g * GW, (g + 1) * GW)
        pieces = []
        for j in range(GW // LANES):
            col0 = g * GW + j * LANES
            xpair = xdt[:, col0:col0 + LANES]
            ms, rhs = [], []
            for half, colb in enumerate(_pair_cols(cum_e[:, col0:col0 + LANES])):
                head = col0 // SSM_P + half
                seg = colb[:, :r] - cum_t[head:head + 1, :]
                ms.append((cb * jnp.exp2(jnp.where(causal, seg, NEG))).astype(BF16))
                rhs.append(jnp.where(lane >= SSM_P if half else lane < SSM_P, xpair, 0.0).astype(BF16))
            if r % LANES == 0:
                pieces.append(jnp.dot(jnp.concatenate(ms, axis=1), jnp.concatenate(rhs, axis=0),
                                      preferred_element_type=F32))
            else:
                pieces.append(jnp.dot(ms[0], rhs[0], preferred_element_type=F32)
                              + jnp.dot(ms[1], rhs[1], preferred_element_type=F32))
        y_intra = jnp.concatenate(pieces, axis=1)
        y_inter = []
        for b in range(nseq):
            rb = slice(b * seq_len, (b + 1) * seq_len)
            hg = h_ref[b, rows, :]
            y_inter.append(lax.dot_general(cg[rb, :], hg, (((1,), (1,)), ((), ())),
                                           preferred_element_type=F32))
            bmask = bg if nseq == 1 else _row_mask(bg, b, seq_len)
            st = jnp.dot(xw_t[rows, :], _pad_rows(bmask, rp).astype(BF16), preferred_element_type=F32)
            dec8 = jnp.exp2(tot_e[b * seq_len:b * seq_len + 8, rows])
            for j in range(GW // LANES):
                for half, dcol in enumerate(_pair_cols(dec8[:, j * LANES:(j + 1) * LANES])):
                    h8 = 2 * j + half
                    hr = slice(g * GW + h8 * SSM_P, g * GW + (h8 + 1) * SSM_P)
                    h_ref[b, hr, :] = dcol[0:1, :] * h_ref[b, hr, :] + st[h8 * SSM_P:(h8 + 1) * SSM_P, :]
        y_inter = y_inter[0] if nseq == 1 else jnp.concatenate(y_inter, axis=0)
        y_groups.append(y_intra + y_inter * in_scale[:, rows])
    y = jnp.concatenate(y_groups, axis=1) + dskip_ref[...] * xs
    yz = y * _silu(z_ref[...].astype(F32))
    outs = []
    for g in range(SSM_G):
        blk = yz[:, g * GW:(g + 1) * GW]
        ms = jnp.sum(blk * blk, axis=1, keepdims=True) * (1.0 / GW)
        outs.append(blk * lax.rsqrt(ms + EPS))
    y_ref[...] = (jnp.concatenate(outs, axis=1) * normg_ref[...]).astype(y_ref.dtype)

    @pl.when(zi == pl.num_programs(1) - 1)
    def _():
        hout_ref[...] = h_ref[...]


def _ssd(proj, dt_raw, row0, h0, conv0, p, nb, seq):
    seq_len = SSD_CHUNK if seq % SSD_CHUNK == 0 else seq
    nseq = 1 if seq_len == SSD_CHUNK else max(1, min(nb, STEP_ROWS // seq_len))
    nz = seq // seq_len
    r = nseq * seq_len
    t = nb * seq
    r0 = row0 // r
    nblk = t // r
    dtrt = dt_raw[row0:row0 + t, :SSM_HEADS].reshape(nblk, r, SSM_HEADS).transpose(0, 2, 1)
    pad_rows = CARRY - (CONV_K - 1)
    cx0 = jnp.pad(conv0[:, :, :D], ((0, 0), (pad_rows, 0), (0, 0)))
    cbc0 = jnp.pad(conv0[:, :, D:], ((0, 0), (pad_rows, 0), (0, 0)))
    rowblk = lambda col: pl.BlockSpec((r, D), lambda b, z: (r0 + b * nz + z, col))
    const = lambda shape: pl.BlockSpec(shape, lambda b, z: tuple(0 for _ in shape))
    per_b = lambda shape: pl.BlockSpec((nseq,) + shape, lambda b, z: (b, 0, 0))
    y, hout, ctx, ctbc = pl.pallas_call(
        functools.partial(_ssd_kernel, nseq=nseq, seq_len=seq_len),
        grid=(nb // nseq, nz),
        in_specs=[rowblk(1),
                  pl.BlockSpec((r, 2 * BCW), lambda b, z: (r0 + b * nz + z, 4)),
                  rowblk(0),
                  pl.BlockSpec((r, LANES), lambda b, z: (r0 + b * nz + z, 0)),
                  pl.BlockSpec((1, SSM_HEADS, r), lambda b, z: (b * nz + z, 0, 0)),
                  per_b((D, SSM_N)), per_b((CARRY, D)), per_b((CARRY, 2 * BCW)),
                  const((CONV_K, D)), const((CONV_K, 2 * BCW)), const((1, D)), const((1, 2 * BCW)),
                  const((1, LANES)), const((SSM_HEADS, r)), const((SSM_HEADS, r)), const((1, LANES)),
                  const((1, D)), const((1, D)), const((LANES, D))],
        out_specs=[pl.BlockSpec((r, D), lambda b, z: (b * nz + z, 0)),
                   per_b((D, SSM_N)), per_b((CARRY, D)), per_b((CARRY, 2 * BCW))],
        out_shape=[jax.ShapeDtypeStruct((t, D), BF16),
                   jax.ShapeDtypeStruct((nb, D, SSM_N), F32),
                   jax.ShapeDtypeStruct((nb, CARRY, D), F32),
                   jax.ShapeDtypeStruct((nb, CARRY, 2 * BCW), F32)],
        scratch_shapes=[pltpu.VMEM((nseq, seq_len + CARRY, D), F32),
                        pltpu.VMEM((nseq, seq_len + CARRY, 2 * BCW), F32),
                        pltpu.VMEM((nseq, D, SSM_N), F32)],
        compiler_params=pltpu.CompilerParams(
            dimension_semantics=("parallel", "arbitrary"), vmem_limit_bytes=VMEM_LIMIT),
        name="ssd",
    )(proj, proj, proj, dt_raw, dtrt, h0, cx0, cbc0,
      p["cwx"], p["cwbc"], p["cbx"], p["cbbc"], p["dtb"],
      jnp.broadcast_to(p["dtb"][0, :SSM_HEADS, None], (SSM_HEADS, r)),
      jnp.broadcast_to(p["alog"][:, None], (SSM_HEADS, r)),
      p["alog_e"], p["dskip_e"], p["ssm_norm_g"], p["e64"])
    conv_new = jnp.concatenate([ctx[:, pad_rows:], ctbc[:, pad_rows:]], axis=-1)
    return y, hout, conv_new


def _gla_kernel(q_ref, f_ref, i_ref, g_ref, s0_ref, lb_ref, normg_ref, y_ref, sout_ref, s_ref,
                *, nseq, seq_len, nsub):
    r = nseq * seq_len
    rp = max(r, LANES)
    zi = pl.program_id(1)

    @pl.when(zi == 0)
    def _():
        s_ref[...] = s0_ref[...]

    lb = lb_ref[...]
    causal, _ = _seq_masks(r, seq_len)
    for sub in range(nsub):
        rows = slice(sub * r, (sub + 1) * r)
        f = lb + (1.0 - lb) * _sigmoid(f_ref[rows, :].astype(F32))
        k = 1.0 - f
        q = _silu(q_ref[rows, :].astype(F32))
        vb = i_ref[rows, :]
        bc = _dot3_rhs(causal.astype(BF16), jnp.log2(f))
        tot = _seq_totals(bc, nseq, seq_len)
        qe = q * jnp.exp2(bc)
        ke = k * jnp.exp2(-bc)
        kd_t = _pad_rows(ke * jnp.exp2(tot), rp).T.astype(BF16)
        gate = _silu(g_ref[rows, :].astype(F32))
        outs = []
        for h in range(HG_HEADS):
            sl = slice(h * HG_DK, (h + 1) * HG_DK)
            qh = qe[:, sl]
            attn = lax.dot_general(qh, ke[:, sl], (((1,), (1,)), ((), ())), preferred_element_type=F32)
            attn = jnp.where(causal, attn, 0.0)
            o = jnp.dot(attn.astype(BF16), vb[:, sl], preferred_element_type=F32)
            o_inter = []
            for b in range(nseq):
                rb = slice(b * seq_len, (b + 1) * seq_len)
                sh = s_ref[b, sl, :]
                o_inter.append(jnp.dot(qh[rb, :], sh, preferred_element_type=F32))
                vmask = vb[:, sl] if nseq == 1 else _row_mask(vb[:, sl], b, seq_len)
                st = jnp.dot(kd_t[sl, :], _pad_rows(vmask, rp), preferred_element_type=F32)
                dec = jnp.exp2(tot[b * seq_len:b * seq_len + 1, sl])
                dec_col = jnp.broadcast_to(dec, (HG_DK, HG_DK)).T
                s_ref[b, sl, :] = dec_col * sh + st
            o = o + (o_inter[0] if nseq == 1 else jnp.concatenate(o_inter, axis=0))
            ms = jnp.sum(o * o, axis=1, keepdims=True) * (1.0 / HG_DK)
            outs.append(o * lax.rsqrt(ms + EPS))
        y_ref[rows, :] = (jnp.concatenate(outs, axis=1) * normg_ref[...] * gate).astype(y_ref.dtype)

    @pl.when(zi == pl.num_programs(1) - 1)
    def _():
        sout_ref[...] = s_ref[...]


def _gla(proj, row0, s0, p, nb, seq):
    seq_len = GLA_CHUNK if seq % GLA_CHUNK == 0 else seq
    nseq = 1 if seq_len == GLA_CHUNK else max(1, min(nb, STEP_ROWS // seq_len))
    nsub = GLA_SUBCHUNKS if nseq == 1 and (seq // seq_len) % GLA_SUBCHUNKS == 0 else 1
    nz = seq // (seq_len * nsub)
    r = nseq * seq_len * nsub
    r0 = row0 // r
    rowblk = lambda col: pl.BlockSpec((r, D), lambda b, z: (r0 + b * nz + z, col))
    const = lambda shape: pl.BlockSpec(shape, lambda b, z: tuple(0 for _ in shape))
    per_b = lambda shape: pl.BlockSpec((nseq,) + shape, lambda b, z: (b, 0, 0))
    return pl.pallas_call(
        functools.partial(_gla_kernel, nseq=nseq, seq_len=seq_len, nsub=nsub),
        grid=(nb // nseq, nz),
        in_specs=[rowblk(0), rowblk(1), rowblk(2), rowblk(3), per_b((D, HG_DK)), const((1, D)), const((1, D))],
        out_specs=[pl.BlockSpec((r, D), lambda b, z: (b * nz + z, 0)), per_b((D, HG_DK))],
        out_shape=[jax.ShapeDtypeStruct((nb * seq, D), BF16), jax.ShapeDtypeStruct((nb, D, HG_DK), F32)],
        scratch_shapes=[pltpu.VMEM((nseq, D, HG_DK), F32)],
        compiler_params=pltpu.CompilerParams(
            dimension_semantics=("parallel", "arbitrary"), vmem_limit_bytes=VMEM_LIMIT),
        name="gla",
    )(proj, proj, proj, proj, s0, p["lb"], p["hgrn_norm_g"])


def _layer_norm(x, g, b):
    mu = jnp.mean(x, axis=1, keepdims=True)
    xc = x - mu
    var = jnp.mean(xc * xc, axis=1, keepdims=True)
    return xc * lax.rsqrt(var + EPS) * g + b


def _gate_kernel(yap_ref, yas_ref, ybp_ref, ybs_ref, ga_ref, gb_ref, wa_ref, wb_ref, o_ref, *, n_prompt):
    is_prompt = pl.program_id(0) < n_prompt
    ya = jnp.where(is_prompt, yap_ref[...], yas_ref[...])
    yb = jnp.where(is_prompt, ybp_ref[...], ybs_ref[...])
    a = jnp.dot(ya, wa_ref[...], preferred_element_type=F32)
    b = jnp.dot(yb, wb_ref[...], preferred_element_type=F32)
    ga = _sigmoid(ga_ref[...].astype(F32))
    gb = _sigmoid(gb_ref[...].astype(F32))
    o_ref[...] = (ga * a + gb * b).astype(o_ref.dtype)


def _gate(ya_p, ya_s, yb_p, yb_s, proj, p):
    tm = GATE_TM if ya_p.shape[0] % GATE_TM == 0 and ya_s.shape[0] % GATE_TM == 0 else MERGE_TM
    tn = MERGE_TN
    n_p = ya_p.shape[0] // tm
    t = ya_p.shape[0] + ya_s.shape[0]
    prow = pl.BlockSpec((tm, D), lambda i, j: (jnp.minimum(i, n_p - 1), 0))
    srow = pl.BlockSpec((tm, D), lambda i, j: (jnp.maximum(i - n_p, 0), 0))
    gcol = lambda seg: pl.BlockSpec((tm, tn), lambda i, j: (i, seg * (D // tn) + j))
    wcol = pl.BlockSpec((D, tn), lambda i, j: (0, j))
    return pl.pallas_call(
        functools.partial(_gate_kernel, n_prompt=n_p),
        grid=(t // tm, D // tn),
        in_specs=[prow, srow, prow, srow, gcol(4), gcol(5), wcol, wcol],
        out_specs=pl.BlockSpec((tm, tn), lambda i, j: (i, j)),
        out_shape=jax.ShapeDtypeStruct((t, D), BF16),
        compiler_params=pltpu.CompilerParams(
            dimension_semantics=("parallel", "arbitrary"), vmem_limit_bytes=VMEM_LIMIT),
        name="gate",
    )(ya_p, ya_s, yb_p, yb_s, proj, proj, p["w_a"], p["w_b"])


def _merge_kernel(m_ref, xp_ref, xs_ref, wo_ref, g1_ref, b1_ref, wrh_ref, wrl_ref, br_ref,
                  x1_ref, route_ref, *, n_prompt):
    x = jnp.where(pl.program_id(0) < n_prompt, xp_ref[...], xs_ref[...])
    mix = jnp.dot(m_ref[...], wo_ref[...], preferred_element_type=F32)
    x1 = _layer_norm(ALPHA * x + mix, g1_ref[...], b1_ref[...])
    _to_token_tiles(x1_ref, x1)
    hi = x1.astype(BF16)
    lo = (x1 - hi.astype(F32)).astype(BF16)
    logits = (jnp.dot(hi, wrh_ref[...], preferred_element_type=F32)
              + jnp.dot(lo, wrh_ref[...], preferred_element_type=F32)
              + jnp.dot(hi, wrl_ref[...], preferred_element_type=F32) + br_ref[...])
    lane = lax.broadcasted_iota(jnp.int32, logits.shape, 1).astype(F32)
    gl = jnp.where(lane < N_GROUPS, logits, NEG)
    gmax = jnp.max(gl, axis=1, keepdims=True)
    gsel = jnp.min(jnp.where(gl == gmax, lane, 1e9), axis=1, keepdims=True)
    p_grp = 1.0 / jnp.sum(jnp.exp(gl - gmax), axis=1, keepdims=True)
    lo = N_GROUPS + gsel * EPG
    el = jnp.where((lane >= lo) & (lane < lo + EPG), logits, NEG)
    v0 = jnp.max(el, axis=1, keepdims=True)
    i0 = jnp.min(jnp.where(el == v0, lane, 1e9), axis=1, keepdims=True)
    el2 = jnp.where(lane == i0, NEG, el)
    v1 = jnp.max(el2, axis=1, keepdims=True)
    i1 = jnp.min(jnp.where(el2 == v1, lane, 1e9), axis=1, keepdims=True)
    e1 = jnp.exp(v1 - v0)
    p0 = 1.0 / (1.0 + e1)
    p1 = e1 * p0
    route = jnp.where(lane == 0, i0 - N_GROUPS,
                      jnp.where(lane == 1, i1 - N_GROUPS,
                                jnp.where(lane == 2, p_grp * p0,
                                          jnp.where(lane == 3, p_grp * p1, 0.0))))
    route_ref[...] = route


def _merge(merged, x_p, x_s, p):
    tm = MERGE_TM
    n_p = x_p.shape[0] // tm
    t = x_p.shape[0] + x_s.shape[0]
    row = pl.BlockSpec((tm, D), lambda i: (i, 0))
    prow = pl.BlockSpec((tm, D), lambda i: (jnp.minimum(i, n_p - 1), 0))
    srow = pl.BlockSpec((tm, D), lambda i: (jnp.maximum(i - n_p, 0), 0))
    res = lambda shape: pl.BlockSpec(shape, lambda i: (0, 0), pipeline_mode=pl.Buffered(1))
    wr_hi = p["wr"].astype(BF16)
    wr_lo = (p["wr"] - wr_hi.astype(F32)).astype(BF16)
    return pl.pallas_call(
        functools.partial(_merge_kernel, n_prompt=n_p),
        grid=(t // tm,),
        in_specs=[row, prow, srow, res((D, D)), res((1, D)), res((1, D)),
                  res((D, LANES)), res((D, LANES)), res((1, LANES))],
        out_specs=[pl.BlockSpec((tm * ROW_TILES, LANES), lambda i: (i, 0)),
                   pl.BlockSpec((tm, LANES), lambda i: (i, 0))],
        out_shape=[jax.ShapeDtypeStruct((t * ROW_TILES, LANES), F32), jax.ShapeDtypeStruct((t, LANES), F32)],
        compiler_params=pltpu.CompilerParams(
            dimension_semantics=("parallel",), vmem_limit_bytes=VMEM_LIMIT),
        name="merge",
    )(merged, x_p, x_s, p["w_out"], p["ln1_g"], p["ln1_b"], wr_hi, wr_lo, p["br"])


def _row_gather(src_hbm, dst, sem, idx_ref, base, n, unroll=8):
    def body(j, carry):
        r = idx_ref[base + j]
        pltpu.make_async_copy(src_hbm.at[pl.ds(pl.multiple_of(r * ROW_TILES, ROW_TILES), ROW_TILES), :],
                              dst.at[pl.ds(j * ROW_TILES, ROW_TILES), :], sem).start()
        return carry
    lax.fori_loop(0, n, body, 0, unroll=unroll)


def _expert_kernel(tstart_ref, ntile_ref, nv_ref, src_ref, x1_hbm, w1_hbm, w3_hbm, w2_hbm, es_hbm,
                   xbuf, obuf, gsem, osem, w1f, w3f, w2f, wsem, w1b, w3b, w2b, *, tm, n_tiles):
    e = pl.program_id(0)
    n_e = ntile_ref[e]
    g0 = tstart_ref[e]
    nv = nv_ref[0]
    rows = tm * ROW_TILES

    def gather_wait(s):
        pltpu.make_async_copy(x1_hbm.at[pl.ds(0, rows), :], xbuf.at[s], gsem.at[s]).wait()

    def out_copy(g):
        return pltpu.make_async_copy(obuf, es_hbm.at[pl.ds(pl.multiple_of(g * rows, rows), rows), :], osem)

    def weight_copies(ex, s):
        return [pltpu.make_async_copy(w_hbm.at[ex], w_f.at[s], wsem.at[s])
                for w_hbm, w_f in ((w1_hbm, w1f), (w3_hbm, w3f), (w2_hbm, w2f))]

    wslot = lax.rem(e, 2)

    @pl.when(e == 0)
    def _():
        for g in range(GATHER_SLOTS - 1):
            _row_gather(x1_hbm, xbuf.at[g], gsem.at[g], src_ref, g * tm, tm)
        for c in weight_copies(0, 0):
            c.start(priority=1)

    @pl.when(e + 1 < pl.num_programs(0))
    def _():
        for c in weight_copies(e + 1, 1 - wslot):
            c.start(priority=1)

    for c in weight_copies(e, wslot):
        c.wait()

    @pl.when(n_e > 0)
    def _():
        w1b[...] = w1f[wslot].astype(BF16)
        w3b[...] = w3f[wslot].astype(BF16)
        w2b[...] = w2f[wslot].astype(BF16)

    def tile(k, carry):
        g = g0 + k
        slot = lax.rem(g, GATHER_SLOTS)
        ahead = lax.rem(g + GATHER_SLOTS - 1, GATHER_SLOTS)
        _row_gather(x1_hbm, xbuf.at[ahead], gsem.at[ahead], src_ref, (g + GATHER_SLOTS - 1) * tm, tm, unroll=True)
        gather_wait(slot)
        xb = _from_token_tiles(xbuf.at[slot]).astype(BF16)
        h1 = jnp.dot(xb, w1b[...], preferred_element_type=F32)
        h3 = jnp.dot(xb, w3b[...], preferred_element_type=F32)
        h = (_silu(h1) * h3).astype(BF16)
        out = jnp.dot(h, w2b[...], preferred_element_type=F32)

        @pl.when(g > 0)
        def _():
            out_copy(g).wait()
        _to_token_tiles(obuf, out)
        out_copy(g).start()
        return carry

    lax.fori_loop(0, n_e, tile, 0)

    @pl.when(e == pl.num_programs(0) - 1)
    def _():
        for extra in range(GATHER_SLOTS - 1):
            gather_wait(lax.rem(nv + extra, GATHER_SLOTS))
        out_copy(0).wait()
        obuf[...] = jnp.zeros_like(obuf)

        def zero_tile(g, carry):
            out_copy(g).start()
            out_copy(g).wait()
            return carry

        lax.fori_loop(nv, n_tiles, zero_tile, 0)


def _experts(x1, tstart, ntile, nv, src, w1, w3, w2, n_tiles, tm):
    any_spec = pl.BlockSpec(memory_space=pl.ANY)
    return pl.pallas_call(
        functools.partial(_expert_kernel, tm=tm, n_tiles=n_tiles),
        grid_spec=pltpu.PrefetchScalarGridSpec(
            num_scalar_prefetch=4,
            grid=(N_EXPERTS,),
            in_specs=[any_spec, any_spec, any_spec, any_spec],
            out_specs=any_spec,
            scratch_shapes=[pltpu.VMEM((GATHER_SLOTS, tm * ROW_TILES, LANES), F32),
                            pltpu.VMEM((tm * ROW_TILES, LANES), F32),
                            pltpu.SemaphoreType.DMA((GATHER_SLOTS,)), pltpu.SemaphoreType.DMA(()),
                            pltpu.VMEM((2, D, D_EXPERT), F32), pltpu.VMEM((2, D, D_EXPERT), F32),
                            pltpu.VMEM((2, D_EXPERT, D), F32), pltpu.SemaphoreType.DMA((2,)),
                            pltpu.VMEM((D, D_EXPERT), BF16), pltpu.VMEM((D, D_EXPERT), BF16),
                            pltpu.VMEM((D_EXPERT, D), BF16)]),
        out_shape=jax.ShapeDtypeStruct((n_tiles * tm * ROW_TILES, LANES), F32),
        compiler_params=pltpu.CompilerParams(
            dimension_semantics=("arbitrary",), vmem_limit_bytes=VMEM_LIMIT, disable_bounds_checks=True),
        name="experts",
    )(tstart, ntile, nv, src, x1, w1, w3, w2)


def _combine_kernel(dest_ref, es_hbm, x1_ref, route_ref, g2_ref, b2_ref, op_ref, os_ref, gbuf, sems,
                    *, tm, n_prompt):
    i = pl.program_id(0)
    n = pl.num_programs(0)
    slot = lax.rem(i, 2)

    def start(step, s):
        for k in range(2):
            _row_gather(es_hbm, gbuf.at[s, k], sems.at[s], dest_ref, (k * n + step) * tm, tm)

    @pl.when(i == 0)
    def _():
        start(0, 0)

    @pl.when(i + 1 < n)
    def _():
        start(i + 1, 1 - slot)

    for k in range(2):
        pltpu.make_async_copy(es_hbm.at[pl.ds(0, tm * ROW_TILES), :], gbuf.at[slot, k], sems.at[slot]).wait()
    route = route_ref[...]
    w0 = route[:, 2:3]
    w1 = route[:, 3:4]
    moe = w0 * _from_token_tiles(gbuf.at[slot, 0]) + w1 * _from_token_tiles(gbuf.at[slot, 1])
    out = _layer_norm(ALPHA * _from_token_tiles(x1_ref) + moe, g2_ref[...], b2_ref[...])

    @pl.when(i < n_prompt)
    def _():
        op_ref[...] = out

    @pl.when(i >= n_prompt)
    def _():
        os_ref[...] = out


def _combine(es, dest_km, x1, route, p, tp):
    t = x1.shape[0] // ROW_TILES
    tm = ROW_TM
    n_p = tp // tm
    return pl.pallas_call(
        functools.partial(_combine_kernel, tm=tm, n_prompt=n_p),
        grid_spec=pltpu.PrefetchScalarGridSpec(
            num_scalar_prefetch=1,
            grid=(t // tm,),
            in_specs=[pl.BlockSpec(memory_space=pl.ANY),
                      pl.BlockSpec((tm * ROW_TILES, LANES), lambda i, d: (i, 0)),
                      pl.BlockSpec((tm, LANES), lambda i, d: (i, 0)),
                      pl.BlockSpec((1, D), lambda i, d: (0, 0)),
                      pl.BlockSpec((1, D), lambda i, d: (0, 0))],
            out_specs=[pl.BlockSpec((tm, D), lambda i, d: (jnp.minimum(i, n_p - 1), 0)),
                       pl.BlockSpec((tm, D), lambda i, d: (jnp.maximum(i - n_p, 0), 0))],
            scratch_shapes=[pltpu.VMEM((2, 2, tm * ROW_TILES, LANES), F32), pltpu.SemaphoreType.DMA((2,))]),
        out_shape=[jax.ShapeDtypeStruct((tp, D), F32), jax.ShapeDtypeStruct((t - tp, D), F32)],
        compiler_params=pltpu.CompilerParams(
            dimension_semantics=("arbitrary",), vmem_limit_bytes=VMEM_LIMIT, disable_bounds_checks=True),
        name="combine",
    )(dest_km, es, x1, route, p["ln2_g"], p["ln2_b"])


def _route_meta(e_flat, tm, n_tiles):
    a = e_flat.shape[0]
    ids = jnp.arange(N_EXPERTS, dtype=jnp.int32)
    onehot = (e_flat[:, None] == ids[None, :]).astype(jnp.int32)
    csum = jnp.cumsum(onehot, axis=0)
    rank = jnp.sum((csum - onehot) * onehot, axis=1)
    counts = csum[-1]
    tiles_per = (counts + tm - 1) // tm
    tile_end = jnp.cumsum(tiles_per)
    tile_start = tile_end - tiles_per
    nv = tile_end[-1]
    dest = tile_start[e_flat] * tm + rank
    src = jnp.zeros((n_tiles * tm,), jnp.int32).at[dest].set(
        jnp.arange(a, dtype=jnp.int32) // 2, unique_indices=True, mode="promise_in_bounds")
    return (tile_start.astype(jnp.int32), tiles_per.astype(jnp.int32), nv.reshape(1).astype(jnp.int32), src,
            dest.astype(jnp.int32))


def kernel(x_prompt, x_sample, state_ssm, state_conv, state_hgrn, w_in, conv_w, conv_b, dt_bias, a_log, d_skip, ssm_norm_g, hgrn_lb_logits, hgrn_norm_g, w_a, w_b, w_out, ln1_g, ln1_b, router_g_w, router_g_b, router_e_w, router_e_b, exp_w1, exp_w3, exp_w2, ln2_g, ln2_b):
    nbp, seqp, _ = x_prompt.shape
    nbs, seqs, _ = x_sample.shape
    tp, ts = nbp * seqp, nbs * seqs
    t = tp + ts
    l = 0
    x_p, x_s = x_prompt.reshape(tp, D), x_sample.reshape(ts, D)

    wt = jnp.transpose(w_in[l])
    o_dt = D + D + 2 * BCW
    o_q = o_dt + SSM_HEADS
    xb, dt_raw = _cast_rows(x_p, x_s, wt, o_dt, MERGE_TM)
    dt_raw = jnp.pad(dt_raw, ((0, 0), (0, LANES - SSM_HEADS)))
    tm = PROJ_TM if t % PROJ_TM == 0 else 1024
    proj_a = _project(xb, wt, 0, o_dt, tm, PROJ_TN, BF16)
    proj_b = _project(xb, wt, o_q, 6 * D, tm, PROJ_TN, BF16)

    lb_all = jnp.cumsum(jax.nn.softmax(hgrn_lb_logits.astype(F32), axis=0), axis=0)
    head_of = np.arange(D) // SSM_P
    pad128 = lambda v: jnp.pad(v, (0, LANES - v.shape[0])).reshape(1, LANES)
    p = {
        "cwx": conv_w[l][:, :D], "cwbc": conv_w[l][:, D:],
        "cbx": conv_b[l][:D].reshape(1, D), "cbbc": conv_b[l][D:].reshape(1, 2 * BCW),
        "dtb": pad128(dt_bias[l]), "alog": a_log[l], "alog_e": pad128(a_log[l]),
        "dskip_e": jnp.repeat(d_skip[l], SSM_P).reshape(1, D),
        "ssm_norm_g": ssm_norm_g[l].reshape(1, D),
        "e64": jnp.asarray(np.arange(LANES)[:, None] == head_of[None, :], dtype=BF16),
        "lb": lb_all[l].reshape(1, D), "hgrn_norm_g": hgrn_norm_g[l].reshape(1, D),
        "w_a": w_a[l].astype(BF16), "w_b": w_b[l].astype(BF16), "w_out": w_out[l].astype(BF16),
        "ln1_g": ln1_g[l].reshape(1, D), "ln1_b": ln1_b[l].reshape(1, D),
        "ln2_g": ln2_g[l].reshape(1, D), "ln2_b": ln2_b[l].reshape(1, D),
        "wr": jnp.pad(jnp.concatenate([router_g_w[l], router_e_w[l]], axis=1),
                      ((0, 0), (0, LANES - N_GROUPS - N_EXPERTS))),
        "br": pad128(jnp.concatenate([router_g_b[l], router_e_b[l]])),
    }

    ya_p, ssm_p, conv_p = _ssd(proj_a, dt_raw, 0, jnp.zeros((nbp, D, SSM_N), F32),
                               jnp.zeros((nbp, CONV_K - 1, D + 2 * BCW), F32), p, nbp, seqp)
    ya_s, ssm_s, conv_s = _ssd(proj_a, dt_raw, tp, state_ssm[l].reshape(nbs, D, SSM_N), state_conv[l],
                               p, nbs, seqs)
    yb_p, hg_p = _gla(proj_b, 0, jnp.zeros((nbp, D, HG_DK), F32), p, nbp, seqp)
    yb_s, hg_s = _gla(proj_b, tp, state_hgrn[l].reshape(nbs, D, HG_DK), p, nbs, seqs)

    merged = _gate(ya_p, ya_s, yb_p, yb_s, proj_b, p)
    x1, route = _merge(merged, x_p, x_s, p)

    e_flat = route[:, :2].astype(jnp.int32).reshape(-1)
    n_tiles = (2 * t) // MOE_TM + N_EXPERTS + GATHER_SLOTS - 1
    tstart, ntile, nv, src, dest = _route_meta(e_flat, MOE_TM, n_tiles)
    es = _experts(x1, tstart, ntile, nv, src, exp_w1[l], exp_w3[l], exp_w2[l], n_tiles, MOE_TM)
    dest_km = dest.reshape(t // ROW_TM, ROW_TM, 2).transpose(2, 0, 1).reshape(-1)
    x2_p, x2_s = _combine(es, dest_km, x1, route, p, tp)

    y_prompt = x2_p.reshape(nbp, seqp, D)
    y_sample = x2_s.reshape(nbs, seqs, D)
    return (y_prompt, y_sample,
            ssm_p.reshape(1, nbp, SSM_HEADS, SSM_P, SSM_N), conv_p[None],
            hg_p.reshape(1, nbp, HG_HEADS, HG_DK, HG_DK),
            ssm_s.reshape(1, nbs, SSM_HEADS, SSM_P, SSM_N), conv_s[None],
            hg_s.reshape(1, nbs, HG_HEADS, HG_DK, HG_DK))
```

```python
import functools

import jax
import jax.numpy as jnp
import numpy as np
from jax import lax
from jax.experimental import pallas as pl
from jax.experimental.pallas import tpu as pltpu

F32 = jnp.float32
BF16 = jnp.bfloat16

D = 2048
SSM_HEADS = 32
SSM_P = 64
SSM_N = 128
SSM_G = 4
GW = D // SSM_G
BCW = SSM_G * SSM_N
CONV_K = 4
HG_HEADS = 16
HG_DK = 128
N_GROUPS = 4
EPG = 8
N_EXPERTS = 32
D_EXPERT = 512
EPS = 1e-5
ALPHA = 2.0 ** 0.25
NEG = -1e30
LOG2E = 1.4426950408889634
LANES = 128
ROW_TILES = D // LANES

VMEM_LIMIT = 56 * 1024 * 1024
SSD_CHUNK = 128
GLA_CHUNK = 64
GLA_SUBCHUNKS = 2
STEP_ROWS = 64
PROJ_TM = 2304
PROJ_TN = 512
MOE_TM = 128
GATHER_SLOTS = 4
ROW_TM = 256
MERGE_TM = 512
GATE_TM = 1024
MERGE_TN = 512
CARRY = 8


def _sigmoid(x):
    return 0.5 * jnp.tanh(0.5 * x) + 0.5


def _silu(x):
    h = 0.5 * x
    return h * jnp.tanh(h) + h


def _softplus(x):
    return jnp.maximum(x, 0.0) + jnp.log(1.0 + jnp.exp(-jnp.abs(x)))


def _split3(x):
    hi = x.astype(BF16)
    r = x - hi.astype(F32)
    mid = r.astype(BF16)
    lo = (r - mid.astype(F32)).astype(BF16)
    return hi, mid, lo


def _dot3_rhs(m_bf16, x):
    return sum(jnp.dot(m_bf16, part, preferred_element_type=F32) for part in _split3(x))


def _dot3_lhs(x, m_bf16):
    return sum(jnp.dot(part, m_bf16, preferred_element_type=F32) for part in _split3(x))


def _seq_masks(r, seq_len):
    shift = seq_len.bit_length() - 1
    row = lax.broadcasted_iota(jnp.int32, (r, r), 0)
    col = lax.broadcasted_iota(jnp.int32, (r, r), 1)
    same = lax.shift_right_logical(row, shift) == lax.shift_right_logical(col, shift)
    return same & (row >= col), same & (row <= col)


def _seq_totals(cum, nseq, seq_len):
    w = cum.shape[1]
    parts = [jnp.broadcast_to(cum[(b + 1) * seq_len - 1:(b + 1) * seq_len, :], (seq_len, w)) for b in range(nseq)]
    return parts[0] if nseq == 1 else jnp.concatenate(parts, axis=0)


def _pad_rows(x, rows):
    if x.shape[0] == rows:
        return x
    return jnp.concatenate([x, jnp.zeros((rows - x.shape[0], x.shape[1]), x.dtype)], axis=0)


def _to_token_tiles(ref, val):
    rows = val.shape[0]
    for s in range(ROW_TILES):
        ref[pl.ds(s, rows, stride=ROW_TILES), :] = val[:, s * LANES:(s + 1) * LANES]


def _from_token_tiles(ref):
    rows = ref.shape[0] // ROW_TILES
    return jnp.concatenate([ref[pl.ds(s, rows, stride=ROW_TILES), :] for s in range(ROW_TILES)], axis=1)


def _row_mask(x, b, seq_len):
    row = lax.broadcasted_iota(jnp.int32, x.shape, 0)
    return jnp.where((row >= b * seq_len) & (row < (b + 1) * seq_len), x, jnp.zeros_like(x))


def _cast_kernel(xp_ref, xs_ref, wdt_ref, o_ref, dt_ref, *, n_prompt):
    xb = jnp.where(pl.program_id(0) < n_prompt, xp_ref[...], xs_ref[...]).astype(BF16)
    o_ref[...] = xb
    dt_ref[...] = lax.dot_general(xb, wdt_ref[...].astype(BF16), (((1,), (1,)), ((), ())),
                                  preferred_element_type=F32)


def _cast_rows(x_p, x_s, wt, dt_row0, tm):
    k = x_p.shape[1]
    n_p = x_p.shape[0] // tm
    m = x_p.shape[0] + x_s.shape[0]
    return pl.pallas_call(
        functools.partial(_cast_kernel, n_prompt=n_p),
        grid=(m // tm,),
        in_specs=[pl.BlockSpec((tm, k), lambda i: (jnp.minimum(i, n_p - 1), 0)),
                  pl.BlockSpec((tm, k), lambda i: (jnp.maximum(i - n_p, 0), 0)),
                  pl.BlockSpec((SSM_HEADS, k), lambda i: (dt_row0 // SSM_HEADS, 0))],
        out_specs=[pl.BlockSpec((tm, k), lambda i: (i, 0)), pl.BlockSpec((tm, SSM_HEADS), lambda i: (i, 0))],
        out_shape=[jax.ShapeDtypeStruct((m, k), BF16), jax.ShapeDtypeStruct((m, SSM_HEADS), F32)],
        compiler_params=pltpu.CompilerParams(dimension_semantics=("parallel",), vmem_limit_bytes=VMEM_LIMIT),
        name="cast",
    )(x_p, x_s, wt)


def _mm_nt_kernel(x_ref, wt_ref, o_ref):
    w = wt_ref[...].astype(BF16)
    o_ref[...] = lax.dot_general(x_ref[...], w, (((1,), (1,)), ((), ())),
                                 preferred_element_type=F32).astype(o_ref.dtype)


def _project(x, wt, row0, n, tm, tn, out_dtype):
    m, k = x.shape
    assert n % tn == 0 and m % tm == 0 and row0 % 8 == 0
    if row0 % tn == 0:
        w_spec = pl.BlockSpec((tn, k), lambda i, j: (row0 // tn + j, 0))
    else:
        w_spec = pl.BlockSpec((pl.Element(tn), pl.Element(k)),
                              lambda i, j: (pl.multiple_of(row0 + j * tn, 8), 0))
    return pl.pallas_call(
        _mm_nt_kernel,
        grid=(m // tm, n // tn),
        in_specs=[pl.BlockSpec((tm, k), lambda i, j: (i, 0)), w_spec],
        out_specs=pl.BlockSpec((tm, tn), lambda i, j: (i, j)),
        out_shape=jax.ShapeDtypeStruct((m, n), out_dtype),
        compiler_params=pltpu.CompilerParams(
            dimension_semantics=("parallel", "arbitrary"), vmem_limit_bytes=VMEM_LIMIT),
        name="proj",
    )(x, wt)


def _conv_silu(buf_ref, carry0_ref, u, w_ref, b_ref, nseq, seq_len, first):
    ch = u.shape[1]

    @pl.when(first)
    def _():
        buf_ref[:, 0:CARRY, :] = carry0_ref[...]
    u3 = u.reshape(nseq, seq_len, ch)
    buf_ref[:, CARRY:CARRY + seq_len, :] = u3
    acc = b_ref[...] + w_ref[CONV_K - 1:CONV_K, :] * u3
    for k in range(CONV_K - 1):
        off = CARRY - (CONV_K - 1) + k
        acc = acc + w_ref[k:k + 1, :] * buf_ref[:, off:off + seq_len, :]
    tail = buf_ref[:, seq_len:seq_len + CARRY, :]
    buf_ref[:, 0:CARRY, :] = tail
    return _silu(acc).reshape(nseq * seq_len, ch), tail


def _pair_cols(tile):
    lane = lax.broadcasted_iota(jnp.int32, tile.shape, 1)
    swapped = pltpu.roll(tile, SSM_P, axis=1)
    return jnp.where(lane < SSM_P, tile, swapped), jnp.where(lane < SSM_P, swapped, tile)


def _ssd_kernel(xp_ref, bcp_ref, z_ref, dtr_ref, dtrt_ref, h0_ref, cx0_ref, cbc0_ref,
                cwx_ref, cwbc_ref, cbx_ref, cbbc_ref, dtb_ref, dtbt_ref, alogt_ref, aloge_ref,
                dskip_ref, normg_ref, e64_ref,
                y_ref, hout_ref, ctx_ref, ctbc_ref,
                xbuf, bcbuf, h_ref, *, nseq, seq_len):
    r = nseq * seq_len
    rp = max(r, LANES)
    zi = pl.program_id(1)
    first = zi == 0

    @pl.when(first)
    def _():
        h_ref[...] = h0_ref[...]

    xs, tail_x = _conv_silu(xbuf, cx0_ref, xp_ref[...].astype(F32), cwx_ref, cbx_ref, nseq, seq_len, first)
    bc, tail_bc = _conv_silu(bcbuf, cbc0_ref, bcp_ref[...].astype(F32), cwbc_ref, cbbc_ref, nseq, seq_len, first)
    ctx_ref[...] = tail_x
    ctbc_ref[...] = tail_bc

    causal, anti = _seq_masks(r, seq_len)
    dt = _softplus(dtr_ref[...] + dtb_ref[...])
    dt_e = _dot3_lhs(dt, e64_ref[...])
    cum = _dot3_rhs(causal.astype(BF16), dt * (-LOG2E * jnp.exp(aloge_ref[...])))
    cum_e = _dot3_lhs(cum, e64_ref[...])
    loga_t = _softplus(dtrt_ref[0] + dtbt_ref[...]) * (-LOG2E * jnp.exp(alogt_ref[...]))
    cum_t = _dot3_lhs(loga_t, anti.astype(BF16))
    tot_e = _seq_totals(cum_e, nseq, seq_len)
    xdt = xs * dt_e
    in_scale = jnp.exp2(cum_e)
    xw = xdt * jnp.exp2(tot_e - cum_e)
    xw_t = _pad_rows(xw, rp).T.astype(BF16)
    lane = lax.broadcasted_iota(jnp.int32, (r, LANES), 1)

    y_groups = []
    for g in range(SSM_G):
        bg = bc[:, g * SSM_N:(g + 1) * SSM_N]
        cg = bc[:, BCW + g * SSM_N:BCW + (g + 1) * SSM_N]
        cb = lax.dot_general(cg, bg, (((1,), (1,)), ((), ())), preferred_element_type=F32)
        rows = slice(g * GW, (g + 1) * GW)
        pieces = []
        for j in range(GW // LANES):
            col0 = g * GW + j * LANES
            xpair = xdt[:, col0:col0 + LANES]
            ms, rhs = [], []
            for half, colb in enumerate(_pair_cols(cum_e[:, col0:col0 + LANES])):
                head = col0 // SSM_P + half
                seg = colb[:, :r] - cum_t[head:head + 1, :]
                ms.append((cb * jnp.exp2(jnp.where(causal, seg, NEG))).astype(BF16))
                rhs.append(jnp.where(lane >= SSM_P if half else lane < SSM_P, xpair, 0.0).astype(BF16))
            if r % LANES == 0:
                pieces.append(jnp.dot(jnp.concatenate(ms, axis=1), jnp.concatenate(rhs, axis=0),
                                      preferred_element_type=F32))
            else:
                pieces.append(jnp.dot(ms[0], rhs[0], preferred_element_type=F32)
                              + jnp.dot(ms[1], rhs[1], preferred_element_type=F32))
        y_intra = jnp.concatenate(pieces, axis=1)
        y_inter = []
        for b in range(nseq):
            rb = slice(b * seq_len, (b + 1) * seq_len)
            hg = h_ref[b, rows, :]
            y_inter.append(lax.dot_general(cg[rb, :], hg, (((1,), (1,)), ((), ())),
                                           preferred_element_type=F32))
            bmask = bg if nseq == 1 else _row_mask(bg, b, seq_len)
            st = jnp.dot(xw_t[rows, :], _pad_rows(bmask, rp).astype(BF16), preferred_element_type=F32)
            dec8 = jnp.exp2(tot_e[b * seq_len:b * seq_len + 8, rows])
            for j in range(GW // LANES):
                for half, dcol in enumerate(_pair_cols(dec8[:, j * LANES:(j + 1) * LANES])):
                    h8 = 2 * j + half
                    hr = slice(g * GW + h8 * SSM_P, g * GW + (h8 + 1) * SSM_P)
                    h_ref[b, hr, :] = dcol[0:1, :] * h_ref[b, hr, :] + st[h8 * SSM_P:(h8 + 1) * SSM_P, :]
        y_inter = y_inter[0] if nseq == 1 else jnp.concatenate(y_inter, axis=0)
        y_groups.append(y_intra + y_inter * in_scale[:, rows])
    y = jnp.concatenate(y_groups, axis=1) + dskip_ref[...] * xs
    yz = y * _silu(z_ref[...].astype(F32))
    outs = []
    for g in range(SSM_G):
        blk = yz[:, g * GW:(g + 1) * GW]
        ms = jnp.sum(blk * blk, axis=1, keepdims=True) * (1.0 / GW)
        outs.append(blk * lax.rsqrt(ms + EPS))
    y_ref[...] = (jnp.concatenate(outs, axis=1) * normg_ref[...]).astype(y_ref.dtype)

    @pl.when(zi == pl.num_programs(1) - 1)
    def _():
        hout_ref[...] = h_ref[...]


def _ssd(proj, dt_raw, row0, h0, conv0, p, nb, seq):
    seq_len = SSD_CHUNK if seq % SSD_CHUNK == 0 else seq
    nseq = 1 if seq_len == SSD_CHUNK else max(1, min(nb, STEP_ROWS // seq_len))
    nz = seq // seq_len
    r = nseq * seq_len
    t = nb * seq
    r0 = row0 // r
    nblk = t // r
    dtrt = dt_raw[row0:row0 + t, :SSM_HEADS].reshape(nblk, r, SSM_HEADS).transpose(0, 2, 1)
    pad_rows = CARRY - (CONV_K - 1)
    cx0 = jnp.pad(conv0[:, :, :D], ((0, 0), (pad_rows, 0), (0, 0)))
    cbc0 = jnp.pad(conv0[:, :, D:], ((0, 0), (pad_rows, 0), (0, 0)))
    rowblk = lambda col: pl.BlockSpec((r, D), lambda b, z: (r0 + b * nz + z, col))
    const = lambda shape: pl.BlockSpec(shape, lambda b, z: tuple(0 for _ in shape))
    per_b = lambda shape: pl.BlockSpec((nseq,) + shape, lambda b, z: (b, 0, 0))
    y, hout, ctx, ctbc = pl.pallas_call(
        functools.partial(_ssd_kernel, nseq=nseq, seq_len=seq_len),
        grid=(nb // nseq, nz),
        in_specs=[rowblk(1),
                  pl.BlockSpec((r, 2 * BCW), lambda b, z: (r0 + b * nz + z, 4)),
                  rowblk(0),
                  pl.BlockSpec((r, LANES), lambda b, z: (r0 + b * nz + z, 0)),
                  pl.BlockSpec((1, SSM_HEADS, r), lambda b, z: (b * nz + z, 0, 0)),
                  per_b((D, SSM_N)), per_b((CARRY, D)), per_b((CARRY, 2 * BCW)),
                  const((CONV_K, D)), const((CONV_K, 2 * BCW)), const((1, D)), const((1, 2 * BCW)),
                  const((1, LANES)), const((SSM_HEADS, r)), const((SSM_HEADS, r)), const((1, LANES)),
                  const((1, D)), const((1, D)), const((LANES, D))],
        out_specs=[pl.BlockSpec((r, D), lambda b, z: (b * nz + z, 0)),
                   per_b((D, SSM_N)), per_b((CARRY, D)), per_b((CARRY, 2 * BCW))],
        out_shape=[jax.ShapeDtypeStruct((t, D), BF16),
                   jax.ShapeDtypeStruct((nb, D, SSM_N), F32),
                   jax.ShapeDtypeStruct((nb, CARRY, D), F32),
                   jax.ShapeDtypeStruct((nb, CARRY, 2 * BCW), F32)],
        scratch_shapes=[pltpu.VMEM((nseq, seq_len + CARRY, D), F32),
                        pltpu.VMEM((nseq, seq_len + CARRY, 2 * BCW), F32),
                        pltpu.VMEM((nseq, D, SSM_N), F32)],
        compiler_params=pltpu.CompilerParams(
            dimension_semantics=("parallel", "arbitrary"), vmem_limit_bytes=VMEM_LIMIT),
        name="ssd",
    )(proj, proj, proj, dt_raw, dtrt, h0, cx0, cbc0,
      p["cwx"], p["cwbc"], p["cbx"], p["cbbc"], p["dtb"],
      jnp.broadcast_to(p["dtb"][0, :SSM_HEADS, None], (SSM_HEADS, r)),
      jnp.broadcast_to(p["alog"][:, None], (SSM_HEADS, r)),
      p["alog_e"], p["dskip_e"], p["ssm_norm_g"], p["e64"])
    conv_new = jnp.concatenate([ctx[:, pad_rows:], ctbc[:, pad_rows:]], axis=-1)
    return y, hout, conv_new


def _gla_kernel(q_ref, f_ref, i_ref, g_ref, s0_ref, lb_ref, normg_ref, y_ref, sout_ref, s_ref,
                *, nseq, seq_len, nsub):
    r = nseq * seq_len
    rp = max(r, LANES)
    zi = pl.program_id(1)

    @pl.when(zi == 0)
    def _():
        s_ref[...] = s0_ref[...]

    lb = lb_ref[...]
    causal, _ = _seq_masks(r, seq_len)
    for sub in range(nsub):
        rows = slice(sub * r, (sub + 1) * r)
        f = lb + (1.0 - lb) * _sigmoid(f_ref[rows, :].astype(F32))
        k = 1.0 - f
        q = _silu(q_ref[rows, :].astype(F32))
        vb = i_ref[rows, :]
        bc = _dot3_rhs(causal.astype(BF16), jnp.log2(f))
        tot = _seq_totals(bc, nseq, seq_len)
        qe = q * jnp.exp2(bc)
        ke = k * jnp.exp2(-bc)
        kd_t = _pad_rows(ke * jnp.exp2(tot), rp).T.astype(BF16)
        gate = _silu(g_ref[rows, :].astype(F32))
        outs = []
        for h in range(HG_HEADS):
            sl = slice(h * HG_DK, (h + 1) * HG_DK)
            qh = qe[:, sl]
            attn = lax.dot_general(qh, ke[:, sl], (((1,), (1,)), ((), ())), preferred_element_type=F32)
            attn = jnp.where(causal, attn, 0.0)
            o = jnp.dot(attn.astype(BF16), vb[:, sl], preferred_element_type=F32)
            o_inter = []
            for b in range(nseq):
                rb = slice(b * seq_len, (b + 1) * seq_len)
                sh = s_ref[b, sl, :]
                o_inter.append(jnp.dot(qh[rb, :], sh, preferred_element_type=F32))
                vmask = vb[:, sl] if nseq == 1 else _row_mask(vb[:, sl], b, seq_len)
                st = jnp.dot(kd_t[sl, :], _pad_rows(vmask, rp), preferred_element_type=F32)
                dec = jnp.exp2(tot[b * seq_len:b * seq_len + 1, sl])
                dec_col = jnp.broadcast_to(dec, (HG_DK, HG_DK)).T
                s_ref[b, sl, :] = dec_col * sh + st
            o = o + (o_inter[0] if nseq == 1 else jnp.concatenate(o_inter, axis=0))
            ms = jnp.sum(o * o, axis=1, keepdims=True) * (1.0 / HG_DK)
            outs.append(o * lax.rsqrt(ms + EPS))
        y_ref[rows, :] = (jnp.concatenate(outs, axis=1) * normg_ref[...] * gate).astype(y_ref.dtype)

    @pl.when(zi == pl.num_programs(1) - 1)
    def _():
        sout_ref[...] = s_ref[...]


def _gla(proj, row0, s0, p, nb, seq):
    seq_len = GLA_CHUNK if seq % GLA_CHUNK == 0 else seq
    nseq = 1 if seq_len == GLA_CHUNK else max(1, min(nb, STEP_ROWS // seq_len))
    nsub = GLA_SUBCHUNKS if nseq == 1 and (seq // seq_len) % GLA_SUBCHUNKS == 0 else 1
    nz = seq // (seq_len * nsub)
    r = nseq * seq_len * nsub
    r0 = row0 // r
    rowblk = lambda col: pl.BlockSpec((r, D), lambda b, z: (r0 + b * nz + z, col))
    const = lambda shape: pl.BlockSpec(shape, lambda b, z: tuple(0 for _ in shape))
    per_b = lambda shape: pl.BlockSpec((nseq,) + shape, lambda b, z: (b, 0, 0))
    return pl.pallas_call(
        functools.partial(_gla_kernel, nseq=nseq, seq_len=seq_len, nsub=nsub),
        grid=(nb // nseq, nz),
        in_specs=[rowblk(0), rowblk(1), rowblk(2), rowblk(3), per_b((D, HG_DK)), const((1, D)), const((1, D))],
        out_specs=[pl.BlockSpec((r, D), lambda b, z: (b * nz + z, 0)), per_b((D, HG_DK))],
        out_shape=[jax.ShapeDtypeStruct((nb * seq, D), BF16), jax.ShapeDtypeStruct((nb, D, HG_DK), F32)],
        scratch_shapes=[pltpu.VMEM((nseq, D, HG_DK), F32)],
        compiler_params=pltpu.CompilerParams(
            dimension_semantics=("parallel", "arbitrary"), vmem_limit_bytes=VMEM_LIMIT),
        name="gla",
    )(proj, proj, proj, proj, s0, p["lb"], p["hgrn_norm_g"])


def _layer_norm(x, g, b):
    mu = jnp.mean(x, axis=1, keepdims=True)
    xc = x - mu
    var = jnp.mean(xc * xc, axis=1, keepdims=True)
    return xc * lax.rsqrt(var + EPS) * g + b


def _gate_kernel(yap_ref, yas_ref, ybp_ref, ybs_ref, ga_ref, gb_ref, wa_ref, wb_ref, o_ref, *, n_prompt):
    is_prompt = pl.program_id(0) < n_prompt
    ya = jnp.where(is_prompt, yap_ref[...], yas_ref[...])
    yb = jnp.where(is_prompt, ybp_ref[...], ybs_ref[...])
    a = jnp.dot(ya, wa_ref[...], preferred_element_type=F32)
    b = jnp.dot(yb, wb_ref[...], preferred_element_type=F32)
    ga = _sigmoid(ga_ref[...].astype(F32))
    gb = _sigmoid(gb_ref[...].astype(F32))
    o_ref[...] = (ga * a + gb * b).astype(o_ref.dtype)


def _gate(ya_p, ya_s, yb_p, yb_s, proj, p):
    tm = GATE_TM if ya_p.shape[0] % GATE_TM == 0 and ya_s.shape[0] % GATE_TM == 0 else MERGE_TM
    tn = MERGE_TN
    n_p = ya_p.shape[0] // tm
    t = ya_p.shape[0] + ya_s.shape[0]
    prow = pl.BlockSpec((tm, D), lambda i, j: (jnp.minimum(i, n_p - 1), 0))
    srow = pl.BlockSpec((tm, D), lambda i, j: (jnp.maximum(i - n_p, 0), 0))
    gcol = lambda seg: pl.BlockSpec((tm, tn), lambda i, j: (i, seg * (D // tn) + j))
    wcol = pl.BlockSpec((D, tn), lambda i, j: (0, j))
    return pl.pallas_call(
        functools.partial(_gate_kernel, n_prompt=n_p),
        grid=(t // tm, D // tn),
        in_specs=[prow, srow, prow, srow, gcol(4), gcol(5), wcol, wcol],
        out_specs=pl.BlockSpec((tm, tn), lambda i, j: (i, j)),
        out_shape=jax.ShapeDtypeStruct((t, D), BF16),
        compiler_params=pltpu.CompilerParams(
            dimension_semantics=("parallel", "arbitrary"), vmem_limit_bytes=VMEM_LIMIT),
        name="gate",
    )(ya_p, ya_s, yb_p, yb_s, proj, proj, p["w_a"], p["w_b"])


def _merge_kernel(m_ref, xp_ref, xs_ref, wo_ref, g1_ref, b1_ref, wrh_ref, wrl_ref, br_ref,
                  x1_ref, route_ref, *, n_prompt):
    x = jnp.where(pl.program_id(0) < n_prompt, xp_ref[...], xs_ref[...])
    mix = jnp.dot(m_ref[...], wo_ref[...], preferred_element_type=F32)
    x1 = _layer_norm(ALPHA * x + mix, g1_ref[...], b1_ref[...])
    _to_token_tiles(x1_ref, x1)
    hi = x1.astype(BF16)
    lo = (x1 - hi.astype(F32)).astype(BF16)
    logits = (jnp.dot(hi, wrh_ref[...], preferred_element_type=F32)
              + jnp.dot(lo, wrh_ref[...], preferred_element_type=F32)
              + jnp.dot(hi, wrl_ref[...], preferred_element_type=F32) + br_ref[...])
    lane = lax.broadcasted_iota(jnp.int32, logits.shape, 1).astype(F32)
    gl = jnp.where(lane < N_GROUPS, logits, NEG)
    gmax = jnp.max(gl, axis=1, keepdims=True)
    gsel = jnp.min(jnp.where(gl == gmax, lane, 1e9), axis=1, keepdims=True)
    p_grp = 1.0 / jnp.sum(jnp.exp(gl - gmax), axis=1, keepdims=True)
    lo = N_GROUPS + gsel * EPG
    el = jnp.where((lane >= lo) & (lane < lo + EPG), logits, NEG)
    v0 = jnp.max(el, axis=1, keepdims=True)
    i0 = jnp.min(jnp.where(el == v0, lane, 1e9), axis=1, keepdims=True)
    el2 = jnp.where(lane == i0, NEG, el)
    v1 = jnp.max(el2, axis=1, keepdims=True)
    i1 = jnp.min(jnp.where(el2 == v1, lane, 1e9), axis=1, keepdims=True)
    e1 = jnp.exp(v1 - v0)
    p0 = 1.0 / (1.0 + e1)
    p1 = e1 * p0
    route = jnp.where(lane == 0, i0 - N_GROUPS,
                      jnp.where(lane == 1, i1 - N_GROUPS,
                                jnp.where(lane == 2, p_grp * p0,
                                          jnp.where(lane == 3, p_grp * p1, 0.0))))
    route_ref[...] = route


def _merge(merged, x_p, x_s, p):
    tm = MERGE_TM
    n_p = x_p.shape[0] // tm
    t = x_p.shape[0] + x_s.shape[0]
    row = pl.BlockSpec((tm, D), lambda i: (i, 0))
    prow = pl.BlockSpec((tm, D), lambda i: (jnp.minimum(i, n_p - 1), 0))
    srow = pl.BlockSpec((tm, D), lambda i: (jnp.maximum(i - n_p, 0), 0))
    res = lambda shape: pl.BlockSpec(shape, lambda i: (0, 0), pipeline_mode=pl.Buffered(1))
    wr_hi = p["wr"].astype(BF16)
    wr_lo = (p["wr"] - wr_hi.astype(F32)).astype(BF16)
    return pl.pallas_call(
        functools.partial(_merge_kernel, n_prompt=n_p),
        grid=(t // tm,),
        in_specs=[row, prow, srow, res((D, D)), res((1, D)), res((1, D)),
                  res((D, LANES)), res((D, LANES)), res((1, LANES))],
        out_specs=[pl.BlockSpec((tm * ROW_TILES, LANES), lambda i: (i, 0)),
                   pl.BlockSpec((tm, LANES), lambda i: (i, 0))],
        out_shape=[jax.ShapeDtypeStruct((t * ROW_TILES, LANES), F32), jax.ShapeDtypeStruct((t, LANES), F32)],
        compiler_params=pltpu.CompilerParams(
            dimension_semantics=("parallel",), vmem_limit_bytes=VMEM_LIMIT),
        name="merge",
    )(merged, x_p, x_s, p["w_out"], p["ln1_g"], p["ln1_b"], wr_hi, wr_lo, p["br"])


def _row_gather(src_hbm, dst, sem, idx_ref, base, n, unroll=8):
    def body(j, carry):
        r = idx_ref[base + j]
        pltpu.make_async_copy(src_hbm.at[pl.ds(pl.multiple_of(r * ROW_TILES, ROW_TILES), ROW_TILES), :],
                              dst.at[pl.ds(j * ROW_TILES, ROW_TILES), :], sem).start()
        return carry
    lax.fori_loop(0, n, body, 0, unroll=unroll)


def _expert_kernel(tstart_ref, ntile_ref, nv_ref, src_ref, x1_hbm, w1_hbm, w3_hbm, w2_hbm, es_hbm,
                   xbuf, obuf, gsem, osem, w1f, w3f, w2f, wsem, w1b, w3b, w2b, *, tm, n_tiles):
    e = pl.program_id(0)
    n_e = ntile_ref[e]
    g0 = tstart_ref[e]
    nv = nv_ref[0]
    rows = tm * ROW_TILES

    def gather_wait(s):
        pltpu.make_async_copy(x1_hbm.at[pl.ds(0, rows), :], xbuf.at[s], gsem.at[s]).wait()

    def out_copy(g):
        return pltpu.make_async_copy(obuf, es_hbm.at[pl.ds(pl.multiple_of(g * rows, rows), rows), :], osem)

    def weight_copies(ex, s):
        return [pltpu.make_async_copy(w_hbm.at[ex], w_f.at[s], wsem.at[s])
                for w_hbm, w_f in ((w1_hbm, w1f), (w3_hbm, w3f), (w2_hbm, w2f))]

    wslot = lax.rem(e, 2)

    @pl.when(e == 0)
    def _():
        for g in range(GATHER_SLOTS - 1):
            _row_gather(x1_hbm, xbuf.at[g], gsem.at[g], src_ref, g * tm, tm)
        for c in weight_copies(0, 0):
            c.start(priority=1)

    @pl.when(e + 1 < pl.num_programs(0))
    def _():
        for c in weight_copies(e + 1, 1 - wslot):
            c.start(priority=1)

    for c in weight_copies(e, wslot):
        c.wait()

    @pl.when(n_e > 0)
    def _():
        w1b[...] = w1f[wslot].astype(BF16)
        w3b[...] = w3f[wslot].astype(BF16)
        w2b[...] = w2f[wslot].astype(BF16)

    def tile(k, carry):
        g = g0 + k
        slot = lax.rem(g, GATHER_SLOTS)
        ahead = lax.rem(g + GATHER_SLOTS - 1, GATHER_SLOTS)
        _row_gather(x1_hbm, xbuf.at[ahead], gsem.at[ahead], src_ref, (g + GATHER_SLOTS - 1) * tm, tm, unroll=True)
        gather_wait(slot)
        xb = _from_token_tiles(xbuf.at[slot]).astype(BF16)
        h1 = jnp.dot(xb, w1b[...], preferred_element_type=F32)
        h3 = jnp.dot(xb, w3b[...], preferred_element_type=F32)
        h = (_silu(h1) * h3).astype(BF16)
        out = jnp.dot(h, w2b[...], preferred_element_type=F32)

        @pl.when(g > 0)
        def _():
            out_copy(g).wait()
        _to_token_tiles(obuf, out)
        out_copy(g).start()
        return carry

    lax.fori_loop(0, n_e, tile, 0)

    @pl.when(e == pl.num_programs(0) - 1)
    def _():
        for extra in range(GATHER_SLOTS - 1):
            gather_wait(lax.rem(nv + extra, GATHER_SLOTS))
        out_copy(0).wait()
        obuf[...] = jnp.zeros_like(obuf)

        def zero_tile(g, carry):
            out_copy(g).start()
            out_copy(g).wait()
            return carry

        lax.fori_loop(nv, n_tiles, zero_tile, 0)


def _experts(x1, tstart, ntile, nv, src, w1, w3, w2, n_tiles, tm):
    any_spec = pl.BlockSpec(memory_space=pl.ANY)
    return pl.pallas_call(
        functools.partial(_expert_kernel, tm=tm, n_tiles=n_tiles),
        grid_spec=pltpu.PrefetchScalarGridSpec(
            num_scalar_prefetch=4,
            grid=(N_EXPERTS,),
            in_specs=[any_spec, any_spec, any_spec, any_spec],
            out_specs=any_spec,
            scratch_shapes=[pltpu.VMEM((GATHER_SLOTS, tm * ROW_TILES, LANES), F32),
                            pltpu.VMEM((tm * ROW_TILES, LANES), F32),
                            pltpu.SemaphoreType.DMA((GATHER_SLOTS,)), pltpu.SemaphoreType.DMA(()),
                            pltpu.VMEM((2, D, D_EXPERT), F32), pltpu.VMEM((2, D, D_EXPERT), F32),
                            pltpu.VMEM((2, D_EXPERT, D), F32), pltpu.SemaphoreType.DMA((2,)),
                            pltpu.VMEM((D, D_EXPERT), BF16), pltpu.VMEM((D, D_EXPERT), BF16),
                            pltpu.VMEM((D_EXPERT, D), BF16)]),
        out_shape=jax.ShapeDtypeStruct((n_tiles * tm * ROW_TILES, LANES), F32),
        compiler_params=pltpu.CompilerParams(
            dimension_semantics=("arbitrary",), vmem_limit_bytes=VMEM_LIMIT, disable_bounds_checks=True),
        name="experts",
    )(tstart, ntile, nv, src, x1, w1, w3, w2)


def _combine_kernel(dest_ref, es_hbm, x1_ref, route_ref, g2_ref, b2_ref, op_ref, os_ref, gbuf, sems,
                    *, tm, n_prompt):
    i = pl.program_id(0)
    n = pl.num_programs(0)
    slot = lax.rem(i, 2)

    def start(step, s):
        for k in range(2):
            _row_gather(es_hbm, gbuf.at[s, k], sems.at[s], dest_ref, (k * n + step) * tm, tm)

    @pl.when(i == 0)
    def _():
        start(0, 0)

    @pl.when(i + 1 < n)
    def _():
        start(i + 1, 1 - slot)

    for k in range(2):
        pltpu.make_async_copy(es_hbm.at[pl.ds(0, tm * ROW_TILES), :], gbuf.at[slot, k], sems.at[slot]).wait()
    route = route_ref[...]
    w0 = route[:, 2:3]
    w1 = route[:, 3:4]
    moe = w0 * _from_token_tiles(gbuf.at[slot, 0]) + w1 * _from_token_tiles(gbuf.at[slot, 1])
    out = _layer_norm(ALPHA * _from_token_tiles(x1_ref) + moe, g2_ref[...], b2_ref[...])

    @pl.when(i < n_prompt)
    def _():
        op_ref[...] = out

    @pl.when(i >= n_prompt)
    def _():
        os_ref[...] = out


def _combine(es, dest_km, x1, route, p, tp):
    t = x1.shape[0] // ROW_TILES
    tm = ROW_TM
    n_p = tp // tm
    return pl.pallas_call(
        functools.partial(_combine_kernel, tm=tm, n_prompt=n_p),
        grid_spec=pltpu.PrefetchScalarGridSpec(
            num_scalar_prefetch=1,
            grid=(t // tm,),
            in_specs=[pl.BlockSpec(memory_space=pl.ANY),
                      pl.BlockSpec((tm * ROW_TILES, LANES), lambda i, d: (i, 0)),
                      pl.BlockSpec((tm, LANES), lambda i, d: (i, 0)),
                      pl.BlockSpec((1, D), lambda i, d: (0, 0)),
                      pl.BlockSpec((1, D), lambda i, d: (0, 0))],
            out_specs=[pl.BlockSpec((tm, D), lambda i, d: (jnp.minimum(i, n_p - 1), 0)),
                       pl.BlockSpec((tm, D), lambda i, d: (jnp.maximum(i - n_p, 0), 0))],
            scratch_shapes=[pltpu.VMEM((2, 2, tm * ROW_TILES, LANES), F32), pltpu.SemaphoreType.DMA((2,))]),
        out_shape=[jax.ShapeDtypeStruct((tp, D), F32), jax.ShapeDtypeStruct((t - tp, D), F32)],
        compiler_params=pltpu.CompilerParams(
            dimension_semantics=("arbitrary",), vmem_limit_bytes=VMEM_LIMIT, disable_bounds_checks=True),
        name="combine",
    )(dest_km, es, x1, route, p["ln2_g"], p["ln2_b"])


def _route_meta(e_flat, tm, n_tiles):
    a = e_flat.shape[0]
    ids = jnp.arange(N_EXPERTS, dtype=jnp.int32)
    onehot = (e_flat[:, None] == ids[None, :]).astype(jnp.int32)
    csum = jnp.cumsum(onehot, axis=0)
    rank = jnp.sum((csum - onehot) * onehot, axis=1)
    counts = csum[-1]
    tiles_per = (counts + tm - 1) // tm
    tile_end = jnp.cumsum(tiles_per)
    tile_start = tile_end - tiles_per
    nv = tile_end[-1]
    dest = tile_start[e_flat] * tm + rank
    src = jnp.zeros((n_tiles * tm,), jnp.int32).at[dest].set(
        jnp.arange(a, dtype=jnp.int32) // 2, unique_indices=True, mode="promise_in_bounds")
    return (tile_start.astype(jnp.int32), tiles_per.astype(jnp.int32), nv.reshape(1).astype(jnp.int32), src,
            dest.astype(jnp.int32))


def kernel(x_prompt, x_sample, state_ssm, state_conv, state_hgrn, w_in, conv_w, conv_b, dt_bias, a_log, d_skip, ssm_norm_g, hgrn_lb_logits, hgrn_norm_g, w_a, w_b, w_out, ln1_g, ln1_b, router_g_w, router_g_b, router_e_w, router_e_b, exp_w1, exp_w3, exp_w2, ln2_g, ln2_b):
    nbp, seqp, _ = x_prompt.shape
    nbs, seqs, _ = x_sample.shape
    tp, ts = nbp * seqp, nbs * seqs
    t = tp + ts
    l = 0
    x_p, x_s = x_prompt.reshape(tp, D), x_sample.reshape(ts, D)

    wt = jnp.transpose(w_in[l])
    o_dt = D + D + 2 * BCW
    o_q = o_dt + SSM_HEADS
    xb, dt_raw = _cast_rows(x_p, x_s, wt, o_dt, MERGE_TM)
    dt_raw = jnp.pad(dt_raw, ((0, 0), (0, LANES - SSM_HEADS)))
    tm = PROJ_TM if t % PROJ_TM == 0 else 1024
    proj_a = _project(xb, wt, 0, o_dt, tm, PROJ_TN, BF16)
    proj_b = _project(xb, wt, o_q, 6 * D, tm, PROJ_TN, BF16)

    lb_all = jnp.cumsum(jax.nn.softmax(hgrn_lb_logits.astype(F32), axis=0), axis=0)
    head_of = np.arange(D) // SSM_P
    pad128 = lambda v: jnp.pad(v, (0, LANES - v.shape[0])).reshape(1, LANES)
    p = {
        "cwx": conv_w[l][:, :D], "cwbc": conv_w[l][:, D:],
        "cbx": conv_b[l][:D].reshape(1, D), "cbbc": conv_b[l][D:].reshape(1, 2 * BCW),
        "dtb": pad128(dt_bias[l]), "alog": a_log[l], "alog_e": pad128(a_log[l]),
        "dskip_e": jnp.repeat(d_skip[l], SSM_P).reshape(1, D),
        "ssm_norm_g": ssm_norm_g[l].reshape(1, D),
        "e64": jnp.asarray(np.arange(LANES)[:, None] == head_of[None, :], dtype=BF16),
        "lb": lb_all[l].reshape(1, D), "hgrn_norm_g": hgrn_norm_g[l].reshape(1, D),
        "w_a": w_a[l].astype(BF16), "w_b": w_b[l].astype(BF16), "w_out": w_out[l].astype(BF16),
        "ln1_g": ln1_g[l].reshape(1, D), "ln1_b": ln1_b[l].reshape(1, D),
        "ln2_g": ln2_g[l].reshape(1, D), "ln2_b": ln2_b[l].reshape(1, D),
        "wr": jnp.pad(jnp.concatenate([router_g_w[l], router_e_w[l]], axis=1),
                      ((0, 0), (0, LANES - N_GROUPS - N_EXPERTS))),
        "br": pad128(jnp.concatenate([router_g_b[l], router_e_b[l]])),
    }

    ya_p, ssm_p, conv_p = _ssd(proj_a, dt_raw, 0, jnp.zeros((nbp, D, SSM_N), F32),
                               jnp.zeros((nbp, CONV_K - 1, D + 2 * BCW), F32), p, nbp, seqp)
    ya_s, ssm_s, conv_s = _ssd(proj_a, dt_raw, tp, state_ssm[l].reshape(nbs, D, SSM_N), state_conv[l],
                               p, nbs, seqs)
    yb_p, hg_p = _gla(proj_b, 0, jnp.zeros((nbp, D, HG_DK), F32), p, nbp, seqp)
    yb_s, hg_s = _gla(proj_b, tp, state_hgrn[l].reshape(nbs, D, HG_DK), p, nbs, seqs)

    merged = _gate(ya_p, ya_s, yb_p, yb_s, proj_b, p)
    x1, route = _merge(merged, x_p, x_s, p)

    e_flat = route[:, :2].astype(jnp.int32).reshape(-1)
    n_tiles = (2 * t) // MOE_TM + N_EXPERTS + GATHER_SLOTS - 1
    tstart, ntile, nv, src, dest = _route_meta(e_flat, MOE_TM, n_tiles)
    es = _experts(x1, tstart, ntile, nv, src, exp_w1[l], exp_w3[l], exp_w2[l], n_tiles, MOE_TM)
    dest_km = dest.reshape(t // ROW_TM, ROW_TM, 2).transpose(2, 0, 1).reshape(-1)
    x2_p, x2_s = _combine(es, dest_km, x1, route, p, tp)

    y_prompt = x2_p.reshape(nbp, seqp, D)
    y_sample = x2_s.reshape(nbs, seqs, D)
    return (y_prompt, y_sample,
            ssm_p.reshape(1, nbp, SSM_HEADS, SSM_P, SSM_N), conv_p[None],
            hg_p.reshape(1, nbp, HG_HEADS, HG_DK, HG_DK),
            ssm_s.reshape(1, nbs, SSM_HEADS, SSM_P, SSM_N), conv_s[None],
            hg_s.reshape(1, nbs, HG_HEADS, HG_DK, HG_DK))
```

```python
import functools

import jax
import jax.numpy as jnp
import numpy as np
from jax import lax
from jax.experimental import pallas as pl
from jax.experimental.pallas import tpu as pltpu

F32 = jnp.float32
BF16 = jnp.bfloat16

D = 2048
SSM_HEADS = 32
SSM_P = 64
SSM_N = 128
SSM_G = 4
GW = D // SSM_G
BCW = SSM_G * SSM_N
CONV_K = 4
HG_HEADS = 16
HG_DK = 128
N_GROUPS = 4
EPG = 8
N_EXPERTS = 32
D_EXPERT = 512
EPS = 1e-5
ALPHA = 2.0 ** 0.25
NEG = -1e30
LOG2E = 1.4426950408889634
LANES = 128
ROW_TILES = D // LANES

VMEM_LIMIT = 56 * 1024 * 1024
SSD_CHUNK = 128
GLA_CHUNK = 64
GLA_SUBCHUNKS = 2
STEP_ROWS = 64
PROJ_TM = 2304
PROJ_TN = 512
MOE_TM = 128
GATHER_SLOTS = 4
ROW_TM = 256
MERGE_TM = 512
GATE_TM = 1024
MERGE_TN = 512
CARRY = 8


def _sigmoid(x):
    return 0.5 * jnp.tanh(0.5 * x) + 0.5


def _silu(x):
    h = 0.5 * x
    return h * jnp.tanh(h) + h


def _softplus(x):
    return jnp.maximum(x, 0.0) + jnp.log(1.0 + jnp.exp(-jnp.abs(x)))


def _split3(x):
    hi = x.astype(BF16)
    r = x - hi.astype(F32)
    mid = r.astype(BF16)
    lo = (r - mid.astype(F32)).astype(BF16)
    return hi, mid, lo


def _dot3_rhs(m_bf16, x):
    return sum(jnp.dot(m_bf16, part, preferred_element_type=F32) for part in _split3(x))


def _dot3_lhs(x, m_bf16):
    return sum(jnp.dot(part, m_bf16, preferred_element_type=F32) for part in _split3(x))


def _seq_masks(r, seq_len):
    shift = seq_len.bit_length() - 1
    row = lax.broadcasted_iota(jnp.int32, (r, r), 0)
    col = lax.broadcasted_iota(jnp.int32, (r, r), 1)
    same = lax.shift_right_logical(row, shift) == lax.shift_right_logical(col, shift)
    return same & (row >= col), same & (row <= col)


def _seq_totals(cum, nseq, seq_len):
    w = cum.shape[1]
    parts = [jnp.broadcast_to(cum[(b + 1) * seq_len - 1:(b + 1) * seq_len, :], (seq_len, w)) for b in range(nseq)]
    return parts[0] if nseq == 1 else jnp.concatenate(parts, axis=0)


def _pad_rows(x, rows):
    if x.shape[0] == rows:
        return x
    return jnp.concatenate([x, jnp.zeros((rows - x.shape[0], x.shape[1]), x.dtype)], axis=0)


def _to_token_tiles(ref, val):
    rows = val.shape[0]
    for s in range(ROW_TILES):
        ref[pl.ds(s, rows, stride=ROW_TILES), :] = val[:, s * LANES:(s + 1) * LANES]


def _from_token_tiles(ref):
    rows = ref.shape[0] // ROW_TILES
    return jnp.concatenate([ref[pl.ds(s, rows, stride=ROW_TILES), :] for s in range(ROW_TILES)], axis=1)


def _row_mask(x, b, seq_len):
    row = lax.broadcasted_iota(jnp.int32, x.shape, 0)
    return jnp.where((row >= b * seq_len) & (row < (b + 1) * seq_len), x, jnp.zeros_like(x))


def _cast_kernel(xp_ref, xs_ref, wdt_ref, o_ref, dt_ref, *, n_prompt):
    xb = jnp.where(pl.program_id(0) < n_prompt, xp_ref[...], xs_ref[...]).astype(BF16)
    o_ref[...] = xb
    dt_ref[...] = lax.dot_general(xb, wdt_ref[...].astype(BF16), (((1,), (1,)), ((), ())),
                                  preferred_element_type=F32)


def _cast_rows(x_p, x_s, wt, dt_row0, tm):
    k = x_p.shape[1]
    n_p = x_p.shape[0] // tm
    m = x_p.shape[0] + x_s.shape[0]
    return pl.pallas_call(
        functools.partial(_cast_kernel, n_prompt=n_p),
        grid=(m // tm,),
        in_specs=[pl.BlockSpec((tm, k), lambda i: (jnp.minimum(i, n_p - 1), 0)),
                  pl.BlockSpec((tm, k), lambda i: (jnp.maximum(i - n_p, 0), 0)),
                  pl.BlockSpec((SSM_HEADS, k), lambda i: (dt_row0 // SSM_HEADS, 0))],
        out_specs=[pl.BlockSpec((tm, k), lambda i: (i, 0)), pl.BlockSpec((tm, SSM_HEADS), lambda i: (i, 0))],
        out_shape=[jax.ShapeDtypeStruct((m, k), BF16), jax.ShapeDtypeStruct((m, SSM_HEADS), F32)],
        compiler_params=pltpu.CompilerParams(dimension_semantics=("parallel",), vmem_limit_bytes=VMEM_LIMIT),
        name="cast",
    )(x_p, x_s, wt)


def _mm_nt_kernel(x_ref, wt_ref, o_ref):
    w = wt_ref[...].astype(BF16)
    o_ref[...] = lax.dot_general(x_ref[...], w, (((1,), (1,)), ((), ())),
                                 preferred_element_type=F32).astype(o_ref.dtype)


def _project(x, wt, row0, n, tm, tn, out_dtype):
    m, k = x.shape
    assert n % tn == 0 and m % tm == 0 and row0 % 8 == 0
    if row0 % tn == 0:
        w_spec = pl.BlockSpec((tn, k), lambda i, j: (row0 // tn + j, 0))
    else:
        w_spec = pl.BlockSpec((pl.Element(tn), pl.Element(k)),
                              lambda i, j: (pl.multiple_of(row0 + j * tn, 8), 0))
    return pl.pallas_call(
        _mm_nt_kernel,
        grid=(m // tm, n // tn),
        in_specs=[pl.BlockSpec((tm, k), lambda i, j: (i, 0)), w_spec],
        out_specs=pl.BlockSpec((tm, tn), lambda i, j: (i, j)),
        out_shape=jax.ShapeDtypeStruct((m, n), out_dtype),
        compiler_params=pltpu.CompilerParams(
            dimension_semantics=("parallel", "arbitrary"), vmem_limit_bytes=VMEM_LIMIT),
        name="proj",
    )(x, wt)


def _conv_silu(buf_ref, carry0_ref, u, w_ref, b_ref, nseq, seq_len, first):
    ch = u.shape[1]

    @pl.when(first)
    def _():
        buf_ref[:, 0:CARRY, :] = carry0_ref[...]
    u3 = u.reshape(nseq, seq_len, ch)
    buf_ref[:, CARRY:CARRY + seq_len, :] = u3
    acc = b_ref[...] + w_ref[CONV_K - 1:CONV_K, :] * u3
    for k in range(CONV_K - 1):
        off = CARRY - (CONV_K - 1) + k
        acc = acc + w_ref[k:k + 1, :] * buf_ref[:, off:off + seq_len, :]
    tail = buf_ref[:, seq_len:seq_len + CARRY, :]
    buf_ref[:, 0:CARRY, :] = tail
    return _silu(acc).reshape(nseq * seq_len, ch), tail


def _pair_cols(tile):
    lane = lax.broadcasted_iota(jnp.int32, tile.shape, 1)
    swapped = pltpu.roll(tile, SSM_P, axis=1)
    return jnp.where(lane < SSM_P, tile, swapped), jnp.where(lane < SSM_P, swapped, tile)


def _ssd_kernel(xp_ref, bcp_ref, z_ref, dtr_ref, dtrt_ref, h0_ref, cx0_ref, cbc0_ref,
                cwx_ref, cwbc_ref, cbx_ref, cbbc_ref, dtb_ref, dtbt_ref, alogt_ref, aloge_ref,
                dskip_ref, normg_ref, e64_ref,
                y_ref, hout_ref, ctx_ref, ctbc_ref,
                xbuf, bcbuf, h_ref, *, nseq, seq_len):
    r = nseq * seq_len
    rp = max(r, LANES)
    zi = pl.program_id(1)
    first = zi == 0

    @pl.when(first)
    def _():
        h_ref[...] = h0_ref[...]

    xs, tail_x = _conv_silu(xbuf, cx0_ref, xp_ref[...].astype(F32), cwx_ref, cbx_ref, nseq, seq_len, first)
    bc, tail_bc = _conv_silu(bcbuf, cbc0_ref, bcp_ref[...].astype(F32), cwbc_ref, cbbc_ref, nseq, seq_len, first)
    ctx_ref[...] = tail_x
    ctbc_ref[...] = tail_bc

    causal, anti = _seq_masks(r, seq_len)
    dt = _softplus(dtr_ref[...] + dtb_ref[...])
    dt_e = _dot3_lhs(dt, e64_ref[...])
    cum = _dot3_rhs(causal.astype(BF16), dt * (-LOG2E * jnp.exp(aloge_ref[...])))
    cum_e = _dot3_lhs(cum, e64_ref[...])
    loga_t = _softplus(dtrt_ref[0] + dtbt_ref[...]) * (-LOG2E * jnp.exp(alogt_ref[...]))
    cum_t = _dot3_lhs(loga_t, anti.astype(BF16))
    tot_e = _seq_totals(cum_e, nseq, seq_len)
    xdt = xs * dt_e
    in_scale = jnp.exp2(cum_e)
    xw = xdt * jnp.exp2(tot_e - cum_e)
    xw_t = _pad_rows(xw, rp).T.astype(BF16)
    lane = lax.broadcasted_iota(jnp.int32, (r, LANES), 1)

    y_groups = []
    for g in range(SSM_G):
        bg = bc[:, g * SSM_N:(g + 1) * SSM_N]
        cg = bc[:, BCW + g * SSM_N:BCW + (g + 1) * SSM_N]
        cb = lax.dot_general(cg, bg, (((1,), (1,)), ((), ())), preferred_element_type=F32)
        rows = slice(g * GW, (g + 1) * GW)
        pieces = []
        for j in range(GW // LANES):
            col0 = g * GW + j * LANES
            xpair = xdt[:, col0:col0 + LANES]
            ms, rhs = [], []
            for half, colb in enumerate(_pair_cols(cum_e[:, col0:col0 + LANES])):
                head = col0 // SSM_P + half
                seg = colb[:, :r] - cum_t[head:head + 1, :]
                ms.append((cb * jnp.exp2(jnp.where(causal, seg, NEG))).astype(BF16))
                rhs.append(jnp.where(lane >= SSM_P if half else lane < SSM_P, xpair, 0.0).astype(BF16))
            if r % LANES == 0:
                pieces.append(jnp.dot(jnp.concatenate(ms, axis=1), jnp.concatenate(rhs, axis=0),
                                      preferred_element_type=F32))
            else:
                pieces.append(jnp.dot(ms[0], rhs[0], preferred_element_type=F32)
                              + jnp.dot(ms[1], rhs[1], preferred_element_type=F32))
        y_intra = jnp.concatenate(pieces, axis=1)
        y_inter = []
        for b in range(nseq):
            rb = slice(b * seq_len, (b + 1) * seq_len)
            hg = h_ref[b, rows, :]
            y_inter.append(lax.dot_general(cg[rb, :], hg, (((1,), (1,)), ((), ())),
                                           preferred_element_type=F32))
            bmask = bg if nseq == 1 else _row_mask(bg, b, seq_len)
            st = jnp.dot(xw_t[rows, :], _pad_rows(bmask, rp).astype(BF16), preferred_element_type=F32)
            dec8 = jnp.exp2(tot_e[b * seq_len:b * seq_len + 8, rows])
            for j in range(GW // LANES):
                for half, dcol in enumerate(_pair_cols(dec8[:, j * LANES:(j + 1) * LANES])):
                    h8 = 2 * j + half
                    hr = slice(g * GW + h8 * SSM_P, g * GW + (h8 + 1) * SSM_P)
                    h_ref[b, hr, :] = dcol[0:1, :] * h_ref[b, hr, :] + st[h8 * SSM_P:(h8 + 1) * SSM_P, :]
        y_inter = y_inter[0] if nseq == 1 else jnp.concatenate(y_inter, axis=0)
        y_groups.append(y_intra + y_inter * in_scale[:, rows])
    y = jnp.concatenate(y_groups, axis=1) + dskip_ref[...] * xs
    yz = y * _silu(z_ref[...].astype(F32))
    outs = []
    for g in range(SSM_G):
        blk = yz[:, g * GW:(g + 1) * GW]
        ms = jnp.sum(blk * blk, axis=1, keepdims=True) * (1.0 / GW)
        outs.append(blk * lax.rsqrt(ms + EPS))
    y_ref[...] = (jnp.concatenate(outs, axis=1) * normg_ref[...]).astype(y_ref.dtype)

    @pl.when(zi == pl.num_programs(1) - 1)
    def _():
        hout_ref[...] = h_ref[...]


def _ssd(proj, dt_raw, row0, h0, conv0, p, nb, seq):
    seq_len = SSD_CHUNK if seq % SSD_CHUNK == 0 else seq
    nseq = 1 if seq_len == SSD_CHUNK else max(1, min(nb, STEP_ROWS // seq_len))
    nz = seq // seq_len
    r = nseq * seq_len
    t = nb * seq
    r0 = row0 // r
    nblk = t // r
    dtrt = dt_raw[row0:row0 + t, :SSM_HEADS].reshape(nblk, r, SSM_HEADS).transpose(0, 2, 1)
    pad_rows = CARRY - (CONV_K - 1)
    cx0 = jnp.pad(conv0[:, :, :D], ((0, 0), (pad_rows, 0), (0, 0)))
    cbc0 = jnp.pad(conv0[:, :, D:], ((0, 0), (pad_rows, 0), (0, 0)))
    rowblk = lambda col: pl.BlockSpec((r, D), lambda b, z: (r0 + b * nz + z, col))
    const = lambda shape: pl.BlockSpec(shape, lambda b, z: tuple(0 for _ in shape))
    per_b = lambda shape: pl.BlockSpec((nseq,) + shape, lambda b, z: (b, 0, 0))
    y, hout, ctx, ctbc = pl.pallas_call(
        functools.partial(_ssd_kernel, nseq=nseq, seq_len=seq_len),
        grid=(nb // nseq, nz),
        in_specs=[rowblk(1),
                  pl.BlockSpec((r, 2 * BCW), lambda b, z: (r0 + b * nz + z, 4)),
                  rowblk(0),
                  pl.BlockSpec((r, LANES), lambda b, z: (r0 + b * nz + z, 0)),
                  pl.BlockSpec((1, SSM_HEADS, r), lambda b, z: (b * nz + z, 0, 0)),
                  per_b((D, SSM_N)), per_b((CARRY, D)), per_b((CARRY, 2 * BCW)),
                  const((CONV_K, D)), const((CONV_K, 2 * BCW)), const((1, D)), const((1, 2 * BCW)),
                  const((1, LANES)), const((SSM_HEADS, r)), const((SSM_HEADS, r)), const((1, LANES)),
                  const((1, D)), const((1, D)), const((LANES, D))],
        out_specs=[pl.BlockSpec((r, D), lambda b, z: (b * nz + z, 0)),
                   per_b((D, SSM_N)), per_b((CARRY, D)), per_b((CARRY, 2 * BCW))],
        out_shape=[jax.ShapeDtypeStruct((t, D), BF16),
                   jax.ShapeDtypeStruct((nb, D, SSM_N), F32),
                   jax.ShapeDtypeStruct((nb, CARRY, D), F32),
                   jax.ShapeDtypeStruct((nb, CARRY, 2 * BCW), F32)],
        scratch_shapes=[pltpu.VMEM((nseq, seq_len + CARRY, D), F32),
                        pltpu.VMEM((nseq, seq_len + CARRY, 2 * BCW), F32),
                        pltpu.VMEM((nseq, D, SSM_N), F32)],
        compiler_params=pltpu.CompilerParams(
            dimension_semantics=("parallel", "arbitrary"), vmem_limit_bytes=VMEM_LIMIT),
        name="ssd",
    )(proj, proj, proj, dt_raw, dtrt, h0, cx0, cbc0,
      p["cwx"], p["cwbc"], p["cbx"], p["cbbc"], p["dtb"],
      jnp.broadcast_to(p["dtb"][0, :SSM_HEADS, None], (SSM_HEADS, r)),
      jnp.broadcast_to(p["alog"][:, None], (SSM_HEADS, r)),
      p["alog_e"], p["dskip_e"], p["ssm_norm_g"], p["e64"])
    conv_new = jnp.concatenate([ctx[:, pad_rows:], ctbc[:, pad_rows:]], axis=-1)
    return y, hout, conv_new


def _gla_kernel(q_ref, f_ref, i_ref, g_ref, s0_ref, lb_ref, normg_ref, y_ref, sout_ref, s_ref,
                *, nseq, seq_len, nsub):
    r = nseq * seq_len
    rp = max(r, LANES)
    zi = pl.program_id(1)

    @pl.when(zi == 0)
    def _():
        s_ref[...] = s0_ref[...]

    lb = lb_ref[...]
    causal, _ = _seq_masks(r, seq_len)
    for sub in range(nsub):
        rows = slice(sub * r, (sub + 1) * r)
        f = lb + (1.0 - lb) * _sigmoid(f_ref[rows, :].astype(F32))
        k = 1.0 - f
        q = _silu(q_ref[rows, :].astype(F32))
        vb = i_ref[rows, :]
        bc = _dot3_rhs(causal.astype(BF16), jnp.log2(f))
        tot = _seq_totals(bc, nseq, seq_len)
        qe = q * jnp.exp2(bc)
        ke = k * jnp.exp2(-bc)
        kd_t = _pad_rows(ke * jnp.exp2(tot), rp).T.astype(BF16)
        gate = _silu(g_ref[rows, :].astype(F32))
        outs = []
        for h in range(HG_HEADS):
            sl = slice(h * HG_DK, (h + 1) * HG_DK)
            qh = qe[:, sl]
            attn = lax.dot_general(qh, ke[:, sl], (((1,), (1,)), ((), ())), preferred_element_type=F32)
            attn = jnp.where(causal, attn, 0.0)
            o = jnp.dot(attn.astype(BF16), vb[:, sl], preferred_element_type=F32)
            o_inter = []
            for b in range(nseq):
                rb = slice(b * seq_len, (b + 1) * seq_len)
                sh = s_ref[b, sl, :]
                o_inter.append(jnp.dot(qh[rb, :], sh, preferred_element_type=F32))
                vmask = vb[:, sl] if nseq == 1 else _row_mask(vb[:, sl], b, seq_len)
                st = jnp.dot(kd_t[sl, :], _pad_rows(vmask, rp), preferred_element_type=F32)
                dec = jnp.exp2(tot[b * seq_len:b * seq_len + 1, sl])
                dec_col = jnp.broadcast_to(dec, (HG_DK, HG_DK)).T
                s_ref[b, sl, :] = dec_col * sh + st
            o = o + (o_inter[0] if nseq == 1 else jnp.concatenate(o_inter, axis=0))
            ms = jnp.sum(o * o, axis=1, keepdims=True) * (1.0 / HG_DK)
            outs.append(o * lax.rsqrt(ms + EPS))
        y_ref[rows, :] = (jnp.concatenate(outs, axis=1) * normg_ref[...] * gate).astype(y_ref.dtype)

    @pl.when(zi == pl.num_programs(1) - 1)
    def _():
        sout_ref[...] = s_ref[...]


def _gla(proj, row0, s0, p, nb, seq):
    seq_len = GLA_CHUNK if seq % GLA_CHUNK == 0 else seq
    nseq = 1 if seq_len == GLA_CHUNK else max(1, min(nb, STEP_ROWS // seq_len))
    nsub = GLA_SUBCHUNKS if nseq == 1 and (seq // seq_len) % GLA_SUBCHUNKS == 0 else 1
    nz = seq // (seq_len * nsub)
    r = nseq * seq_len * nsub
    r0 = row0 // r
    rowblk = lambda col: pl.BlockSpec((r, D), lambda b, z: (r0 + b * nz + z, col))
    const = lambda shape: pl.BlockSpec(shape, lambda b, z: tuple(0 for _ in shape))
    per_b = lambda shape: pl.BlockSpec((nseq,) + shape, lambda b, z: (b, 0, 0))
    return pl.pallas_call(
        functools.partial(_gla_kernel, nseq=nseq, seq_len=seq_len, nsub=nsub),
        grid=(nb // nseq, nz),
        in_specs=[rowblk(0), rowblk(1), rowblk(2), rowblk(3), per_b((D, HG_DK)), const((1, D)), const((1, D))],
        out_specs=[pl.BlockSpec((r, D), lambda b, z: (b * nz + z, 0)), per_b((D, HG_DK))],
        out_shape=[jax.ShapeDtypeStruct((nb * seq, D), BF16), jax.ShapeDtypeStruct((nb, D, HG_DK), F32)],
        scratch_shapes=[pltpu.VMEM((nseq, D, HG_DK), F32)],
        compiler_params=pltpu.CompilerParams(
            dimension_semantics=("parallel", "arbitrary"), vmem_limit_bytes=VMEM_LIMIT),
        name="gla",
    )(proj, proj, proj, proj, s0, p["lb"], p["hgrn_norm_g"])


def _layer_norm(x, g, b):
    mu = jnp.mean(x, axis=1, keepdims=True)
    xc = x - mu
    var = jnp.mean(xc * xc, axis=1, keepdims=True)
    return xc * lax.rsqrt(var + EPS) * g + b


def _gate_kernel(yap_ref, yas_ref, ybp_ref, ybs_ref, ga_ref, gb_ref, wa_ref, wb_ref, o_ref, *, n_prompt):
    is_prompt = pl.program_id(0) < n_prompt
    ya = jnp.where(is_prompt, yap_ref[...], yas_ref[...])
    yb = jnp.where(is_prompt, ybp_ref[...], ybs_ref[...])
    a = jnp.dot(ya, wa_ref[...], preferred_element_type=F32)
    b = jnp.dot(yb, wb_ref[...], preferred_element_type=F32)
    ga = _sigmoid(ga_ref[...].astype(F32))
    gb = _sigmoid(gb_ref[...].astype(F32))
    o_ref[...] = (ga * a + gb * b).astype(o_ref.dtype)


def _gate(ya_p, ya_s, yb_p, yb_s, proj, p):
    tm = GATE_TM if ya_p.shape[0] % GATE_TM == 0 and ya_s.shape[0] % GATE_TM == 0 else MERGE_TM
    tn = MERGE_TN
    n_p = ya_p.shape[0] // tm
    t = ya_p.shape[0] + ya_s.shape[0]
    prow = pl.BlockSpec((tm, D), lambda i, j: (jnp.minimum(i, n_p - 1), 0))
    srow = pl.BlockSpec((tm, D), lambda i, j: (jnp.maximum(i - n_p, 0), 0))
    gcol = lambda seg: pl.BlockSpec((tm, tn), lambda i, j: (i, seg * (D // tn) + j))
    wcol = pl.BlockSpec((D, tn), lambda i, j: (0, j))
    return pl.pallas_call(
        functools.partial(_gate_kernel, n_prompt=n_p),
        grid=(t // tm, D // tn),
        in_specs=[prow, srow, prow, srow, gcol(4), gcol(5), wcol, wcol],
        out_specs=pl.BlockSpec((tm, tn), lambda i, j: (i, j)),
        out_shape=jax.ShapeDtypeStruct((t, D), BF16),
        compiler_params=pltpu.CompilerParams(
            dimension_semantics=("parallel", "arbitrary"), vmem_limit_bytes=VMEM_LIMIT),
        name="gate",
    )(ya_p, ya_s, yb_p, yb_s, proj, proj, p["w_a"], p["w_b"])


def _merge_kernel(m_ref, xp_ref, xs_ref, wo_ref, g1_ref, b1_ref, wrh_ref, wrl_ref, br_ref,
                  x1_ref, route_ref, *, n_prompt):
    x = jnp.where(pl.program_id(0) < n_prompt, xp_ref[...], xs_ref[...])
    mix = jnp.dot(m_ref[...], wo_ref[...], preferred_element_type=F32)
    x1 = _layer_norm(ALPHA * x + mix, g1_ref[...], b1_ref[...])
    _to_token_tiles(x1_ref, x1)
    hi = x1.astype(BF16)
    lo = (x1 - hi.astype(F32)).astype(BF16)
    logits = (jnp.dot(hi, wrh_ref[...], preferred_element_type=F32)
              + jnp.dot(lo, wrh_ref[...], preferred_element_type=F32)
              + jnp.dot(hi, wrl_ref[...], preferred_element_type=F32) + br_ref[...])
    lane = lax.broadcasted_iota(jnp.int32, logits.shape, 1).astype(F32)
    gl = jnp.where(lane < N_GROUPS, logits, NEG)
    gmax = jnp.max(gl, axis=1, keepdims=True)
    gsel = jnp.min(jnp.where(gl == gmax, lane, 1e9), axis=1, keepdims=True)
    p_grp = 1.0 / jnp.sum(jnp.exp(gl - gmax), axis=1, keepdims=True)
    lo = N_GROUPS + gsel * EPG
    el = jnp.where((lane >= lo) & (lane < lo + EPG), logits, NEG)
    v0 = jnp.max(el, axis=1, keepdims=True)
    i0 = jnp.min(jnp.where(el == v0, lane, 1e9), axis=1, keepdims=True)
    el2 = jnp.where(lane == i0, NEG, el)
    v1 = jnp.max(el2, axis=1, keepdims=True)
    i1 = jnp.min(jnp.where(el2 == v1, lane, 1e9), axis=1, keepdims=True)
    e1 = jnp.exp(v1 - v0)
    p0 = 1.0 / (1.0 + e1)
    p1 = e1 * p0
    route = jnp.where(lane == 0, i0 - N_GROUPS,
                      jnp.where(lane == 1, i1 - N_GROUPS,
                                jnp.where(lane == 2, p_grp * p0,
                                          jnp.where(lane == 3, p_grp * p1, 0.0))))
    route_ref[...] = route


def _merge(merged, x_p, x_s, p):
    tm = MERGE_TM
    n_p = x_p.shape[0] // tm
    t = x_p.shape[0] + x_s.shape[0]
    row = pl.BlockSpec((tm, D), lambda i: (i, 0))
    prow = pl.BlockSpec((tm, D), lambda i: (jnp.minimum(i, n_p - 1), 0))
    srow = pl.BlockSpec((tm, D), lambda i: (jnp.maximum(i - n_p, 0), 0))
    res = lambda shape: pl.BlockSpec(shape, lambda i: (0, 0), pipeline_mode=pl.Buffered(1))
    wr_hi = p["wr"].astype(BF16)
    wr_lo = (p["wr"] - wr_hi.astype(F32)).astype(BF16)
    return pl.pallas_call(
        functools.partial(_merge_kernel, n_prompt=n_p),
        grid=(t // tm,),
        in_specs=[row, prow, srow, res((D, D)), res((1, D)), res((1, D)),
                  res((D, LANES)), res((D, LANES)), res((1, LANES))],
        out_specs=[pl.BlockSpec((tm * ROW_TILES, LANES), lambda i: (i, 0)),
                   pl.BlockSpec((tm, LANES), lambda i: (i, 0))],
        out_shape=[jax.ShapeDtypeStruct((t * ROW_TILES, LANES), F32), jax.ShapeDtypeStruct((t, LANES), F32)],
        compiler_params=pltpu.CompilerParams(
            dimension_semantics=("parallel",), vmem_limit_bytes=VMEM_LIMIT),
        name="merge",
    )(merged, x_p, x_s, p["w_out"], p["ln1_g"], p["ln1_b"], wr_hi, wr_lo, p["br"])


def _row_gather(src_hbm, dst, sem, idx_ref, base, n, unroll=8, split_priorities=False):
    def issue(j, priority):
        r = idx_ref[base + j]
        pltpu.make_async_copy(src_hbm.at[pl.ds(pl.multiple_of(r * ROW_TILES, ROW_TILES), ROW_TILES), :],
                              dst.at[pl.ds(j * ROW_TILES, ROW_TILES), :], sem).start(priority=priority)

    if split_priorities:
        def pair(j, carry):
            issue(2 * j, 0)
            issue(2 * j + 1, 1)
            return carry
        lax.fori_loop(0, n // 2, pair, 0, unroll=unroll)
    else:
        def body(j, carry):
            issue(j, 0)
            return carry
        lax.fori_loop(0, n, body, 0, unroll=unroll)


def _expert_kernel(tstart_ref, ntile_ref, nv_ref, src_ref, x1_hbm, w1_hbm, w3_hbm, w2_hbm, es_hbm,
                   xbuf, obuf, gsem, osem, w1f, w3f, w2f, wsem, w1b, w3b, w2b, *, tm, n_tiles):
    e = pl.program_id(0)
    n_e = ntile_ref[e]
    g0 = tstart_ref[e]
    nv = nv_ref[0]
    rows = tm * ROW_TILES

    def gather_wait(s):
        pltpu.make_async_copy(x1_hbm.at[pl.ds(0, rows), :], xbuf.at[s], gsem.at[s]).wait()

    def out_copy(g):
        return pltpu.make_async_copy(obuf, es_hbm.at[pl.ds(pl.multiple_of(g * rows, rows), rows), :], osem)

    def weight_copies(ex, s):
        return [pltpu.make_async_copy(w_hbm.at[ex], w_f.at[s], wsem.at[s])
                for w_hbm, w_f in ((w1_hbm, w1f), (w3_hbm, w3f), (w2_hbm, w2f))]

    wslot = lax.rem(e, 2)

    @pl.when(e == 0)
    def _():
        for g in range(GATHER_SLOTS - 1):
            _row_gather(x1_hbm, xbuf.at[g], gsem.at[g], src_ref, g * tm, tm)
        for c in weight_copies(0, 0):
            c.start(priority=1)

    @pl.when(e + 1 < pl.num_programs(0))
    def _():
        for c in weight_copies(e + 1, 1 - wslot):
            c.start(priority=1)

    for c in weight_copies(e, wslot):
        c.wait()

    @pl.when(n_e > 0)
    def _():
        w1b[...] = w1f[wslot].astype(BF16)
        w3b[...] = w3f[wslot].astype(BF16)
        w2b[...] = w2f[wslot].astype(BF16)

    def tile(k, carry):
        g = g0 + k
        slot = lax.rem(g, GATHER_SLOTS)
        ahead = lax.rem(g + GATHER_SLOTS - 1, GATHER_SLOTS)
        _row_gather(x1_hbm, xbuf.at[ahead], gsem.at[ahead], src_ref, (g + GATHER_SLOTS - 1) * tm, tm, unroll=True)
        gather_wait(slot)
        xb = _from_token_tiles(xbuf.at[slot]).astype(BF16)
        h1 = jnp.dot(xb, w1b[...], preferred_element_type=F32)
        h3 = jnp.dot(xb, w3b[...], preferred_element_type=F32)
        h = (_silu(h1) * h3).astype(BF16)
        out = jnp.dot(h, w2b[...], preferred_element_type=F32)

        @pl.when(g > 0)
        def _():
            out_copy(g).wait()
        _to_token_tiles(obuf, out)
        out_copy(g).start()
        return carry

    lax.fori_loop(0, n_e, tile, 0)

    @pl.when(e == pl.num_programs(0) - 1)
    def _():
        for extra in range(GATHER_SLOTS - 1):
            gather_wait(lax.rem(nv + extra, GATHER_SLOTS))
        out_copy(0).wait()
        obuf[...] = jnp.zeros_like(obuf)

        def zero_tile(g, carry):
            out_copy(g).start()
            out_copy(g).wait()
            return carry

        lax.fori_loop(nv, n_tiles, zero_tile, 0)


def _experts(x1, tstart, ntile, nv, src, w1, w3, w2, n_tiles, tm):
    any_spec = pl.BlockSpec(memory_space=pl.ANY)
    return pl.pallas_call(
        functools.partial(_expert_kernel, tm=tm, n_tiles=n_tiles),
        grid_spec=pltpu.PrefetchScalarGridSpec(
            num_scalar_prefetch=4,
            grid=(N_EXPERTS,),
            in_specs=[any_spec, any_spec, any_spec, any_spec],
            out_specs=any_spec,
            scratch_shapes=[pltpu.VMEM((GATHER_SLOTS, tm * ROW_TILES, LANES), F32),
                            pltpu.VMEM((tm * ROW_TILES, LANES), F32),
                            pltpu.SemaphoreType.DMA((GATHER_SLOTS,)), pltpu.SemaphoreType.DMA(()),
                            pltpu.VMEM((2, D, D_EXPERT), F32), pltpu.VMEM((2, D, D_EXPERT), F32),
                            pltpu.VMEM((2, D_EXPERT, D), F32), pltpu.SemaphoreType.DMA((2,)),
                            pltpu.VMEM((D, D_EXPERT), BF16), pltpu.VMEM((D, D_EXPERT), BF16),
                            pltpu.VMEM((D_EXPERT, D), BF16)]),
        out_shape=jax.ShapeDtypeStruct((n_tiles * tm * ROW_TILES, LANES), F32),
        compiler_params=pltpu.CompilerParams(
            dimension_semantics=("arbitrary",), vmem_limit_bytes=VMEM_LIMIT, disable_bounds_checks=True),
        name="experts",
    )(tstart, ntile, nv, src, x1, w1, w3, w2)


def _combine_kernel(dest_ref, es_hbm, x1_ref, route_ref, g2_ref, b2_ref, op_ref, os_ref, gbuf, sems,
                    *, tm, n_prompt):
    i = pl.program_id(0)
    n = pl.num_programs(0)
    slot = lax.rem(i, 2)

    def start(step, s):
        for k in range(2):
            _row_gather(es_hbm, gbuf.at[s, k], sems.at[s], dest_ref, (k * n + step) * tm, tm, split_priorities=True)

    @pl.when(i == 0)
    def _():
        start(0, 0)

    @pl.when(i + 1 < n)
    def _():
        start(i + 1, 1 - slot)

    for k in range(2):
        pltpu.make_async_copy(es_hbm.at[pl.ds(0, tm * ROW_TILES), :], gbuf.at[slot, k], sems.at[slot]).wait()
    route = route_ref[...]
    w0 = route[:, 2:3]
    w1 = route[:, 3:4]
    moe = w0 * _from_token_tiles(gbuf.at[slot, 0]) + w1 * _from_token_tiles(gbuf.at[slot, 1])
    out = _layer_norm(ALPHA * _from_token_tiles(x1_ref) + moe, g2_ref[...], b2_ref[...])

    @pl.when(i < n_prompt)
    def _():
        op_ref[...] = out

    @pl.when(i >= n_prompt)
    def _():
        os_ref[...] = out


def _combine(es, dest_km, x1, route, p, tp):
    t = x1.shape[0] // ROW_TILES
    tm = ROW_TM
    n_p = tp // tm
    return pl.pallas_call(
        functools.partial(_combine_kernel, tm=tm, n_prompt=n_p),
        grid_spec=pltpu.PrefetchScalarGridSpec(
            num_scalar_prefetch=1,
            grid=(t // tm,),
            in_specs=[pl.BlockSpec(memory_space=pl.ANY),
                      pl.BlockSpec((tm * ROW_TILES, LANES), lambda i, d: (i, 0)),
                      pl.BlockSpec((tm, LANES), lambda i, d: (i, 0)),
                      pl.BlockSpec((1, D), lambda i, d: (0, 0)),
                      pl.BlockSpec((1, D), lambda i, d: (0, 0))],
            out_specs=[pl.BlockSpec((tm, D), lambda i, d: (jnp.minimum(i, n_p - 1), 0)),
                       pl.BlockSpec((tm, D), lambda i, d: (jnp.maximum(i - n_p, 0), 0))],
            scratch_shapes=[pltpu.VMEM((2, 2, tm * ROW_TILES, LANES), F32), pltpu.SemaphoreType.DMA((2,))]),
        out_shape=[jax.ShapeDtypeStruct((tp, D), F32), jax.ShapeDtypeStruct((t - tp, D), F32)],
        compiler_params=pltpu.CompilerParams(
            dimension_semantics=("arbitrary",), vmem_limit_bytes=VMEM_LIMIT, disable_bounds_checks=True),
        name="combine",
    )(dest_km, es, x1, route, p["ln2_g"], p["ln2_b"])


def _route_meta(e_flat, tm, n_tiles):
    a = e_flat.shape[0]
    ids = jnp.arange(N_EXPERTS, dtype=jnp.int32)
    onehot = (e_flat[:, None] == ids[None, :]).astype(jnp.int32)
    csum = jnp.cumsum(onehot, axis=0)
    rank = jnp.sum((csum - onehot) * onehot, axis=1)
    counts = csum[-1]
    tiles_per = (counts + tm - 1) // tm
    tile_end = jnp.cumsum(tiles_per)
    tile_start = tile_end - tiles_per
    nv = tile_end[-1]
    dest = tile_start[e_flat] * tm + rank
    src = jnp.zeros((n_tiles * tm,), jnp.int32).at[dest].set(
        jnp.arange(a, dtype=jnp.int32) // 2, unique_indices=True, mode="promise_in_bounds")
    return (tile_start.astype(jnp.int32), tiles_per.astype(jnp.int32), nv.reshape(1).astype(jnp.int32), src,
            dest.astype(jnp.int32))


def kernel(x_prompt, x_sample, state_ssm, state_conv, state_hgrn, w_in, conv_w, conv_b, dt_bias, a_log, d_skip, ssm_norm_g, hgrn_lb_logits, hgrn_norm_g, w_a, w_b, w_out, ln1_g, ln1_b, router_g_w, router_g_b, router_e_w, router_e_b, exp_w1, exp_w3, exp_w2, ln2_g, ln2_b):
    nbp, seqp, _ = x_prompt.shape
    nbs, seqs, _ = x_sample.shape
    tp, ts = nbp * seqp, nbs * seqs
    t = tp + ts
    l = 0
    x_p, x_s = x_prompt.reshape(tp, D), x_sample.reshape(ts, D)

    wt = jnp.transpose(w_in[l])
    o_dt = D + D + 2 * BCW
    o_q = o_dt + SSM_HEADS
    xb, dt_raw = _cast_rows(x_p, x_s, wt, o_dt, MERGE_TM)
    dt_raw = jnp.pad(dt_raw, ((0, 0), (0, LANES - SSM_HEADS)))
    tm = PROJ_TM if t % PROJ_TM == 0 else 1024
    proj_a = _project(xb, wt, 0, o_dt, tm, PROJ_TN, BF16)
    proj_b = _project(xb, wt, o_q, 6 * D, tm, PROJ_TN, BF16)

    lb_all = jnp.cumsum(jax.nn.softmax(hgrn_lb_logits.astype(F32), axis=0), axis=0)
    head_of = np.arange(D) // SSM_P
    pad128 = lambda v: jnp.pad(v, (0, LANES - v.shape[0])).reshape(1, LANES)
    p = {
        "cwx": conv_w[l][:, :D], "cwbc": conv_w[l][:, D:],
        "cbx": conv_b[l][:D].reshape(1, D), "cbbc": conv_b[l][D:].reshape(1, 2 * BCW),
        "dtb": pad128(dt_bias[l]), "alog": a_log[l], "alog_e": pad128(a_log[l]),
        "dskip_e": jnp.repeat(d_skip[l], SSM_P).reshape(1, D),
        "ssm_norm_g": ssm_norm_g[l].reshape(1, D),
        "e64": jnp.asarray(np.arange(LANES)[:, None] == head_of[None, :], dtype=BF16),
        "lb": lb_all[l].reshape(1, D), "hgrn_norm_g": hgrn_norm_g[l].reshape(1, D),
        "w_a": w_a[l].astype(BF16), "w_b": w_b[l].astype(BF16), "w_out": w_out[l].astype(BF16),
        "ln1_g": ln1_g[l].reshape(1, D), "ln1_b": ln1_b[l].reshape(1, D),
        "ln2_g": ln2_g[l].reshape(1, D), "ln2_b": ln2_b[l].reshape(1, D),
        "wr": jnp.pad(jnp.concatenate([router_g_w[l], router_e_w[l]], axis=1),
                      ((0, 0), (0, LANES - N_GROUPS - N_EXPERTS))),
        "br": pad128(jnp.concatenate([router_g_b[l], router_e_b[l]])),
    }

    ya_p, ssm_p, conv_p = _ssd(proj_a, dt_raw, 0, jnp.zeros((nbp, D, SSM_N), F32),
                               jnp.zeros((nbp, CONV_K - 1, D + 2 * BCW), F32), p, nbp, seqp)
    ya_s, ssm_s, conv_s = _ssd(proj_a, dt_raw, tp, state_ssm[l].reshape(nbs, D, SSM_N), state_conv[l],
                               p, nbs, seqs)
    yb_p, hg_p = _gla(proj_b, 0, jnp.zeros((nbp, D, HG_DK), F32), p, nbp, seqp)
    yb_s, hg_s = _gla(proj_b, tp, state_hgrn[l].reshape(nbs, D, HG_DK), p, nbs, seqs)

    merged = _gate(ya_p, ya_s, yb_p, yb_s, proj_b, p)
    x1, route = _merge(merged, x_p, x_s, p)

    e_flat = route[:, :2].astype(jnp.int32).reshape(-1)
    n_tiles = (2 * t) // MOE_TM + N_EXPERTS + GATHER_SLOTS - 1
    tstart, ntile, nv, src, dest = _route_meta(e_flat, MOE_TM, n_tiles)
    es = _experts(x1, tstart, ntile, nv, src, exp_w1[l], exp_w3[l], exp_w2[l], n_tiles, MOE_TM)
    dest_km = dest.reshape(t // ROW_TM, ROW_TM, 2).transpose(2, 0, 1).reshape(-1)
    x2_p, x2_s = _combine(es, dest_km, x1, route, p, tp)

    y_prompt = x2_p.reshape(nbp, seqp, D)
    y_sample = x2_s.reshape(nbs, seqs, D)
    return (y_prompt, y_sample,
            ssm_p.reshape(1, nbp, SSM_HEADS, SSM_P, SSM_N), conv_p[None],
            hg_p.reshape(1, nbp, HG_HEADS, HG_DK, HG_DK),
            ssm_s.reshape(1, nbs, SSM_HEADS, SSM_P, SSM_N), conv_s[None],
            hg_s.reshape(1, nbs, HG_HEADS, HG_DK, HG_DK))
```
